```python
import math
import jax
import jax.numpy as jnp
from jax import lax
import numpy as np

D_MODEL = 1024
BATCH = 1
SEQ = 16384
DEPTH = 1
DEC_BATCH = 128
DEC_SEQ = 4
PAST_LEN = 16384
PAGE_SIZE = 128

HEAD_DIM = 64
A_HEADS = 8
A_KV_HEADS = 2
A_WINDOW = 128
B_KV_HEADS = 4
B_GROUPS = ((128, 1), (512, 4), (2048, 16))
B_HEADS = B_KV_HEADS * len(B_GROUPS)
B_WINDOW_MAX = max(w for w, _ in B_GROUPS)
B_QBLOCK = 128
MEM_TOKENS = 256
M_HEADS = 4
M_HEAD_DIM = 128
REL_BUCKETS = 32
REL_MAX_EXACT = REL_BUCKETS // 2
REL_MAX_DISTANCE = B_WINDOW_MAX
REL_HEADS = A_HEADS + B_HEADS
N_BRANCHES = 3
IN_SPLITS = (A_HEADS * HEAD_DIM, A_KV_HEADS * HEAD_DIM, A_KV_HEADS * HEAD_DIM,
             B_HEADS * HEAD_DIM, B_KV_HEADS * HEAD_DIM, B_KV_HEADS * HEAD_DIM,
             M_HEADS * M_HEAD_DIM, N_BRANCHES * D_MODEL)
IN_OFFSETS = tuple(int(o) for o in np.cumsum(IN_SPLITS)[:-1])
IN_WIDTH = sum(IN_SPLITS)
N_EXPERTS = 32
TOP_K = 4
D_FF = D_MODEL
SWIGLU_LIMIT = 7.0
SWIGLU_ALPHA = 1.702
EXPERT_BLOCK = 128
LN_EPS = 1e-5
DEEPNORM_ALPHA = (2 * DEPTH) ** 0.25
DEEPNORM_BETA = (8 * DEPTH) ** -0.25

kernel_name = 'hybrid_swa_dilated_memory_moe_step'


def layer_norm(x, g, b):
    xf = x.astype(jnp.float32)
    mu = xf.mean(-1, keepdims=True)
    var = jnp.square(xf - mu).mean(-1, keepdims=True)
    y = (xf - mu) * lax.rsqrt(var + LN_EPS) * g.astype(jnp.float32) + b.astype(jnp.float32)
    return y.astype(x.dtype)


def t5_bucket(dist):
    d = jnp.maximum(dist, 0)
    d_large = jnp.maximum(d, REL_MAX_EXACT).astype(jnp.float32)
    large = REL_MAX_EXACT + (jnp.log(d_large / REL_MAX_EXACT) / math.log(REL_MAX_DISTANCE / REL_MAX_EXACT)
                             * (REL_BUCKETS - REL_MAX_EXACT)).astype(jnp.int32)
    return jnp.where(d < REL_MAX_EXACT, d, jnp.minimum(large, REL_BUCKETS - 1))


def split_projection(x, w_in):
    bsz, n = x.shape[:2]
    qa, ka, va, qb, kb, vb, qm, gates = jnp.split(x @ w_in, IN_OFFSETS, axis=-1)
    return (qa.reshape(bsz, n, A_HEADS, HEAD_DIM), ka.reshape(bsz, n, A_KV_HEADS, HEAD_DIM),
            va.reshape(bsz, n, A_KV_HEADS, HEAD_DIM), qb.reshape(bsz, n, B_HEADS, HEAD_DIM),
            kb.reshape(bsz, n, B_KV_HEADS, HEAD_DIM), vb.reshape(bsz, n, B_KV_HEADS, HEAD_DIM),
            qm.reshape(bsz, n, M_HEADS, M_HEAD_DIM), gates)


def sink_window_attention(q, k, v, dist, valid, rel_bias, sinks):
    bsz, nblk, qlen = q.shape[:3]
    rep = A_HEADS // A_KV_HEADS
    qg = q.reshape(bsz, nblk, qlen, A_KV_HEADS, rep, HEAD_DIM)
    s = jnp.einsum('bnqgrd,bnkgd->bngrqk', qg, k).astype(jnp.float32) * HEAD_DIM ** -0.5
    bias = rel_bias[:, :A_HEADS][t5_bucket(dist)].astype(jnp.float32)
    s = s + jnp.moveaxis(bias, -1, 0).reshape(A_KV_HEADS, rep, *dist.shape)
    s = jnp.where(valid[:, None, None], s, -jnp.inf)
    sink = sinks.astype(jnp.float32).reshape(A_KV_HEADS, rep, 1, 1)
    m = jnp.maximum(s.max(-1, keepdims=True), sink)
    p = jnp.exp(s - m)
    p = p / (p.sum(-1, keepdims=True) + jnp.exp(sink - m))
    o = jnp.einsum('bngrqk,bnkgd->bnqgrd', p.astype(v.dtype), v)
    return o.reshape(bsz, nblk, qlen, A_HEADS * HEAD_DIM)


def dilated_attention(q, k, v, qpos, rel_bias):
    outs, lses = [], []
    for g, (window, dil) in enumerate(B_GROUPS):
        dist = dil * jnp.arange(window // dil + 1, dtype=jnp.int32)
        idx = qpos[:, None] - dist[None, :]
        valid = idx >= 0
        kg = jnp.take(k, jnp.maximum(idx, 0), axis=1)
        vg = jnp.take(v, jnp.maximum(idx, 0), axis=1)
        qg = q[:, :, g * B_KV_HEADS:(g + 1) * B_KV_HEADS]
        lo = A_HEADS + g * B_KV_HEADS
        bias = rel_bias[:, lo:lo + B_KV_HEADS][t5_bucket(dist)].astype(jnp.float32).T
        s = jnp.einsum('bqhd,bqjhd->bqhj', qg, kg).astype(jnp.float32) * HEAD_DIM ** -0.5 + bias
        s = jnp.where(valid[None, :, None, :], s, -jnp.inf)
        lse = jax.nn.logsumexp(s, axis=-1)
        p = jnp.exp(s - lse[..., None])
        outs.append(jnp.einsum('bqhj,bqjhd->bqhd', p.astype(v.dtype), vg).astype(jnp.float32))
        lses.append(lse)
    w = jax.nn.softmax(jnp.stack(lses), axis=0)
    o = (w[..., None] * jnp.stack(outs)).sum(0)
    return o.reshape(q.shape[0], q.shape[1], B_KV_HEADS * HEAD_DIM).astype(q.dtype)


def memory_kv(mem, w_mem_kv):
    bsz, n = mem.shape[:2]
    kv = (mem @ w_mem_kv).reshape(bsz, n, 2, M_HEADS, M_HEAD_DIM)
    return kv[:, :, 0], kv[:, :, 1]


def memory_attention(qm, mk, mv):
    s = jnp.einsum('bthd,bmhd->bhtm', qm, mk).astype(jnp.float32) * M_HEAD_DIM ** -0.5
    p = jax.nn.softmax(s, axis=-1)
    o = jnp.einsum('bhtm,bmhd->bthd', p.astype(mv.dtype), mv)
    return o.reshape(qm.shape[0], qm.shape[1], M_HEADS * M_HEAD_DIM)


def merge_branches(oa, ob, om, gates, w_br_a, w_br_b, w_br_m, w_o):
    g = jax.nn.sigmoid(gates.astype(jnp.float32)).astype(oa.dtype)
    ga, gb, gm = jnp.split(g, N_BRANCHES, axis=-1)
    u = ga * (oa @ w_br_a) + gb * (ob @ w_br_b) + gm * (om @ w_br_m)
    return u @ w_o


def moe_ffn(x, w_router, b_router, w_gu, b_gu, w_down, b_down):
    lead = x.shape[:-1]
    xt = x.reshape(-1, D_MODEL)
    n_tok = xt.shape[0]
    logits = (xt @ w_router).astype(jnp.float32) + b_router.astype(jnp.float32)
    top_val, top_idx = lax.top_k(logits, TOP_K)
    gate = jax.nn.softmax(top_val, axis=-1)
    n_assign = n_tok * TOP_K
    flat_e = top_idx.reshape(-1)
    order = jnp.argsort(flat_e)
    sorted_e = flat_e[order]
    counts = jnp.bincount(flat_e, length=N_EXPERTS)
    padded = (counts + EXPERT_BLOCK - 1) // EXPERT_BLOCK * EXPERT_BLOCK
    grp_start = jnp.cumsum(counts) - counts
    pad_end = jnp.cumsum(padded)
    pad_start = pad_end - padded
    dest = pad_start[sorted_e] + jnp.arange(n_assign) - grp_start[sorted_e]
    n_blocks = -(-n_assign // EXPERT_BLOCK) + N_EXPERTS
    n_rows = n_blocks * EXPERT_BLOCK
    row_tok = jnp.full((n_rows,), n_tok, jnp.int32).at[dest].set((order // TOP_K).astype(jnp.int32))
    row_gate = jnp.zeros((n_rows,), jnp.float32).at[dest].set(gate.reshape(-1)[order])
    block_expert = jnp.minimum(
        jnp.searchsorted(pad_end, jnp.arange(n_blocks) * EXPERT_BLOCK, side='right'), N_EXPERTS - 1)
    x_rows = jnp.concatenate([xt, jnp.zeros((1, D_MODEL), xt.dtype)])[row_tok]
    x_rows = x_rows.reshape(n_blocks, EXPERT_BLOCK, D_MODEL)

    def expert_block(args):
        xb, e = args
        gu = xb @ w_gu[e] + b_gu[e]
        g = jnp.minimum(gu[:, :D_FF], SWIGLU_LIMIT)
        u = jnp.clip(gu[:, D_FF:], -SWIGLU_LIMIT, SWIGLU_LIMIT)
        h = (u + 1.0) * g * jax.nn.sigmoid(SWIGLU_ALPHA * g)
        return h @ w_down[e] + b_down[e]

    y_rows = lax.map(expert_block, (x_rows, block_expert)).reshape(n_rows, D_MODEL)
    y = jnp.zeros((n_tok + 1, D_MODEL), y_rows.dtype).at[row_tok].add(
        y_rows * row_gate[:, None].astype(y_rows.dtype))
    return y[:n_tok].reshape(*lead, D_MODEL)


def finish_layer(x, mixed, ln1_g, ln1_b, ln2_g, ln2_b, w_router, b_router, w_gu, b_gu, w_down, b_down):
    h = layer_norm(DEEPNORM_ALPHA * x + mixed, ln1_g, ln1_b)
    f = moe_ffn(h, w_router, b_router, w_gu, b_gu, w_down, b_down)
    return layer_norm(DEEPNORM_ALPHA * h + f, ln2_g, ln2_b)


def prompt_layer(x, mem, rel_bias, params):
    (sinks_a, w_in, w_mem_kv, w_br_a, w_br_b, w_br_m, w_o, ln1_g, ln1_b, ln2_g, ln2_b,
     w_router, b_router, w_gu, b_gu, w_down, b_down) = params
    bsz, seq, _ = x.shape
    qa, ka, va, qb, kb, vb, qm, gates = split_projection(x, w_in)
    nblk = seq // A_WINDOW

    def band(t):
        t = t.reshape(bsz, nblk, A_WINDOW, t.shape[2], HEAD_DIM)
        prev = jnp.pad(t[:, :-1], ((0, 0), (1, 0), (0, 0), (0, 0), (0, 0)))
        return jnp.concatenate([prev, t], axis=2)

    qi = jnp.arange(A_WINDOW)[:, None]
    kj = jnp.arange(2 * A_WINDOW)[None, :]
    dist = A_WINDOW + qi - kj
    valid = (dist >= 0) & (dist < A_WINDOW) & ((jnp.arange(nblk)[:, None, None] > 0) | (kj >= A_WINDOW))
    oa = sink_window_attention(qa.reshape(bsz, nblk, A_WINDOW, A_HEADS, HEAD_DIM), band(ka), band(va),
                               dist, valid, rel_bias, sinks_a).reshape(bsz, seq, A_HEADS * HEAD_DIM)
    nq = seq // B_QBLOCK
    qpos = jnp.arange(seq, dtype=jnp.int32).reshape(nq, B_QBLOCK)
    qblocks = jnp.moveaxis(qb.reshape(bsz, nq, B_QBLOCK, B_HEADS, HEAD_DIM), 1, 0)
    ob = lax.map(lambda a: dilated_attention(a[0], kb, vb, a[1], rel_bias), (qblocks, qpos))
    ob = jnp.moveaxis(ob, 0, 1).reshape(bsz, seq, B_KV_HEADS * HEAD_DIM)
    mk, mv = memory_kv(mem, w_mem_kv)
    om = memory_attention(qm, mk, mv)
    mixed = merge_branches(oa, ob, om, gates, w_br_a, w_br_b, w_br_m, w_o)
    y = finish_layer(x, mixed, ln1_g, ln1_b, ln2_g, ln2_b, w_router, b_router, w_gu, b_gu, w_down, b_down)
    la = min(A_WINDOW, seq)
    lb = min(B_WINDOW_MAX, seq)
    return y, ka[:, seq - la:], va[:, seq - la:], kb[:, seq - lb:], vb[:, seq - lb:], mk, mv


def sample_layer(x, ca_k, ca_v, cb_k, cb_v, cm_k, cm_v, rel_bias, params):
    (sinks_a, w_in, w_mem_kv, w_br_a, w_br_b, w_br_m, w_o, ln1_g, ln1_b, ln2_g, ln2_b,
     w_router, b_router, w_gu, b_gu, w_down, b_down) = params
    bsz, n_new, _ = x.shape
    qa, ka, va, qb, kb, vb, qm, gates = split_projection(x, w_in)
    la = ca_k.shape[1]
    keys_a = jnp.concatenate([ca_k, ka], axis=1)[:, None]
    vals_a = jnp.concatenate([ca_v, va], axis=1)[:, None]
    qi = jnp.arange(n_new)[:, None]
    kj = jnp.arange(la + n_new)[None, :]
    dist = la + qi - kj
    valid = ((dist >= 0) & (dist < A_WINDOW))[None]
    oa = sink_window_attention(qa[:, None], keys_a, vals_a, dist, valid, rel_bias, sinks_a)[:, 0]
    lb = cb_k.shape[1]
    ob = dilated_attention(qb, jnp.concatenate([cb_k, kb], axis=1), jnp.concatenate([cb_v, vb], axis=1),
                           lb + jnp.arange(n_new, dtype=jnp.int32), rel_bias)
    om = memory_attention(qm, cm_k, cm_v)
    mixed = merge_branches(oa, ob, om, gates, w_br_a, w_br_b, w_br_m, w_o)
    y = finish_layer(x, mixed, ln1_g, ln1_b, ln2_g, ln2_b, w_router, b_router, w_gu, b_gu, w_down, b_down)
    return y, ka, va, kb, vb


def setup_inputs(seed: int = 0) -> dict:
    key = jax.random.key(seed)
    ks = jax.random.split(key, 32)
    f32 = jnp.float32
    beta = DEEPNORM_BETA
    la = min(A_WINDOW, PAST_LEN)
    lb = min(B_WINDOW_MAX, PAST_LEN)

    def nrm(i, shape, scale):
        return jax.random.normal(ks[i], shape, f32) * scale

    col_scales = (1.0, 1.0, beta, 1.0, 1.0, beta, 1.0, 1.0)
    in_scale = jnp.concatenate([jnp.full((w,), s, f32) for w, s in zip(IN_SPLITS, col_scales)]) * D_MODEL ** -0.5
    mw = M_HEADS * M_HEAD_DIM
    mem_scale = jnp.concatenate([jnp.ones((mw,), f32), jnp.full((mw,), beta, f32)]) * D_MODEL ** -0.5
    return {
        'x_prompt': nrm(0, (BATCH, SEQ, D_MODEL), 1.0),
        'x_sample': nrm(1, (DEC_BATCH, DEC_SEQ, D_MODEL), 1.0),
        'cache_a_k': nrm(2, (DEPTH, DEC_BATCH, la, A_KV_HEADS, HEAD_DIM), 1.0),
        'cache_a_v': nrm(3, (DEPTH, DEC_BATCH, la, A_KV_HEADS, HEAD_DIM), beta),
        'cache_b_k': nrm(4, (DEPTH, DEC_BATCH, lb, B_KV_HEADS, HEAD_DIM), 1.0),
        'cache_b_v': nrm(5, (DEPTH, DEC_BATCH, lb, B_KV_HEADS, HEAD_DIM), beta),
        'cache_mem_k': nrm(6, (DEPTH, DEC_BATCH, MEM_TOKENS, M_HEADS, M_HEAD_DIM), 1.0),
        'cache_mem_v': nrm(7, (DEPTH, DEC_BATCH, MEM_TOKENS, M_HEADS, M_HEAD_DIM), beta),
        'mem_prompt': nrm(8, (BATCH, MEM_TOKENS, D_MODEL), 1.0),
        'rel_bias': nrm(9, (REL_BUCKETS, REL_HEADS), 0.2),
        'sinks_a': nrm(10, (DEPTH, A_HEADS), 0.5),
        'w_in': nrm(11, (DEPTH, D_MODEL, IN_WIDTH), 1.0) * in_scale,
        'w_mem_kv': nrm(12, (DEPTH, D_MODEL, 2 * mw), 1.0) * mem_scale,
        'w_br_a': nrm(13, (DEPTH, A_HEADS * HEAD_DIM, D_MODEL), beta * (A_HEADS * HEAD_DIM) ** -0.5),
        'w_br_b': nrm(14, (DEPTH, B_KV_HEADS * HEAD_DIM, D_MODEL), beta * (B_KV_HEADS * HEAD_DIM) ** -0.5),
        'w_br_m': nrm(15, (DEPTH, mw, D_MODEL), beta * mw ** -0.5),
        'w_o': nrm(16, (DEPTH, D_MODEL, D_MODEL), beta * D_MODEL ** -0.5),
        'ln1_g': 1.0 + nrm(17, (DEPTH, D_MODEL), 0.05),
        'ln1_b': nrm(18, (DEPTH, D_MODEL), 0.05),
        'ln2_g': 1.0 + nrm(19, (DEPTH, D_MODEL), 0.05),
        'ln2_b': nrm(20, (DEPTH, D_MODEL), 0.05),
        'w_router': nrm(21, (DEPTH, D_MODEL, N_EXPERTS), D_MODEL ** -0.5),
        'b_router': nrm(22, (DEPTH, N_EXPERTS), 0.01),
        'w_gu': nrm(23, (DEPTH, N_EXPERTS, D_MODEL, 2 * D_FF), beta * D_MODEL ** -0.5),
        'b_gu': nrm(24, (DEPTH, N_EXPERTS, 2 * D_FF), 0.02),
        'w_down': nrm(25, (DEPTH, N_EXPERTS, D_FF, D_MODEL), beta * D_FF ** -0.5),
        'b_down': nrm(26, (DEPTH, N_EXPERTS, D_MODEL), 0.02),
    }


def reference(x_prompt, x_sample, cache_a_k, cache_a_v, cache_b_k, cache_b_v, cache_mem_k, cache_mem_v,
              mem_prompt, rel_bias, sinks_a, w_in, w_mem_kv, w_br_a, w_br_b, w_br_m, w_o,
              ln1_g, ln1_b, ln2_g, ln2_b, w_router, b_router, w_gu, b_gu, w_down, b_down):
    y_prompt, y_sample = x_prompt, x_sample
    pa_k, pa_v, pb_k, pb_v, pm_k, pm_v = [], [], [], [], [], []
    sa_k, sa_v, sb_k, sb_v = [], [], [], []
    for l in range(DEPTH):
        params = (sinks_a[l], w_in[l], w_mem_kv[l], w_br_a[l], w_br_b[l], w_br_m[l], w_o[l],
                  ln1_g[l], ln1_b[l], ln2_g[l], ln2_b[l], w_router[l], b_router[l],
                  w_gu[l], b_gu[l], w_down[l], b_down[l])
        y_prompt, a_k, a_v, b_k, b_v, m_k, m_v = prompt_layer(y_prompt, mem_prompt, rel_bias, params)
        pa_k.append(a_k)
        pa_v.append(a_v)
        pb_k.append(b_k)
        pb_v.append(b_v)
        pm_k.append(m_k)
        pm_v.append(m_v)
        y_sample, a_k, a_v, b_k, b_v = sample_layer(y_sample, cache_a_k[l], cache_a_v[l], cache_b_k[l],
                                                     cache_b_v[l], cache_mem_k[l], cache_mem_v[l],
                                                     rel_bias, params)
        sa_k.append(a_k)
        sa_v.append(a_v)
        sb_k.append(b_k)
        sb_v.append(b_v)
    return (y_prompt, y_sample,
            jnp.stack(pa_k), jnp.stack(pa_v), jnp.stack(pb_k), jnp.stack(pb_v), jnp.stack(pm_k), jnp.stack(pm_v),
            jnp.stack(sa_k), jnp.stack(sa_v), jnp.stack(sb_k), jnp.stack(sb_v))
```

```python
import functools
import math

import numpy as np
import jax
import jax.numpy as jnp
from jax import lax
from jax.experimental import pallas as pl
from jax.experimental.pallas import tpu as pltpu

F32 = jnp.float32
BF16 = jnp.bfloat16
I32 = jnp.int32

D_MODEL = 1024
HEAD_DIM = 64
A_HEADS = 8
A_WINDOW = 128
B_KV_HEADS = 4
B_GROUPS = ((128, 1), (512, 4), (2048, 16))
B_WINDOW_MAX = 2048
MEM_TOKENS = 256
M_HEADS = 4
M_HEAD_DIM = 128
REL_BUCKETS = 32
REL_MAX_EXACT = REL_BUCKETS // 2
REL_MAX_DISTANCE = B_WINDOW_MAX
N_EXPERTS = 32
TOP_K = 4
D_FF = D_MODEL
SWIGLU_LIMIT = 7.0
SWIGLU_ALPHA = 1.702
LN_EPS = 1e-5
DEPTH = 1
DEEPNORM_ALPHA = (2 * DEPTH) ** 0.25

LANES = 128
QBLK = 128
SAMPLE_ROWS = 8
EXPERT_ROWS = 512
VMEM_LIMIT = 56 * 1024 * 1024
NEG_INF = float("-inf")

PROJ_SLABS = (6, 4, 4, 4, 4)


def _cparams(sem):
    return pltpu.CompilerParams(dimension_semantics=sem, vmem_limit_bytes=VMEM_LIMIT)


def _dot(a, b):
    return jnp.dot(a, b, preferred_element_type=F32)


def _dot_t(a, b):
    return lax.dot_general(a, b, (((1,), (1,)), ((), ())), preferred_element_type=F32)


def _lo_mask(rows):
    return lax.broadcasted_iota(I32, (rows, LANES), 1) < HEAD_DIM


def _proj_body(x_ref, w_ref, *o_refs):
    x = x_ref[...].astype(BF16)
    col = 0
    for o_ref in o_refs:
        n = o_ref.shape[0] * LANES
        acc = _dot(x, w_ref[:, col:col + n])
        for s in range(o_ref.shape[0]):
            o_ref[s] = acc[:, s * LANES:(s + 1) * LANES]
        col += n


def _project(x, w_bf, tm):
    rows = x.shape[0]
    n_cols = w_bf.shape[1]
    assert rows % tm == 0 and n_cols == sum(PROJ_SLABS) * LANES
    return pl.pallas_call(
        _proj_body,
        grid=(rows // tm,),
        in_specs=[pl.BlockSpec((tm, D_MODEL), lambda i: (i, 0)),
                  pl.BlockSpec((D_MODEL, n_cols), lambda i: (0, 0))],
        out_specs=[pl.BlockSpec((n, tm, LANES), lambda i: (0, i, 0)) for n in PROJ_SLABS],
        out_shape=[jax.ShapeDtypeStruct((n, rows, LANES), F32) for n in PROJ_SLABS],
        compiler_params=_cparams(("parallel",)),
        name="in_proj",
    )(x, w_bf)


def _matmul_body(x_ref, w_ref, o_ref):
    o_ref[...] = _dot(x_ref[...].astype(BF16), w_ref[...])


def _matmul_f32(x, w_bf):
    return pl.pallas_call(
        _matmul_body,
        out_shape=jax.ShapeDtypeStruct((x.shape[0], w_bf.shape[1]), F32),
        compiler_params=_cparams(None),
        name="mem_kv_proj",
    )(x, w_bf)


def _softmax_sink(s, sink):
    m = jnp.maximum(jnp.max(s, axis=-1, keepdims=True), sink)
    p = jnp.exp(s - m)
    denom = jnp.sum(p, axis=-1, keepdims=True) + jnp.exp(sink - m)
    return p / denom


def _softmax_lse(s):
    m = jnp.max(s, axis=-1, keepdims=True)
    p = jnp.exp(s - m)
    l = jnp.sum(p, axis=-1, keepdims=True)
    return p / l, m + jnp.log(l)


def _attn_a_prompt_body(q_ref, kvc_ref, kvp_ref, bias_ref, sink_ref, o_ref, *, nsub):
    first = pl.program_id(0) == 0
    lo = _lo_mask(QBLK)
    col = lax.broadcasted_iota(I32, (4 * QBLK, 2 * QBLK), 1)
    for j in range(nsub):
        rows = slice(j * QBLK, (j + 1) * QBLK)

        def keys(slab):
            cur = kvc_ref[slab, rows, :]
            prev = kvp_ref[slab] if j == 0 else kvc_ref[slab, (j - 1) * QBLK:j * QBLK, :]
            return jnp.concatenate([prev, cur], axis=0).astype(BF16)

        q = [q_ref[p, rows, :] for p in range(4)]
        zero = jnp.zeros_like(q[0])
        q_sets = (
            jnp.concatenate([jnp.where(lo, q[0], zero), jnp.where(lo, q[1], zero),
                             jnp.where(lo, zero, q[2]), jnp.where(lo, zero, q[3])], axis=0).astype(BF16),
            jnp.concatenate([jnp.where(lo, zero, q[0]), jnp.where(lo, zero, q[1]),
                             jnp.where(lo, q[2], zero), jnp.where(lo, q[3], zero)], axis=0).astype(BF16),
        )
        outs = []
        for st in range(2):
            s = _dot_t(q_sets[st], keys(st)) * (HEAD_DIM ** -0.5) + bias_ref[st]
            if j == 0:
                s = jnp.where(jnp.logical_and(first, col < QBLK), NEG_INF, s)
            p = _softmax_sink(s, sink_ref[st])
            outs.append(_dot(p.astype(BF16), keys(2 + st)))
        o1, o2 = outs
        blk = lambda o, r: o[r * QBLK:(r + 1) * QBLK]
        slabs = (jnp.where(lo, blk(o1, 0), blk(o2, 0)), jnp.where(lo, blk(o1, 1), blk(o2, 1)),
                 jnp.where(lo, blk(o2, 2), blk(o1, 2)), jnp.where(lo, blk(o2, 3), blk(o1, 3)))
        for p in range(4):
            o_ref[rows, p * LANES:(p + 1) * LANES] = slabs[p].astype(o_ref.dtype)


def _attn_a_prompt(qa, kva, bias, sink, tq):
    seq = qa.shape[1]
    nsub = tq // QBLK
    return pl.pallas_call(
        functools.partial(_attn_a_prompt_body, nsub=nsub),
        grid=(seq // tq,),
        in_specs=[pl.BlockSpec((4, tq, LANES), lambda n: (0, n, 0)),
                  pl.BlockSpec((4, tq, LANES), lambda n: (0, n, 0)),
                  pl.BlockSpec((4, QBLK, LANES), lambda n: (0, jnp.maximum(n * nsub - 1, 0), 0)),
                  pl.BlockSpec((2, 4 * QBLK, 2 * QBLK), lambda n: (0, 0, 0)),
                  pl.BlockSpec((2, 4 * QBLK, 1), lambda n: (0, 0, 0))],
        out_specs=pl.BlockSpec((tq, 4 * LANES), lambda n: (n, 0)),
        out_shape=jax.ShapeDtypeStruct((seq, 4 * LANES), BF16),
        compiler_params=_cparams(("parallel",)),
        name="attn_a_prompt",
    )(qa, kva, kva, bias, sink)


def _attn_b_prompt_body(q_ref, kvc_ref, kvp_ref, bias_ref, o_ref, kv_buf, o_buf, l_buf, *, sb):
    first = pl.program_id(0) == 0
    kv_buf[:, :sb, :] = kvp_ref[...]
    kv_buf[:, sb:, :] = kvc_ref[...]
    lo = _lo_mask(QBLK)
    col = lax.broadcasted_iota(I32, (2 * QBLK, 2 * QBLK), 1)
    units = sb // QBLK
    for g, (_, dil) in enumerate(B_GROUPS):
        blocks_per_class = units // dil

        def unit(u, carry, g=g, dil=dil, blocks_per_class=blocks_per_class):
            r = u // blocks_per_class
            m = u % blocks_per_class
            q_start = r + dil * QBLK * m
            k_start = sb - dil * QBLK + q_start
            if dil == 1:
                q_start = pl.multiple_of(q_start, QBLK)
                k_start = pl.multiple_of(k_start, QBLK)
                q_idx = pl.ds(q_start, QBLK)
                k_idx = pl.ds(k_start, 2 * QBLK)
            else:
                q_idx = pl.ds(q_start, QBLK, stride=dil)
                k_idx = pl.ds(k_start, 2 * QBLK, stride=dil)
            for sp in range(2):
                q = q_ref[2 * g + sp, q_idx, :]
                zero = jnp.zeros_like(q)
                q2 = jnp.concatenate([jnp.where(lo, q, zero), jnp.where(lo, zero, q)], axis=0).astype(BF16)
                k = kv_buf[sp, k_idx, :].astype(BF16)
                v = kv_buf[2 + sp, k_idx, :].astype(BF16)
                s = _dot_t(q2, k) * (HEAD_DIM ** -0.5) + bias_ref[g, sp]
                s = jnp.where(jnp.logical_and(jnp.logical_and(first, m == 0), col < QBLK), NEG_INF, s)
                p, lse = _softmax_lse(s)
                o = _dot(p.astype(BF16), v)
                o_buf[g, sp, q_idx, :] = jnp.where(lo, o[:QBLK], o[QBLK:])
                l_buf[g, sp, q_idx, :] = jnp.where(lo, lse[:QBLK], lse[QBLK:])
            return carry

        lax.fori_loop(0, units, unit, 0)
    for sp in range(2):
        l = [l_buf[g, sp] for g in range(3)]
        mx = jnp.maximum(jnp.maximum(l[0], l[1]), l[2])
        e = [jnp.exp(x - mx) for x in l]
        den = e[0] + e[1] + e[2]
        acc = (e[0] / den) * o_buf[0, sp] + (e[1] / den) * o_buf[1, sp] + (e[2] / den) * o_buf[2, sp]
        o_ref[:, sp * LANES:(sp + 1) * LANES] = acc.astype(o_ref.dtype)


def _attn_b_prompt(qb, kvb, bias):
    seq = qb.shape[1]
    sb = 16 * QBLK
    assert seq % sb == 0
    return pl.pallas_call(
        functools.partial(_attn_b_prompt_body, sb=sb),
        grid=(seq // sb,),
        in_specs=[pl.BlockSpec((6, sb, LANES), lambda n: (0, n, 0)),
                  pl.BlockSpec((4, sb, LANES), lambda n: (0, n, 0)),
                  pl.BlockSpec((4, sb, LANES), lambda n: (0, jnp.maximum(n - 1, 0), 0)),
                  pl.BlockSpec((3, 2, 2 * QBLK, 2 * QBLK), lambda n: (0, 0, 0, 0))],
        out_specs=pl.BlockSpec((sb, 2 * LANES), lambda n: (n, 0)),
        out_shape=jax.ShapeDtypeStruct((seq, 2 * LANES), BF16),
        scratch_shapes=[pltpu.VMEM((4, 2 * sb, LANES), F32),
                        pltpu.VMEM((3, 2, sb, LANES), F32),
                        pltpu.VMEM((3, 2, sb, LANES), F32)],
        compiler_params=_cparams(("parallel",)),
        name="attn_b_prompt",
    )(qb, kvb, kvb, bias)


def _attn_mem_prompt_body(q_ref, mkv_ref, o_ref):
    for h in range(M_HEADS):
        k = mkv_ref[:, h * LANES:(h + 1) * LANES].astype(BF16)
        v = mkv_ref[:, (M_HEADS + h) * LANES:(M_HEADS + h + 1) * LANES].astype(BF16)
        s = _dot_t(q_ref[h].astype(BF16), k) * (M_HEAD_DIM ** -0.5)
        p, _ = _softmax_lse(s)
        o_ref[:, h * LANES:(h + 1) * LANES] = _dot(p.astype(BF16), v).astype(o_ref.dtype)


def _attn_mem_prompt(qm, mkv, tq):
    seq = qm.shape[1]
    return pl.pallas_call(
        _attn_mem_prompt_body,
        grid=(seq // tq,),
        in_specs=[pl.BlockSpec((M_HEADS, tq, LANES), lambda n: (0, n, 0)),
                  pl.BlockSpec((MEM_TOKENS, 2 * M_HEADS * LANES), lambda n: (0, 0))],
        out_specs=pl.BlockSpec((tq, M_HEADS * LANES), lambda n: (n, 0)),
        out_shape=jax.ShapeDtypeStruct((seq, M_HEADS * LANES), BF16),
        compiler_params=_cparams(("parallel",)),
        name="attn_mem_prompt",
    )(qm, mkv)


def _attn_mem_sample_body(q_ref, mk_ref, mv_ref, o_ref, *, bb):
    def one(b, carry):
        rows = pl.ds(pl.multiple_of(b * SAMPLE_ROWS, SAMPLE_ROWS), SAMPLE_ROWS)
        for h in range(M_HEADS):
            k = mk_ref[b, :, h * LANES:(h + 1) * LANES].astype(BF16)
            v = mv_ref[b, :, h * LANES:(h + 1) * LANES].astype(BF16)
            s = _dot_t(q_ref[h, rows, :].astype(BF16), k) * (M_HEAD_DIM ** -0.5)
            p, _ = _softmax_lse(s)
            o_ref[rows, h * LANES:(h + 1) * LANES] = _dot(p.astype(BF16), v).astype(o_ref.dtype)
        return carry

    lax.fori_loop(0, bb, one, 0)


def _attn_mem_sample(qm, mk, mv, bb):
    batch = mk.shape[0]
    return pl.pallas_call(
        functools.partial(_attn_mem_sample_body, bb=bb),
        grid=(batch // bb,),
        in_specs=[pl.BlockSpec((M_HEADS, bb * SAMPLE_ROWS, LANES), lambda n: (0, n, 0)),
                  pl.BlockSpec((bb, MEM_TOKENS, M_HEADS * LANES), lambda n: (n, 0, 0)),
                  pl.BlockSpec((bb, MEM_TOKENS, M_HEADS * LANES), lambda n: (n, 0, 0))],
        out_specs=pl.BlockSpec((bb * SAMPLE_ROWS, M_HEADS * LANES), lambda n: (n, 0)),
        out_shape=jax.ShapeDtypeStruct((batch * SAMPLE_ROWS, M_HEADS * LANES), BF16),
        compiler_params=_cparams(("parallel",)),
        name="attn_mem_sample",
    )(qm, mk, mv)


def _attn_a_sample_body(q_ref, kvn_ref, ck_ref, cv_ref, bias_ref, sink_ref, o_ref, *, bb):
    lo = _lo_mask(SAMPLE_ROWS)
    pad = jnp.zeros((QBLK - SAMPLE_ROWS, LANES), F32)

    def one(b, carry):
        rows = pl.ds(pl.multiple_of(b * SAMPLE_ROWS, SAMPLE_ROWS), SAMPLE_ROWS)
        q = [q_ref[p, rows, :] for p in range(4)]
        zero = jnp.zeros_like(q[0])
        q_sets = (
            jnp.concatenate([jnp.where(lo, q[0], zero), jnp.where(lo, q[1], zero),
                             jnp.where(lo, zero, q[2]), jnp.where(lo, zero, q[3])], axis=0).astype(BF16),
            jnp.concatenate([jnp.where(lo, zero, q[0]), jnp.where(lo, zero, q[1]),
                             jnp.where(lo, q[2], zero), jnp.where(lo, q[3], zero)], axis=0).astype(BF16),
        )
        ck = ck_ref[b]
        cv = cv_ref[b]
        cache = ((ck, pltpu.roll(ck, HEAD_DIM, 1)), (cv, pltpu.roll(cv, HEAD_DIM, 1)))
        outs = []
        for st in range(2):
            k = jnp.concatenate([cache[0][st], kvn_ref[st, rows, :], pad], axis=0).astype(BF16)
            v = jnp.concatenate([cache[1][st], kvn_ref[2 + st, rows, :], pad], axis=0).astype(BF16)
            s = _dot_t(q_sets[st], k) * (HEAD_DIM ** -0.5) + bias_ref[st]
            p = _softmax_sink(s, sink_ref[st])
            outs.append(_dot(p.astype(BF16), v))
        o1, o2 = outs
        blk = lambda o, r: o[r * SAMPLE_ROWS:(r + 1) * SAMPLE_ROWS]
        slabs = (jnp.where(lo, blk(o1, 0), blk(o2, 0)), jnp.where(lo, blk(o1, 1), blk(o2, 1)),
                 jnp.where(lo, blk(o2, 2), blk(o1, 2)), jnp.where(lo, blk(o2, 3), blk(o1, 3)))
        for p in range(4):
            o_ref[rows, p * LANES:(p + 1) * LANES] = slabs[p].astype(o_ref.dtype)
        return carry

    lax.fori_loop(0, bb, one, 0)


def _attn_a_sample(qa, kva, ck, cv, bias, sink, bb):
    batch = ck.shape[0]
    return pl.pallas_call(
        functools.partial(_attn_a_sample_body, bb=bb),
        grid=(batch // bb,),
        in_specs=[pl.BlockSpec((4, bb * SAMPLE_ROWS, LANES), lambda n: (0, n, 0)),
                  pl.BlockSpec((4, bb * SAMPLE_ROWS, LANES), lambda n: (0, n, 0)),
                  pl.BlockSpec((bb, A_WINDOW, LANES), lambda n: (n, 0, 0)),
                  pl.BlockSpec((bb, A_WINDOW, LANES), lambda n: (n, 0, 0)),
                  pl.BlockSpec((2, 4 * SAMPLE_ROWS, 2 * QBLK), lambda n: (0, 0, 0)),
                  pl.BlockSpec((2, 4 * SAMPLE_ROWS, 1), lambda n: (0, 0, 0))],
        out_specs=pl.BlockSpec((bb * SAMPLE_ROWS, 4 * LANES), lambda n: (n, 0)),
        out_shape=jax.ShapeDtypeStruct((batch * SAMPLE_ROWS, 4 * LANES), BF16),
        compiler_params=_cparams(("parallel",)),
        name="attn_a_sample",
    )(qa, kva, ck, cv, bias, sink)


def _attn_b_sample_body(q_ref, kvn_ref, ck_ref, cv_ref, bias_c_ref, bias_n_ref, o_ref):
    lo = _lo_mask(SAMPLE_ROWS)
    pad = jnp.zeros((QBLK - SAMPLE_ROWS, LANES), F32)
    n = SAMPLE_ROWS
    for sp in range(2):
        qs = [q_ref[2 * g + sp] for g in range(3)]
        zero = jnp.zeros_like(qs[0])
        q2 = jnp.concatenate([jnp.where(lo, q, zero) for q in qs] + [jnp.where(lo, zero, q) for q in qs],
                             axis=0).astype(BF16)
        kc = ck_ref[0, :, sp * LANES:(sp + 1) * LANES].astype(BF16)
        vc = cv_ref[0, :, sp * LANES:(sp + 1) * LANES].astype(BF16)
        kn = jnp.concatenate([kvn_ref[sp], pad], axis=0).astype(BF16)
        vn = jnp.concatenate([kvn_ref[2 + sp], pad], axis=0).astype(BF16)
        s_c = _dot_t(q2, kc) * (HEAD_DIM ** -0.5) + bias_c_ref[sp]
        s_n = _dot_t(q2, kn) * (HEAD_DIM ** -0.5) + bias_n_ref[sp]
        m = jnp.maximum(jnp.max(s_c, axis=-1, keepdims=True), jnp.max(s_n, axis=-1, keepdims=True))
        p_c = jnp.exp(s_c - m)
        p_n = jnp.exp(s_n - m)
        l = jnp.sum(p_c, axis=-1, keepdims=True) + jnp.sum(p_n, axis=-1, keepdims=True)
        lse = m + jnp.log(l)
        o = _dot((p_c / l).astype(BF16), vc) + _dot((p_n / l).astype(BF16), vn)
        o_g = [jnp.where(lo, o[g * n:(g + 1) * n], o[(3 + g) * n:(4 + g) * n]) for g in range(3)]
        l_g = [jnp.where(lo, lse[g * n:(g + 1) * n], lse[(3 + g) * n:(4 + g) * n]) for g in range(3)]
        mx = jnp.maximum(jnp.maximum(l_g[0], l_g[1]), l_g[2])
        e = [jnp.exp(x - mx) for x in l_g]
        den = e[0] + e[1] + e[2]
        acc = (e[0] / den) * o_g[0] + (e[1] / den) * o_g[1] + (e[2] / den) * o_g[2]
        o_ref[:, sp * LANES:(sp + 1) * LANES] = acc.astype(o_ref.dtype)


def _attn_b_sample(qb, kvb, ck, cv, bias_c, bias_n):
    batch, lb = ck.shape[0], ck.shape[1]
    return pl.pallas_call(
        _attn_b_sample_body,
        grid=(batch,),
        in_specs=[pl.BlockSpec((6, SAMPLE_ROWS, LANES), lambda n: (0, n, 0)),
                  pl.BlockSpec((4, SAMPLE_ROWS, LANES), lambda n: (0, n, 0)),
                  pl.BlockSpec((1, lb, 2 * LANES), lambda n: (n, 0, 0)),
                  pl.BlockSpec((1, lb, 2 * LANES), lambda n: (n, 0, 0)),
                  pl.BlockSpec((2, 6 * SAMPLE_ROWS, lb), lambda n: (0, 0, 0)),
                  pl.BlockSpec((2, 6 * SAMPLE_ROWS, LANES), lambda n: (0, 0, 0))],
        out_specs=pl.BlockSpec((SAMPLE_ROWS, 2 * LANES), lambda n: (n, 0)),
        out_shape=jax.ShapeDtypeStruct((batch * SAMPLE_ROWS, 2 * LANES), BF16),
        compiler_params=_cparams(("parallel",)),
        name="attn_b_sample",
    )(qb, kvb, ck, cv, bias_c, bias_n)


def _layer_norm(x, g, b):
    mu = jnp.mean(x, axis=-1, keepdims=True)
    xc = x - mu
    var = jnp.mean(xc * xc, axis=-1, keepdims=True)
    return xc * lax.rsqrt(var + LN_EPS) * g + b


def _split_bf16(x):
    hi = x.astype(BF16)
    return hi, (x - hi.astype(F32)).astype(BF16)


def _merge_body(cnt_in_ref, x_ref, oa_ref, ob_ref, om_ref, wg_ref, wa_ref, wb_ref, wm_ref, wo_ref, g1_ref, b1_ref,
                wr_hi_ref, wr_lo_ref, br_ref, h_ref, route_ref, gate_ref, cnt_ref, *, tm):
    i = pl.program_id(0)
    x = x_ref[...]
    gates = jax.nn.sigmoid(_dot(x.astype(BF16), wg_ref[...]))
    u = (gates[:, :D_MODEL] * _dot(oa_ref[...], wa_ref[...])
         + gates[:, D_MODEL:2 * D_MODEL] * _dot(ob_ref[...], wb_ref[...])
         + gates[:, 2 * D_MODEL:] * _dot(om_ref[...], wm_ref[...]))
    mixed = _dot(u.astype(BF16), wo_ref[...])
    h = _layer_norm(DEEPNORM_ALPHA * x + mixed, g1_ref[...], b1_ref[...])
    h_ref[...] = h

    h_hi, h_lo = _split_bf16(h)
    logits = _dot(h_hi, wr_hi_ref[...]) + (_dot(h_hi, wr_lo_ref[...]) + _dot(h_lo, wr_hi_ref[...])) + br_ref[...]
    lane = lax.broadcasted_iota(I32, (tm, LANES), 1)
    lane_f = lane.astype(F32)
    work = logits
    vals, sels, idxs = [], [], []
    for _ in range(TOP_K):
        mx = jnp.max(work, axis=-1, keepdims=True)
        idx = jnp.min(jnp.where(work == mx, lane_f, float(LANES)), axis=-1, keepdims=True)
        sel = lane_f == idx
        vals.append(mx)
        idxs.append(idx)
        sels.append(sel)
        work = jnp.where(sel, NEG_INF, work)
    ex = [jnp.exp(v - vals[0]) for v in vals]
    den = ex[0] + ex[1] + ex[2] + ex[3]
    chosen = jnp.logical_or(jnp.logical_or(sels[0], sels[1]), jnp.logical_or(sels[2], sels[3]))
    onehot = jnp.where(chosen, 1.0, 0.0)

    @pl.when(i == 0)
    def _():
        cnt_ref[...] = cnt_in_ref[...]

    carry = cnt_ref[...]
    row = lax.broadcasted_iota(I32, (tm, tm), 0)
    colm = lax.broadcasted_iota(I32, (tm, tm), 1)
    tri = jnp.where(colm < row, 1.0, 0.0).astype(BF16)
    before = _dot(tri, onehot.astype(BF16)) + carry
    route = jnp.zeros((tm, LANES), F32)
    gate = jnp.zeros((tm, LANES), F32)
    for k in range(TOP_K):
        rank = jnp.sum(jnp.where(sels[k], before, 0.0), axis=-1, keepdims=True)
        route = jnp.where(lane == k, idxs[k], route)
        route = jnp.where(lane == TOP_K + k, rank, route)
        gate = jnp.where(lane == k, ex[k] / den, gate)
    route_ref[...] = route.astype(I32)
    gate_ref[...] = gate
    cnt_ref[...] = carry + jnp.sum(onehot, axis=0, keepdims=True)


def _merge(cnt_in, x, oa, ob, om, w, tm):
    rows = x.shape[0]
    full = lambda a: pl.BlockSpec(a.shape, lambda i: (0,) * a.ndim)
    row_blk = lambda c: pl.BlockSpec((tm, c), lambda i: (i, 0))
    weights = (w["w_g"], w["w_a"], w["w_b"], w["w_m"], w["w_o"], w["ln1_g"], w["ln1_b"],
               w["wr_hi"], w["wr_lo"], w["b_r"])
    return pl.pallas_call(
        functools.partial(_merge_body, tm=tm),
        grid=(rows // tm,),
        in_specs=[full(cnt_in), row_blk(D_MODEL), row_blk(oa.shape[1]), row_blk(ob.shape[1]), row_blk(om.shape[1])]
        + [full(a) for a in weights],
        out_specs=[row_blk(D_MODEL), row_blk(LANES), row_blk(LANES), pl.BlockSpec((1, LANES), lambda i: (0, 0))],
        out_shape=[jax.ShapeDtypeStruct((rows, D_MODEL), F32), jax.ShapeDtypeStruct((rows, LANES), I32),
                   jax.ShapeDtypeStruct((rows, LANES), F32), jax.ShapeDtypeStruct((1, LANES), F32)],
        compiler_params=_cparams(("arbitrary",)),
        name="merge_ln_router",
    )(cnt_in, x, oa, ob, om, *weights)


def _dispatch_body(dest_ref, h_ref, xin_ref, xout_ref, sem, *, tm):
    del xin_ref

    def copy(t, k):
        d = dest_ref[t * TOP_K + k]
        return pltpu.make_async_copy(h_ref.at[pl.ds(t, 1), :], xout_ref.at[pl.ds(d, 1), :], sem)

    def start(t, carry):
        for k in range(TOP_K):
            copy(t, k).start()
        return carry

    def wait(t, carry):
        for k in range(TOP_K):
            copy(t, k).wait()
        return carry

    lax.fori_loop(0, tm, start, 0)
    lax.fori_loop(0, tm, wait, 0)


def _dispatch(dest_flat, h, x_rows, tm):
    rows = h.shape[0]
    return pl.pallas_call(
        functools.partial(_dispatch_body, tm=tm),
        grid=(rows // tm,),
        in_specs=[pl.BlockSpec((tm * TOP_K,), lambda i: (i,), memory_space=pltpu.SMEM),
                  pl.BlockSpec((tm, D_MODEL), lambda i: (i, 0)),
                  pl.BlockSpec(memory_space=pl.ANY)],
        out_specs=pl.BlockSpec(memory_space=pl.ANY),
        out_shape=jax.ShapeDtypeStruct(x_rows.shape, x_rows.dtype),
        scratch_shapes=[pltpu.SemaphoreType.DMA(())],
        input_output_aliases={2: 0},
        compiler_params=_cparams(("arbitrary",)),
        name="moe_dispatch",
    )(dest_flat, h, x_rows)


def _expert_body(be_ref, nv_ref, x_ref, wgu_ref, bgu_ref, wd_ref, bd_ref, y_ref, wgu_bf, wd_bf):
    i = pl.program_id(0)
    valid = i < nv_ref[0]

    @pl.when(valid)
    def _():
        new_expert = jnp.logical_or(i == 0, be_ref[i] != be_ref[jnp.maximum(i - 1, 0)])

        @pl.when(new_expert)
        def _():
            wgu_bf[...] = wgu_ref[0].astype(BF16)
            wd_bf[...] = wd_ref[0].astype(BF16)

        gu = _dot(x_ref[...].astype(BF16), wgu_bf[...]) + bgu_ref[0]
        g = jnp.minimum(gu[:, :D_FF], SWIGLU_LIMIT)
        u = jnp.clip(gu[:, D_FF:], -SWIGLU_LIMIT, SWIGLU_LIMIT)
        hidden = (u + 1.0) * g * jax.nn.sigmoid(SWIGLU_ALPHA * g)
        y_ref[...] = _dot(hidden.astype(BF16), wd_bf[...]) + bd_ref[0]

    @pl.when(jnp.logical_not(valid))
    def _():
        y_ref[...] = jnp.zeros_like(y_ref)


def _experts(block_expert, n_valid, x_rows, w_gu, b_gu, w_down, b_down):
    n_blocks = x_rows.shape[0] // EXPERT_ROWS
    last = lambda i, nv: jnp.minimum(i, nv[0] - 1)
    grid_spec = pltpu.PrefetchScalarGridSpec(
        num_scalar_prefetch=2,
        grid=(n_blocks,),
        in_specs=[pl.BlockSpec((EXPERT_ROWS, D_MODEL), lambda i, be, nv: (last(i, nv), 0)),
                  pl.BlockSpec((1, D_MODEL, 2 * D_FF), lambda i, be, nv: (be[last(i, nv)], 0, 0)),
                  pl.BlockSpec((1, 1, 2 * D_FF), lambda i, be, nv: (be[last(i, nv)], 0, 0)),
                  pl.BlockSpec((1, D_FF, D_MODEL), lambda i, be, nv: (be[last(i, nv)], 0, 0)),
                  pl.BlockSpec((1, 1, D_MODEL), lambda i, be, nv: (be[last(i, nv)], 0, 0))],
        out_specs=pl.BlockSpec((EXPERT_ROWS, D_MODEL), lambda i, be, nv: (i, 0)),
        scratch_shapes=[pltpu.VMEM((D_MODEL, 2 * D_FF), BF16), pltpu.VMEM((D_FF, D_MODEL), BF16)],
    )
    return pl.pallas_call(
        _expert_body,
        grid_spec=grid_spec,
        out_shape=jax.ShapeDtypeStruct(x_rows.shape, F32),
        compiler_params=_cparams(("arbitrary",)),
        name="moe_experts",
    )(block_expert, n_valid, x_rows, w_gu, b_gu.reshape(N_EXPERTS, 1, 2 * D_FF), w_down,
      b_down.reshape(N_EXPERTS, 1, D_MODEL))


def _combine_body(dest_ref, h_ref, gate_ref, g2_ref, b2_ref, yrows_ref, o_ref, buf, sem, *, tm):
    def copy(t, k):
        d = dest_ref[t * TOP_K + k]
        return pltpu.make_async_copy(yrows_ref.at[pl.ds(d, 1), :], buf.at[k, pl.ds(t, 1), :], sem)

    def start(t, carry):
        for k in range(TOP_K):
            copy(t, k).start()
        return carry

    def wait(t, carry):
        for k in range(TOP_K):
            copy(t, k).wait()
        return carry

    lax.fori_loop(0, tm, start, 0)
    lax.fori_loop(0, tm, wait, 0)
    gate = gate_ref[...]
    f = gate[:, 0:1] * buf[0]
    for k in range(1, TOP_K):
        f = f + gate[:, k:k + 1] * buf[k]
    o_ref[...] = _layer_norm(DEEPNORM_ALPHA * h_ref[...] + f, g2_ref[...], b2_ref[...])


def _combine(dest_flat, h, gate, ln2_g, ln2_b, y_rows, tm):
    rows = h.shape[0]
    return pl.pallas_call(
        functools.partial(_combine_body, tm=tm),
        grid=(rows // tm,),
        in_specs=[pl.BlockSpec((tm * TOP_K,), lambda i: (i,), memory_space=pltpu.SMEM),
                  pl.BlockSpec((tm, D_MODEL), lambda i: (i, 0)),
                  pl.BlockSpec((tm, LANES), lambda i: (i, 0)),
                  pl.BlockSpec((1, D_MODEL), lambda i: (0, 0)),
                  pl.BlockSpec((1, D_MODEL), lambda i: (0, 0)),
                  pl.BlockSpec(memory_space=pl.ANY)],
        out_specs=pl.BlockSpec((tm, D_MODEL), lambda i: (i, 0)),
        out_shape=jax.ShapeDtypeStruct((rows, D_MODEL), F32),
        scratch_shapes=[pltpu.VMEM((TOP_K, tm, D_MODEL), F32), pltpu.SemaphoreType.DMA(())],
        compiler_params=_cparams(("arbitrary",)),
        name="moe_combine_ln2",
    )(dest_flat, h, gate, ln2_g, ln2_b, y_rows)


def _t5_bucket(dist):
    d = jnp.maximum(dist, 0)
    d_large = jnp.maximum(d, REL_MAX_EXACT).astype(F32)
    large = REL_MAX_EXACT + (jnp.log(d_large / REL_MAX_EXACT) / math.log(REL_MAX_DISTANCE / REL_MAX_EXACT)
                             * (REL_BUCKETS - REL_MAX_EXACT)).astype(I32)
    return jnp.where(d < REL_MAX_EXACT, d, jnp.minimum(large, REL_BUCKETS - 1))


_A_SET_HEADS = ((0, 2, 5, 7), (1, 3, 4, 6))


def _bias_a(rel_bias, dist, valid):
    tab = rel_bias.astype(F32)[_t5_bucket(dist)]
    per_head = jnp.where(valid[None], jnp.moveaxis(tab, -1, 0), NEG_INF)
    return jnp.stack([jnp.concatenate([per_head[h] for h in heads], axis=0) for heads in _A_SET_HEADS])


def _sink_a(sinks, q_rows):
    s = sinks.astype(F32)
    return jnp.stack([jnp.concatenate([jnp.full((q_rows, 1), 1.0, F32) * s[h] for h in heads], axis=0)
                      for heads in _A_SET_HEADS])


def _bias_b_prompt(rel_bias):
    qi = jnp.arange(QBLK)[:, None]
    kj = jnp.arange(2 * QBLK)[None, :]
    sub = QBLK + qi - kj
    groups = []
    for g, (window, dil) in enumerate(B_GROUPS):
        valid = (sub >= 0) & (sub <= window // dil)
        tab = rel_bias.astype(F32)[_t5_bucket(sub * dil)]
        heads = [jnp.where(valid, tab[..., A_HEADS + g * B_KV_HEADS + h], NEG_INF) for h in range(B_KV_HEADS)]
        groups.append(jnp.stack([jnp.concatenate([heads[0], heads[1]], axis=0),
                                 jnp.concatenate([heads[2], heads[3]], axis=0)]))
    return jnp.stack(groups)


def _bias_b_sample(rel_bias, lb):
    qi = jnp.arange(SAMPLE_ROWS)[:, None]
    dist_c = lb + qi - jnp.arange(lb)[None, :]
    kn = jnp.arange(LANES)[None, :]
    dist_n = qi - kn
    out_c, out_n = [], []
    for sp in range(2):
        rows_c, rows_n = [], []
        for half in range(2):
            for g, (window, dil) in enumerate(B_GROUPS):
                col = A_HEADS + g * B_KV_HEADS + 2 * sp + half
                for dist, extra, rows in ((dist_c, True, rows_c), (dist_n, kn < SAMPLE_ROWS, rows_n)):
                    valid = (dist >= 0) & (dist <= window) & (dist % dil == 0) & extra
                    rows.append(jnp.where(valid, rel_bias.astype(F32)[:, col][_t5_bucket(dist)], NEG_INF))
        out_c.append(jnp.concatenate(rows_c, axis=0))
        out_n.append(jnp.concatenate(rows_n, axis=0))
    return jnp.stack(out_c), jnp.stack(out_n)


def _swap_halves(w):
    return jnp.concatenate([w[:, HEAD_DIM:], w[:, :HEAD_DIM]], axis=1)


def _slabs_to_rows(slabs):
    return jnp.moveaxis(slabs, 0, 1).reshape(slabs.shape[1], slabs.shape[0] * LANES)


def kernel(x_prompt, x_sample, cache_a_k, cache_a_v, cache_b_k, cache_b_v, cache_mem_k, cache_mem_v, mem_prompt,
           rel_bias, sinks_a, w_in, w_mem_kv, w_br_a, w_br_b, w_br_m, w_o, ln1_g, ln1_b, ln2_g, ln2_b, w_router,
           b_router, w_gu, b_gu, w_down, b_down):
    assert w_in.shape[0] == DEPTH == 1
    bsz, seq, _ = x_prompt.shape
    dec_b, dec_n, _ = x_sample.shape
    assert bsz == 1 and dec_n <= SAMPLE_ROWS
    la, lb = cache_a_k.shape[2], cache_b_k.shape[2]
    assert la == A_WINDOW and lb == B_WINDOW_MAX

    w = w_in[0]
    c = np.cumsum((0, 512, 128, 128, 768, 256, 256, 512, 3072))
    w_qa, w_ka, w_va, w_qb, w_kb, w_vb, w_qm, w_g = (w[:, c[i]:c[i + 1]] for i in range(8))
    w_proj = jnp.concatenate([w_qb, w_kb, w_vb, w_qa, w_ka, _swap_halves(w_ka), w_va, _swap_halves(w_va), w_qm],
                             axis=1).astype(BF16)
    wr = jnp.pad(w_router[0].astype(F32), ((0, 0), (0, LANES - N_EXPERTS)))
    wr_hi = wr.astype(BF16)
    merge_w = dict(
        w_g=w_g.astype(BF16), w_a=w_br_a[0].astype(BF16), w_b=w_br_b[0].astype(BF16), w_m=w_br_m[0].astype(BF16),
        w_o=w_o[0].astype(BF16), ln1_g=ln1_g.astype(F32).reshape(1, D_MODEL), ln1_b=ln1_b.astype(F32).reshape(1, D_MODEL),
        wr_hi=wr_hi, wr_lo=(wr - wr_hi.astype(F32)).astype(BF16),
        b_r=jnp.pad(b_router.astype(F32).reshape(1, N_EXPERTS), ((0, 0), (0, LANES - N_EXPERTS)),
                    constant_values=NEG_INF))
    ln2g = ln2_g.astype(F32).reshape(1, D_MODEL)
    ln2b = ln2_b.astype(F32).reshape(1, D_MODEL)

    qi = jnp.arange(QBLK)[:, None]
    kj = jnp.arange(2 * QBLK)[None, :]
    dist = QBLK + qi - kj
    bias_a_p = _bias_a(rel_bias, dist, (dist >= 0) & (dist < A_WINDOW))
    qs = jnp.arange(SAMPLE_ROWS)[:, None]
    dist_s = la + qs - kj
    bias_a_s = _bias_a(rel_bias, dist_s, (dist_s >= 0) & (dist_s < A_WINDOW) & (kj < la + SAMPLE_ROWS))
    bias_b_p = _bias_b_prompt(rel_bias)
    bias_b_sc, bias_b_sn = _bias_b_sample(rel_bias, lb)

    xp = x_prompt.reshape(seq, D_MODEL)
    qb_p, kvb_p, qa_p, kva_p, qm_p = _project(xp, w_proj, 512)
    mkv_p = _matmul_f32(mem_prompt.reshape(MEM_TOKENS, D_MODEL), w_mem_kv[0].astype(BF16))
    oa_p = _attn_a_prompt(qa_p, kva_p, bias_a_p, _sink_a(sinks_a[0], QBLK), 512)
    ob_p = _attn_b_prompt(qb_p, kvb_p, bias_b_p)
    om_p = _attn_mem_prompt(qm_p, mkv_p, 512)
    zero_cnt = jnp.zeros((1, LANES), F32)
    h_p, route_p, gate_p, cnt_p = _merge(zero_cnt, xp, oa_p, ob_p, om_p, merge_w, 512)

    xs = jnp.pad(x_sample, ((0, 0), (0, SAMPLE_ROWS - dec_n), (0, 0))).reshape(dec_b * SAMPLE_ROWS, D_MODEL)
    qb_s, kvb_s, qa_s, kva_s, qm_s = _project(xs, w_proj, 512)
    oa_s = _attn_a_sample(qa_s, kva_s, cache_a_k[0].reshape(dec_b, la, LANES), cache_a_v[0].reshape(dec_b, la, LANES),
                          bias_a_s, _sink_a(sinks_a[0], SAMPLE_ROWS), 8)
    ob_s = _attn_b_sample(qb_s, kvb_s, cache_b_k[0].reshape(dec_b, lb, 2 * LANES),
                          cache_b_v[0].reshape(dec_b, lb, 2 * LANES), bias_b_sc, bias_b_sn)
    om_s = _attn_mem_sample(qm_s, cache_mem_k[0].reshape(dec_b, MEM_TOKENS, M_HEADS * LANES),
                            cache_mem_v[0].reshape(dec_b, MEM_TOKENS, M_HEADS * LANES), 8)
    unpad = lambda a: a.reshape(dec_b, SAMPLE_ROWS, a.shape[-1])[:, :dec_n].reshape(dec_b * dec_n, a.shape[-1])
    x_s = x_sample.reshape(dec_b * dec_n, D_MODEL)
    h_s, route_s, gate_s, cnt_all = _merge(cnt_p, x_s, unpad(oa_s), unpad(ob_s), unpad(om_s), merge_w, dec_b * dec_n)

    n_tok = seq + dec_b * dec_n
    n_rows = n_tok * TOP_K + N_EXPERTS * EXPERT_ROWS
    counts = cnt_all[0, :N_EXPERTS].astype(I32)
    padded = (counts + EXPERT_ROWS - 1) // EXPERT_ROWS * EXPERT_ROWS
    pad_end = jnp.cumsum(padded)
    pad_start = pad_end - padded
    n_valid = (pad_end[-1] // EXPERT_ROWS).reshape(1).astype(I32)
    block_expert = jnp.minimum(
        jnp.searchsorted(pad_end, jnp.arange(n_rows // EXPERT_ROWS, dtype=I32) * EXPERT_ROWS, side="right"),
        N_EXPERTS - 1).astype(I32)
    dest = lambda route: (pad_start[route[:, :TOP_K]] + route[:, TOP_K:2 * TOP_K]).reshape(-1).astype(I32)
    dest_p, dest_s = dest(route_p), dest(route_s)

    x_rows = jnp.zeros((n_rows, D_MODEL), F32)
    x_rows = _dispatch(dest_p, h_p, x_rows, 512)
    x_rows = _dispatch(dest_s, h_s, x_rows, dec_b * dec_n)
    y_rows = _experts(block_expert, n_valid, x_rows, w_gu[0], b_gu[0], w_down[0], b_down[0])
    y_p = _combine(dest_p, h_p, gate_p, ln2g, ln2b, y_rows, 256)
    y_s = _combine(dest_s, h_s, gate_s, ln2g, ln2b, y_rows, 256)

    kva_rows = _slabs_to_rows(kva_p[:, seq - la:])
    kvb_rows = _slabs_to_rows(kvb_p[:, seq - lb:])
    a_k_p = kva_rows[:, 0:128].reshape(1, 1, la, 2, HEAD_DIM)
    a_v_p = kva_rows[:, 256:384].reshape(1, 1, la, 2, HEAD_DIM)
    b_k_p = kvb_rows[:, 0:256].reshape(1, 1, lb, B_KV_HEADS, HEAD_DIM)
    b_v_p = kvb_rows[:, 256:512].reshape(1, 1, lb, B_KV_HEADS, HEAD_DIM)
    m_k_p = mkv_p[:, :M_HEADS * LANES].reshape(1, 1, MEM_TOKENS, M_HEADS, M_HEAD_DIM)
    m_v_p = mkv_p[:, M_HEADS * LANES:].reshape(1, 1, MEM_TOKENS, M_HEADS, M_HEAD_DIM)
    kva_s_rows = unpad(_slabs_to_rows(kva_s))
    kvb_s_rows = unpad(_slabs_to_rows(kvb_s))
    a_k_s = kva_s_rows[:, 0:128].reshape(1, dec_b, dec_n, 2, HEAD_DIM)
    a_v_s = kva_s_rows[:, 256:384].reshape(1, dec_b, dec_n, 2, HEAD_DIM)
    b_k_s = kvb_s_rows[:, 0:256].reshape(1, dec_b, dec_n, B_KV_HEADS, HEAD_DIM)
    b_v_s = kvb_s_rows[:, 256:512].reshape(1, dec_b, dec_n, B_KV_HEADS, HEAD_DIM)
    return (y_p.reshape(bsz, seq, D_MODEL), y_s.reshape(dec_b, dec_n, D_MODEL),
            a_k_p, a_v_p, b_k_p, b_v_p, m_k_p, m_v_p, a_k_s, a_v_s, b_k_s, b_v_s)
```

```python
import functools
import math

import numpy as np
import jax
import jax.numpy as jnp
from jax import lax
from jax.experimental import pallas as pl
from jax.experimental.pallas import tpu as pltpu

F32 = jnp.float32
BF16 = jnp.bfloat16
I32 = jnp.int32

D_MODEL = 1024
HEAD_DIM = 64
A_HEADS = 8
A_KV_HEADS = 2
A_WINDOW = 128
B_KV_HEADS = 4
B_GROUPS = ((128, 1), (512, 4), (2048, 16))
B_WINDOW_MAX = 2048
MEM_TOKENS = 256
M_HEADS = 4
M_HEAD_DIM = 128
REL_BUCKETS = 32
REL_MAX_EXACT = REL_BUCKETS // 2
REL_MAX_DISTANCE = B_WINDOW_MAX
REL_HEADS = A_HEADS + B_KV_HEADS * len(B_GROUPS)
N_EXPERTS = 32
TOP_K = 4
D_FF = D_MODEL
SWIGLU_LIMIT = 7.0
SWIGLU_ALPHA = 1.702
LN_EPS = 1e-5
DEPTH = 1
DEEPNORM_ALPHA = (2 * DEPTH) ** 0.25

LANES = 128
QBLK = 128
SAMPLE_ROWS = 8
SAMPLE_NEW_MAX = 4
EXPERT_ROWS = 512
VMEM_LIMIT = 56 * 1024 * 1024
NEG_INF = float("-inf")
B_DIL_MAX = B_GROUPS[-1][1]
B_TAIL = B_GROUPS[1][0]

PROJ_SLABS = (6, 4, 4, 4, 4)


def _cparams(sem):
    return pltpu.CompilerParams(dimension_semantics=sem, vmem_limit_bytes=VMEM_LIMIT)


def _dot(a, b):
    return jnp.dot(a, b, preferred_element_type=F32)


def _dot_t(a, b):
    return lax.dot_general(a, b, (((1,), (1,)), ((), ())), preferred_element_type=F32)


def _lo_mask(rows):
    return lax.broadcasted_iota(I32, (rows, LANES), 1) < HEAD_DIM


def _proj_body(x_ref, w_ref, *o_refs):
    x = x_ref[...].astype(BF16)
    col = 0
    for o_ref in o_refs:
        n = o_ref.shape[0] * LANES
        acc = _dot(x, w_ref[:, col:col + n])
        for s in range(o_ref.shape[0]):
            o_ref[s] = acc[:, s * LANES:(s + 1) * LANES]
        col += n


def _project(x, w_bf, tm):
    rows = x.shape[0]
    n_cols = w_bf.shape[1]
    assert rows % tm == 0 and n_cols == sum(PROJ_SLABS) * LANES
    return pl.pallas_call(
        _proj_body,
        grid=(rows // tm,),
        in_specs=[pl.BlockSpec((tm, D_MODEL), lambda i: (i, 0)),
                  pl.BlockSpec((D_MODEL, n_cols), lambda i: (0, 0))],
        out_specs=[pl.BlockSpec((n, tm, LANES), lambda i: (0, i, 0)) for n in PROJ_SLABS],
        out_shape=[jax.ShapeDtypeStruct((n, rows, LANES), F32) for n in PROJ_SLABS],
        compiler_params=_cparams(("parallel",)),
        name="in_proj",
    )(x, w_bf)


def _matmul_body(x_ref, w_ref, o_ref):
    o_ref[...] = _dot(x_ref[...].astype(BF16), w_ref[...])


def _matmul_f32(x, w_bf):
    return pl.pallas_call(
        _matmul_body,
        out_shape=jax.ShapeDtypeStruct((x.shape[0], w_bf.shape[1]), F32),
        compiler_params=_cparams(None),
        name="mem_kv_proj",
    )(x, w_bf)


def _softmax_sink(s, sink):
    m = jnp.maximum(jnp.max(s, axis=-1, keepdims=True), sink)
    p = jnp.exp(s - m)
    denom = jnp.sum(p, axis=-1, keepdims=True) + jnp.exp(sink - m)
    return p / denom


def _softmax_lse(s):
    m = jnp.max(s, axis=-1, keepdims=True)
    p = jnp.exp(s - m)
    l = jnp.sum(p, axis=-1, keepdims=True)
    return p / l, m + jnp.log(l)


def _softmax_parts(parts, sink=None):
    m = functools.reduce(jnp.maximum, [jnp.max(s, axis=-1, keepdims=True) for s in parts])
    if sink is not None:
        m = jnp.maximum(m, sink)
    ps = [jnp.exp(s - m) for s in parts]
    l = functools.reduce(jnp.add, [jnp.sum(p, axis=-1, keepdims=True) for p in ps])
    lse = m + jnp.log(l)
    if sink is not None:
        l = l + jnp.exp(sink - m)
    return [p / l for p in ps], lse


def _attn_a_prompt_body(q_ref, kvc_ref, kvp_ref, bias_ref, sink_ref, o_ref, *, nsub):
    first = pl.program_id(0) == 0
    lo = _lo_mask(QBLK)
    col = lax.broadcasted_iota(I32, (4 * QBLK, 2 * QBLK), 1)
    for j in range(nsub):
        rows = slice(j * QBLK, (j + 1) * QBLK)

        def keys(slab):
            cur = kvc_ref[slab, rows, :]
            prev = kvp_ref[slab] if j == 0 else kvc_ref[slab, (j - 1) * QBLK:j * QBLK, :]
            return jnp.concatenate([prev, cur], axis=0).astype(BF16)

        q = [q_ref[p, rows, :] for p in range(4)]
        zero = jnp.zeros_like(q[0])
        q_sets = (
            jnp.concatenate([jnp.where(lo, q[0], zero), jnp.where(lo, q[1], zero),
                             jnp.where(lo, zero, q[2]), jnp.where(lo, zero, q[3])], axis=0).astype(BF16),
            jnp.concatenate([jnp.where(lo, zero, q[0]), jnp.where(lo, zero, q[1]),
                             jnp.where(lo, q[2], zero), jnp.where(lo, q[3], zero)], axis=0).astype(BF16),
        )
        outs = []
        for st in range(2):
            s = _dot_t(q_sets[st], keys(st)) * (HEAD_DIM ** -0.5) + bias_ref[st]
            if j == 0:
                s = jnp.where(jnp.logical_and(first, col < QBLK), NEG_INF, s)
            p = _softmax_sink(s, sink_ref[st])
            outs.append(_dot(p.astype(BF16), keys(2 + st)))
        o1, o2 = outs
        blk = lambda o, r: o[r * QBLK:(r + 1) * QBLK]
        slabs = (jnp.where(lo, blk(o1, 0), blk(o2, 0)), jnp.where(lo, blk(o1, 1), blk(o2, 1)),
                 jnp.where(lo, blk(o2, 2), blk(o1, 2)), jnp.where(lo, blk(o2, 3), blk(o1, 3)))
        for p in range(4):
            o_ref[rows, p * LANES:(p + 1) * LANES] = slabs[p].astype(o_ref.dtype)


def _attn_a_prompt(qa, kva, bias, sink, tq):
    seq = qa.shape[1]
    nsub = tq // QBLK
    return pl.pallas_call(
        functools.partial(_attn_a_prompt_body, nsub=nsub),
        grid=(seq // tq,),
        in_specs=[pl.BlockSpec((4, tq, LANES), lambda n: (0, n, 0)),
                  pl.BlockSpec((4, tq, LANES), lambda n: (0, n, 0)),
                  pl.BlockSpec((4, QBLK, LANES), lambda n: (0, jnp.maximum(n * nsub - 1, 0), 0)),
                  pl.BlockSpec((2, 4 * QBLK, 2 * QBLK), lambda n: (0, 0, 0)),
                  pl.BlockSpec((2, 4 * QBLK, 1), lambda n: (0, 0, 0))],
        out_specs=pl.BlockSpec((tq, 4 * LANES), lambda n: (n, 0)),
        out_shape=jax.ShapeDtypeStruct((seq, 4 * LANES), BF16),
        compiler_params=_cparams(("parallel",)),
        name="attn_a_prompt",
    )(qa, kva, kva, bias, sink)


def _attn_b_prompt_body(q_ref, kvc_ref, kvp_ref, bias_ref, o_ref, kv_buf, o_buf, l_buf, *, sb):
    first = pl.program_id(0) == 0
    kv_buf[:, :sb, :] = kvp_ref[...]
    kv_buf[:, sb:, :] = kvc_ref[...]
    lo = _lo_mask(QBLK)
    col = lax.broadcasted_iota(I32, (2 * QBLK, 2 * QBLK), 1)
    units = sb // QBLK
    for g, (_, dil) in enumerate(B_GROUPS):
        blocks_per_class = units // dil

        def unit(u, carry, g=g, dil=dil, blocks_per_class=blocks_per_class):
            r = u // blocks_per_class
            m = u % blocks_per_class
            q_start = r + dil * QBLK * m
            k_start = sb - dil * QBLK + q_start
            if dil == 1:
                q_start = pl.multiple_of(q_start, QBLK)
                k_start = pl.multiple_of(k_start, QBLK)
                q_idx = pl.ds(q_start, QBLK)
                k_idx = pl.ds(k_start, 2 * QBLK)
            else:
                q_idx = pl.ds(q_start, QBLK, stride=dil)
                k_idx = pl.ds(k_start, 2 * QBLK, stride=dil)
            for sp in range(2):
                q = q_ref[2 * g + sp, q_idx, :]
                zero = jnp.zeros_like(q)
                q2 = jnp.concatenate([jnp.where(lo, q, zero), jnp.where(lo, zero, q)], axis=0).astype(BF16)
                k = kv_buf[sp, k_idx, :].astype(BF16)
                v = kv_buf[2 + sp, k_idx, :].astype(BF16)
                s = _dot_t(q2, k) * (HEAD_DIM ** -0.5) + bias_ref[g, sp]
                s = jnp.where(jnp.logical_and(jnp.logical_and(first, m == 0), col < QBLK), NEG_INF, s)
                p, lse = _softmax_lse(s)
                o = _dot(p.astype(BF16), v)
                o_buf[g, sp, q_idx, :] = jnp.where(lo, o[:QBLK], o[QBLK:])
                l_buf[g, sp, q_idx, :] = jnp.where(lo, lse[:QBLK], lse[QBLK:])
            return carry

        lax.fori_loop(0, units, unit, 0)
    for sp in range(2):
        l = [l_buf[g, sp] for g in range(3)]
        mx = jnp.maximum(jnp.maximum(l[0], l[1]), l[2])
        e = [jnp.exp(x - mx) for x in l]
        den = e[0] + e[1] + e[2]
        acc = (e[0] / den) * o_buf[0, sp] + (e[1] / den) * o_buf[1, sp] + (e[2] / den) * o_buf[2, sp]
        o_ref[:, sp * LANES:(sp + 1) * LANES] = acc.astype(o_ref.dtype)


def _attn_b_prompt(qb, kvb, bias):
    seq = qb.shape[1]
    sb = B_DIL_MAX * QBLK
    assert seq % sb == 0
    return pl.pallas_call(
        functools.partial(_attn_b_prompt_body, sb=sb),
        grid=(seq // sb,),
        in_specs=[pl.BlockSpec((6, sb, LANES), lambda n: (0, n, 0)),
                  pl.BlockSpec((4, sb, LANES), lambda n: (0, n, 0)),
                  pl.BlockSpec((4, sb, LANES), lambda n: (0, jnp.maximum(n - 1, 0), 0)),
                  pl.BlockSpec((3, 2, 2 * QBLK, 2 * QBLK), lambda n: (0, 0, 0, 0))],
        out_specs=pl.BlockSpec((sb, 2 * LANES), lambda n: (n, 0)),
        out_shape=jax.ShapeDtypeStruct((seq, 2 * LANES), BF16),
        scratch_shapes=[pltpu.VMEM((4, 2 * sb, LANES), F32),
                        pltpu.VMEM((3, 2, sb, LANES), F32),
                        pltpu.VMEM((3, 2, sb, LANES), F32)],
        compiler_params=_cparams(("parallel",)),
        name="attn_b_prompt",
    )(qb, kvb, kvb, bias)


def _attn_mem_prompt_body(q_ref, mkv_ref, o_ref):
    for h in range(M_HEADS):
        k = mkv_ref[:, h * LANES:(h + 1) * LANES].astype(BF16)
        v = mkv_ref[:, (M_HEADS + h) * LANES:(M_HEADS + h + 1) * LANES].astype(BF16)
        s = _dot_t(q_ref[h].astype(BF16), k) * (M_HEAD_DIM ** -0.5)
        p, _ = _softmax_lse(s)
        o_ref[:, h * LANES:(h + 1) * LANES] = _dot(p.astype(BF16), v).astype(o_ref.dtype)


def _attn_mem_prompt(qm, mkv, tq):
    seq = qm.shape[1]
    return pl.pallas_call(
        _attn_mem_prompt_body,
        grid=(seq // tq,),
        in_specs=[pl.BlockSpec((M_HEADS, tq, LANES), lambda n: (0, n, 0)),
                  pl.BlockSpec((MEM_TOKENS, 2 * M_HEADS * LANES), lambda n: (0, 0))],
        out_specs=pl.BlockSpec((tq, M_HEADS * LANES), lambda n: (n, 0)),
        out_shape=jax.ShapeDtypeStruct((seq, M_HEADS * LANES), BF16),
        compiler_params=_cparams(("parallel",)),
        name="attn_mem_prompt",
    )(qm, mkv)


def _attn_mem_sample_body(q_ref, mk_ref, mv_ref, mask_ref, o_ref, *, bb):
    def one(b, carry):
        rows = pl.ds(pl.multiple_of(b * SAMPLE_ROWS, SAMPLE_ROWS), SAMPLE_ROWS)
        q = jnp.concatenate([q_ref[h, rows, :] for h in range(M_HEADS)], axis=0).astype(BF16)
        s = _dot_t(q, mk_ref[b].astype(BF16)) * (M_HEAD_DIM ** -0.5) + mask_ref[...]
        p, _ = _softmax_lse(s)
        o = _dot(p.astype(BF16), mv_ref[b].astype(BF16))
        for h in range(M_HEADS):
            o_ref[rows, h * LANES:(h + 1) * LANES] = o[h * SAMPLE_ROWS:(h + 1) * SAMPLE_ROWS].astype(o_ref.dtype)
        return carry

    lax.fori_loop(0, bb, one, 0)


def _attn_mem_sample(qm, mk, mv, mask, bb):
    batch, n_rows = mk.shape[0], mk.shape[1]
    return pl.pallas_call(
        functools.partial(_attn_mem_sample_body, bb=bb),
        grid=(batch // bb,),
        in_specs=[pl.BlockSpec((M_HEADS, bb * SAMPLE_ROWS, LANES), lambda n: (0, n, 0)),
                  pl.BlockSpec((bb, n_rows, LANES), lambda n: (n, 0, 0)),
                  pl.BlockSpec((bb, n_rows, LANES), lambda n: (n, 0, 0)),
                  pl.BlockSpec(mask.shape, lambda n: (0, 0))],
        out_specs=pl.BlockSpec((bb * SAMPLE_ROWS, M_HEADS * LANES), lambda n: (n, 0)),
        out_shape=jax.ShapeDtypeStruct((batch * SAMPLE_ROWS, M_HEADS * LANES), BF16),
        compiler_params=_cparams(("parallel",)),
        name="attn_mem_sample",
    )(qm, mk, mv, mask)


def _head_rows(ref, slab, rows, half):
    x = ref[slab, rows, :]
    return x[:, half * HEAD_DIM:(half + 1) * HEAD_DIM]


def _pad_rows(x, n):
    return jnp.concatenate([x, jnp.zeros((n - x.shape[0], x.shape[1]), x.dtype)], axis=0)


def _attn_a_sample_body(q_ref, kvn_ref, ck_ref, cv_ref, bias_c_ref, bias_n_ref, sink_ref, o_ref, *, bb):
    def one(b, carry):
        rows = pl.ds(pl.multiple_of(b * SAMPLE_ROWS, SAMPLE_ROWS), SAMPLE_ROWS)
        q = jnp.concatenate([_head_rows(q_ref, h // 2, rows, h % 2) for h in range(A_HEADS)], axis=0).astype(BF16)
        kn = _pad_rows(jnp.concatenate([_head_rows(kvn_ref, 0, rows, g) for g in range(A_KV_HEADS)], axis=0), LANES)
        vn = _pad_rows(jnp.concatenate([_head_rows(kvn_ref, 2, rows, g) for g in range(A_KV_HEADS)], axis=0), LANES)
        s_c = _dot_t(q, ck_ref[b].astype(BF16)) * (HEAD_DIM ** -0.5) + bias_c_ref[...]
        s_n = _dot_t(q, kn.astype(BF16)) * (HEAD_DIM ** -0.5) + bias_n_ref[...]
        (p_c, p_n), _ = _softmax_parts([s_c, s_n], sink_ref[...])
        o = _dot(p_c.astype(BF16), cv_ref[b].astype(BF16)) + _dot(p_n.astype(BF16), vn.astype(BF16))
        for h in range(A_HEADS):
            o_ref[h, rows, :] = o[h * SAMPLE_ROWS:(h + 1) * SAMPLE_ROWS].astype(o_ref.dtype)
        return carry

    lax.fori_loop(0, bb, one, 0)


def _attn_a_sample(qa, kva, ck, cv, bias_c, bias_n, sink, bb):
    batch, n_rows = ck.shape[0], ck.shape[1]
    full = lambda a: pl.BlockSpec(a.shape, lambda n: (0,) * a.ndim)
    return pl.pallas_call(
        functools.partial(_attn_a_sample_body, bb=bb),
        grid=(batch // bb,),
        in_specs=[pl.BlockSpec((4, bb * SAMPLE_ROWS, LANES), lambda n: (0, n, 0)),
                  pl.BlockSpec((4, bb * SAMPLE_ROWS, LANES), lambda n: (0, n, 0)),
                  pl.BlockSpec((bb, n_rows, HEAD_DIM), lambda n: (n, 0, 0)),
                  pl.BlockSpec((bb, n_rows, HEAD_DIM), lambda n: (n, 0, 0)),
                  full(bias_c), full(bias_n), full(sink)],
        out_specs=pl.BlockSpec((A_HEADS, bb * SAMPLE_ROWS, HEAD_DIM), lambda n: (0, n, 0)),
        out_shape=jax.ShapeDtypeStruct((A_HEADS, batch * SAMPLE_ROWS, HEAD_DIM), BF16),
        compiler_params=_cparams(("parallel",)),
        name="attn_a_sample",
    )(qa, kva, ck, cv, bias_c, bias_n, sink)


def _attn_b_sample_body(q_ref, kvn_ref, kd_ref, kt_ref, vd_ref, vt_ref, bias_d_ref, bias_t_ref, bias_n_ref, o_ref):
    n = SAMPLE_ROWS
    rows = slice(None)
    q = jnp.concatenate([_head_rows(q_ref, 2 * g + h // 2, rows, h % 2)
                         for h in range(B_KV_HEADS) for g in range(len(B_GROUPS))], axis=0).astype(BF16)
    kn = _pad_rows(jnp.concatenate([_head_rows(kvn_ref, h // 2, rows, h % 2) for h in range(B_KV_HEADS)], axis=0), LANES)
    vn = _pad_rows(jnp.concatenate([_head_rows(kvn_ref, 2 + h // 2, rows, h % 2) for h in range(B_KV_HEADS)], axis=0),
                   LANES)
    n_dil = kd_ref.shape[1] * kd_ref.shape[2]
    kd = kd_ref[0].reshape(n_dil, HEAD_DIM).astype(BF16)
    vd = vd_ref[0].reshape(n_dil, HEAD_DIM).astype(BF16)
    scale = HEAD_DIM ** -0.5
    s_d = _dot_t(q, kd) * scale + bias_d_ref[...]
    s_t = _dot_t(q, kt_ref[0].astype(BF16)) * scale + bias_t_ref[...]
    s_n = _dot_t(q, kn.astype(BF16)) * scale + bias_n_ref[...]
    (p_d, p_t, p_n), lse = _softmax_parts([s_d, s_t, s_n])
    o = (_dot(p_d.astype(BF16), vd) + _dot(p_t.astype(BF16), vt_ref[0].astype(BF16))
         + _dot(p_n.astype(BF16), vn.astype(BF16)))
    for h in range(B_KV_HEADS):
        o_g = [o[(3 * h + g) * n:(3 * h + g + 1) * n] for g in range(3)]
        l_g = [lse[(3 * h + g) * n:(3 * h + g + 1) * n] for g in range(3)]
        mx = jnp.maximum(jnp.maximum(l_g[0], l_g[1]), l_g[2])
        e = [jnp.exp(x - mx) for x in l_g]
        den = e[0] + e[1] + e[2]
        acc = (e[0] / den) * o_g[0] + (e[1] / den) * o_g[1] + (e[2] / den) * o_g[2]
        o_ref[h] = acc.astype(o_ref.dtype)


def _attn_b_sample(qb, kvb, ck, cv, bias_d, bias_t, bias_n):
    batch, n_rows = ck.shape[0], ck.shape[1]
    grp = B_DIL_MAX * B_KV_HEADS
    tail = B_TAIL * B_KV_HEADS
    n_grp = (n_rows - tail) // grp
    ck_g = ck.reshape(batch, n_rows // grp, grp, HEAD_DIM)
    cv_g = cv.reshape(batch, n_rows // grp, grp, HEAD_DIM)
    first4 = SAMPLE_NEW_MAX * B_KV_HEADS
    dil_spec = pl.BlockSpec((1, n_grp, first4, HEAD_DIM), lambda n: (n, 0, 0, 0))
    tail_spec = pl.BlockSpec((1, tail, HEAD_DIM), lambda n: (n, n_rows // tail - 1, 0))
    full = lambda a: pl.BlockSpec(a.shape, lambda n: (0,) * a.ndim)
    return pl.pallas_call(
        _attn_b_sample_body,
        grid=(batch,),
        in_specs=[pl.BlockSpec((6, SAMPLE_ROWS, LANES), lambda n: (0, n, 0)),
                  pl.BlockSpec((4, SAMPLE_ROWS, LANES), lambda n: (0, n, 0)),
                  dil_spec, tail_spec, dil_spec, tail_spec,
                  full(bias_d), full(bias_t), full(bias_n)],
        out_specs=pl.BlockSpec((B_KV_HEADS, SAMPLE_ROWS, HEAD_DIM), lambda n: (0, n, 0)),
        out_shape=jax.ShapeDtypeStruct((B_KV_HEADS, batch * SAMPLE_ROWS, HEAD_DIM), BF16),
        compiler_params=_cparams(("parallel",)),
        name="attn_b_sample",
    )(qb, kvb, ck_g, ck, cv_g, cv, bias_d, bias_t, bias_n)


def _layer_norm(x, g, b):
    mu = jnp.mean(x, axis=-1, keepdims=True)
    xc = x - mu
    var = jnp.mean(xc * xc, axis=-1, keepdims=True)
    return xc * lax.rsqrt(var + LN_EPS) * g + b


def _split_bf16(x):
    hi = x.astype(BF16)
    return hi, (x - hi.astype(F32)).astype(BF16)


def _merge_body(cnt_in_ref, x_ref, oa_ref, ob_ref, om_ref, wg_ref, wa_ref, wb_ref, wm_ref, wo_ref, g1_ref, b1_ref,
                wr_hi_ref, wr_lo_ref, br_ref, h_ref, route_ref, gate_ref, cnt_ref, *, tm):
    i = pl.program_id(0)
    x = x_ref[...]
    gates = jax.nn.sigmoid(_dot(x.astype(BF16), wg_ref[...]))
    u = (gates[:, :D_MODEL] * _dot(oa_ref[...], wa_ref[...])
         + gates[:, D_MODEL:2 * D_MODEL] * _dot(ob_ref[...], wb_ref[...])
         + gates[:, 2 * D_MODEL:] * _dot(om_ref[...], wm_ref[...]))
    mixed = _dot(u.astype(BF16), wo_ref[...])
    h = _layer_norm(DEEPNORM_ALPHA * x + mixed, g1_ref[...], b1_ref[...])
    h_ref[...] = h

    h_hi, h_lo = _split_bf16(h)
    logits = _dot(h_hi, wr_hi_ref[...]) + (_dot(h_hi, wr_lo_ref[...]) + _dot(h_lo, wr_hi_ref[...])) + br_ref[...]
    lane = lax.broadcasted_iota(I32, (tm, LANES), 1)
    lane_f = lane.astype(F32)
    work = logits
    vals, sels, idxs = [], [], []
    for _ in range(TOP_K):
        mx = jnp.max(work, axis=-1, keepdims=True)
        idx = jnp.min(jnp.where(work == mx, lane_f, float(LANES)), axis=-1, keepdims=True)
        sel = lane_f == idx
        vals.append(mx)
        idxs.append(idx)
        sels.append(sel)
        work = jnp.where(sel, NEG_INF, work)
    ex = [jnp.exp(v - vals[0]) for v in vals]
    den = ex[0] + ex[1] + ex[2] + ex[3]
    chosen = jnp.logical_or(jnp.logical_or(sels[0], sels[1]), jnp.logical_or(sels[2], sels[3]))
    onehot = jnp.where(chosen, 1.0, 0.0)

    @pl.when(i == 0)
    def _():
        cnt_ref[...] = cnt_in_ref[...]

    carry = cnt_ref[...]
    row = lax.broadcasted_iota(I32, (tm, tm), 0)
    colm = lax.broadcasted_iota(I32, (tm, tm), 1)
    tri = jnp.where(colm < row, 1.0, 0.0).astype(BF16)
    before = _dot(tri, onehot.astype(BF16)) + carry
    route = jnp.zeros((tm, LANES), F32)
    gate = jnp.zeros((tm, LANES), F32)
    for k in range(TOP_K):
        rank = jnp.sum(jnp.where(sels[k], before, 0.0), axis=-1, keepdims=True)
        route = jnp.where(lane == k, idxs[k], route)
        route = jnp.where(lane == TOP_K + k, rank, route)
        gate = jnp.where(lane == k, ex[k] / den, gate)
    route_ref[...] = route.astype(I32)
    gate_ref[...] = gate
    cnt_ref[...] = carry + jnp.sum(onehot, axis=0, keepdims=True)


def _merge(cnt_in, x, oa, ob, om, w, tm):
    rows = x.shape[0]
    full = lambda a: pl.BlockSpec(a.shape, lambda i: (0,) * a.ndim)
    row_blk = lambda c: pl.BlockSpec((tm, c), lambda i: (i, 0))
    weights = (w["w_g"], w["w_a"], w["w_b"], w["w_m"], w["w_o"], w["ln1_g"], w["ln1_b"],
               w["wr_hi"], w["wr_lo"], w["b_r"])
    return pl.pallas_call(
        functools.partial(_merge_body, tm=tm),
        grid=(rows // tm,),
        in_specs=[full(cnt_in), row_blk(D_MODEL), row_blk(oa.shape[1]), row_blk(ob.shape[1]), row_blk(om.shape[1])]
        + [full(a) for a in weights],
        out_specs=[row_blk(D_MODEL), row_blk(LANES), row_blk(LANES), pl.BlockSpec((1, LANES), lambda i: (0, 0))],
        out_shape=[jax.ShapeDtypeStruct((rows, D_MODEL), F32), jax.ShapeDtypeStruct((rows, LANES), I32),
                   jax.ShapeDtypeStruct((rows, LANES), F32), jax.ShapeDtypeStruct((1, LANES), F32)],
        compiler_params=_cparams(("arbitrary",)),
        name="merge_ln_router",
    )(cnt_in, x, oa, ob, om, *weights)


def _dispatch_body(dest_ref, h_ref, xin_ref, xout_ref, sem, *, tm):
    del xin_ref

    def copy(t, k):
        d = dest_ref[t * TOP_K + k]
        return pltpu.make_async_copy(h_ref.at[pl.ds(t, 1), :], xout_ref.at[pl.ds(d, 1), :], sem)

    def start(t, carry):
        for k in range(TOP_K):
            copy(t, k).start()
        return carry

    def wait(t, carry):
        for k in range(TOP_K):
            copy(t, k).wait()
        return carry

    lax.fori_loop(0, tm, start, 0)
    lax.fori_loop(0, tm, wait, 0)


def _dispatch(dest_flat, h, x_rows, tm):
    rows = h.shape[0]
    return pl.pallas_call(
        functools.partial(_dispatch_body, tm=tm),
        grid=(rows // tm,),
        in_specs=[pl.BlockSpec((tm * TOP_K,), lambda i: (i,), memory_space=pltpu.SMEM),
                  pl.BlockSpec((tm, D_MODEL), lambda i: (i, 0)),
                  pl.BlockSpec(memory_space=pl.ANY)],
        out_specs=pl.BlockSpec(memory_space=pl.ANY),
        out_shape=jax.ShapeDtypeStruct(x_rows.shape, x_rows.dtype),
        scratch_shapes=[pltpu.SemaphoreType.DMA(())],
        input_output_aliases={2: 0},
        compiler_params=_cparams(("arbitrary",)),
        name="moe_dispatch",
    )(dest_flat, h, x_rows)


def _expert_body(be_ref, nv_ref, x_ref, wgu_ref, bgu_ref, wd_ref, bd_ref, y_ref, wgu_bf, wd_bf):
    i = pl.program_id(0)
    valid = i < nv_ref[0]

    @pl.when(valid)
    def _():
        new_expert = jnp.logical_or(i == 0, be_ref[i] != be_ref[jnp.maximum(i - 1, 0)])

        @pl.when(new_expert)
        def _():
            wgu_bf[...] = wgu_ref[0].astype(BF16)
            wd_bf[...] = wd_ref[0].astype(BF16)

        gu = _dot(x_ref[...].astype(BF16), wgu_bf[...]) + bgu_ref[0]
        g = jnp.minimum(gu[:, :D_FF], SWIGLU_LIMIT)
        u = jnp.clip(gu[:, D_FF:], -SWIGLU_LIMIT, SWIGLU_LIMIT)
        hidden = (u + 1.0) * g * jax.nn.sigmoid(SWIGLU_ALPHA * g)
        y_ref[...] = _dot(hidden.astype(BF16), wd_bf[...]) + bd_ref[0]

    @pl.when(jnp.logical_not(valid))
    def _():
        y_ref[...] = jnp.zeros_like(y_ref)


def _experts(block_expert, n_valid, x_rows, w_gu, b_gu, w_down, b_down):
    n_blocks = x_rows.shape[0] // EXPERT_ROWS
    last = lambda i, nv: jnp.minimum(i, nv[0] - 1)
    grid_spec = pltpu.PrefetchScalarGridSpec(
        num_scalar_prefetch=2,
        grid=(n_blocks,),
        in_specs=[pl.BlockSpec((EXPERT_ROWS, D_MODEL), lambda i, be, nv: (last(i, nv), 0)),
                  pl.BlockSpec((1, D_MODEL, 2 * D_FF), lambda i, be, nv: (be[last(i, nv)], 0, 0)),
                  pl.BlockSpec((1, 1, 2 * D_FF), lambda i, be, nv: (be[last(i, nv)], 0, 0)),
                  pl.BlockSpec((1, D_FF, D_MODEL), lambda i, be, nv: (be[last(i, nv)], 0, 0)),
                  pl.BlockSpec((1, 1, D_MODEL), lambda i, be, nv: (be[last(i, nv)], 0, 0))],
        out_specs=pl.BlockSpec((EXPERT_ROWS, D_MODEL), lambda i, be, nv: (i, 0)),
        scratch_shapes=[pltpu.VMEM((D_MODEL, 2 * D_FF), BF16), pltpu.VMEM((D_FF, D_MODEL), BF16)],
    )
    return pl.pallas_call(
        _expert_body,
        grid_spec=grid_spec,
        out_shape=jax.ShapeDtypeStruct(x_rows.shape, F32),
        compiler_params=_cparams(("arbitrary",)),
        name="moe_experts",
    )(block_expert, n_valid, x_rows, w_gu, b_gu.reshape(N_EXPERTS, 1, 2 * D_FF), w_down,
      b_down.reshape(N_EXPERTS, 1, D_MODEL))


def _combine_body(dest_ref, h_ref, gate_ref, g2_ref, b2_ref, yrows_ref, o_ref, buf, sem, *, tm):
    def copy(t, k):
        d = dest_ref[t * TOP_K + k]
        return pltpu.make_async_copy(yrows_ref.at[pl.ds(d, 1), :], buf.at[k, pl.ds(t, 1), :], sem)

    def start(t, carry):
        for k in range(TOP_K):
            copy(t, k).start()
        return carry

    def wait(t, carry):
        for k in range(TOP_K):
            copy(t, k).wait()
        return carry

    lax.fori_loop(0, tm, start, 0)
    lax.fori_loop(0, tm, wait, 0)
    gate = gate_ref[...]
    f = gate[:, 0:1] * buf[0]
    for k in range(1, TOP_K):
        f = f + gate[:, k:k + 1] * buf[k]
    o_ref[...] = _layer_norm(DEEPNORM_ALPHA * h_ref[...] + f, g2_ref[...], b2_ref[...])


def _combine(dest_flat, h, gate, ln2_g, ln2_b, y_rows, tm):
    rows = h.shape[0]
    return pl.pallas_call(
        functools.partial(_combine_body, tm=tm),
        grid=(rows // tm,),
        in_specs=[pl.BlockSpec((tm * TOP_K,), lambda i: (i,), memory_space=pltpu.SMEM),
                  pl.BlockSpec((tm, D_MODEL), lambda i: (i, 0)),
                  pl.BlockSpec((tm, LANES), lambda i: (i, 0)),
                  pl.BlockSpec((1, D_MODEL), lambda i: (0, 0)),
                  pl.BlockSpec((1, D_MODEL), lambda i: (0, 0)),
                  pl.BlockSpec(memory_space=pl.ANY)],
        out_specs=pl.BlockSpec((tm, D_MODEL), lambda i: (i, 0)),
        out_shape=jax.ShapeDtypeStruct((rows, D_MODEL), F32),
        scratch_shapes=[pltpu.VMEM((TOP_K, tm, D_MODEL), F32), pltpu.SemaphoreType.DMA(())],
        compiler_params=_cparams(("arbitrary",)),
        name="moe_combine_ln2",
    )(dest_flat, h, gate, ln2_g, ln2_b, y_rows)


def _t5_bucket_static(dist):
    d = np.maximum(dist, 0)
    d_large = np.maximum(d, REL_MAX_EXACT).astype(np.float32)
    val = (np.log(d_large / np.float32(REL_MAX_EXACT)) / np.float32(math.log(REL_MAX_DISTANCE / REL_MAX_EXACT))
           * np.float32(REL_BUCKETS - REL_MAX_EXACT))
    inner = (d > REL_MAX_EXACT) & (d < REL_MAX_DISTANCE)
    assert not np.any(inner & (np.abs(val - np.round(val)) < 1e-5))
    large = REL_MAX_EXACT + val.astype(np.int32)
    return np.where(d < REL_MAX_EXACT, d, np.minimum(large, REL_BUCKETS - 1)).astype(np.int32)


def _bias_table(rel_bias, dist, head, valid):
    hi = lax.Precision.HIGHEST
    head_rows = jnp.dot(jnp.asarray(np.eye(REL_HEADS, dtype=np.float32)[np.asarray(head)]),
                        rel_bias.astype(F32).T, precision=hi)
    onehot = (jnp.asarray(_t5_bucket_static(np.asarray(dist)))[..., None]
              == jnp.arange(REL_BUCKETS, dtype=I32)).astype(F32)
    tab = jnp.sum(onehot * head_rows[:, None, :], axis=-1)
    return jnp.where(jnp.asarray(np.asarray(valid)), tab, NEG_INF)


_A_SET_HEADS = ((0, 2, 5, 7), (1, 3, 4, 6))


def _tables_a_prompt(rel_bias, sinks):
    qi = np.arange(QBLK)[:, None]
    kj = np.arange(2 * QBLK)[None, :]
    dist = np.tile(QBLK + qi - kj, (4, 1))
    valid = (dist >= 0) & (dist < A_WINDOW)
    bias = jnp.stack([_bias_table(rel_bias, dist, np.repeat(heads, QBLK), valid) for heads in _A_SET_HEADS])
    s = sinks.astype(F32)
    sink = jnp.stack([jnp.repeat(jnp.stack([s[h] for h in heads]), QBLK).reshape(4 * QBLK, 1)
                      for heads in _A_SET_HEADS])
    return bias, sink


def _tables_b_prompt(rel_bias):
    qi = np.arange(QBLK)[:, None]
    kj = np.arange(2 * QBLK)[None, :]
    sub = np.tile(QBLK + qi - kj, (2, 1))
    out = []
    for g, (window, dil) in enumerate(B_GROUPS):
        valid = (sub >= 0) & (sub <= window // dil)
        out.append(jnp.stack([
            _bias_table(rel_bias, sub * dil, np.repeat(A_HEADS + g * B_KV_HEADS + 2 * sp + np.arange(2), QBLK), valid)
            for sp in range(2)]))
    return jnp.stack(out)


def _tables_a_sample(rel_bias, sinks, la):
    n = SAMPLE_ROWS
    head = np.repeat(np.arange(A_HEADS), n)
    qi = np.tile(np.arange(n), A_HEADS)[:, None]
    kv = (head // (A_HEADS // A_KV_HEADS))[:, None]
    col = np.arange(la * A_KV_HEADS)[None, :]
    dist_c = la + qi - col // A_KV_HEADS
    valid_c = (col % A_KV_HEADS == kv) & (dist_c >= 0) & (dist_c < A_WINDOW)
    coln = np.arange(LANES)[None, :]
    dist_n = qi - coln % n
    valid_n = (coln // n == kv) & (dist_n >= 0) & (dist_n < A_WINDOW)
    sink = jnp.repeat(sinks.astype(F32), n).reshape(A_HEADS * n, 1)
    return _bias_table(rel_bias, dist_c, head, valid_c), _bias_table(rel_bias, dist_n, head, valid_n), sink


def _tables_b_sample(rel_bias, lb):
    n = SAMPLE_ROWS
    n_groups = len(B_GROUPS)
    h = np.repeat(np.arange(B_KV_HEADS), n_groups * n)
    g = np.tile(np.repeat(np.arange(n_groups), n), B_KV_HEADS)
    qi = np.tile(np.arange(n), B_KV_HEADS * n_groups)[:, None]
    window = np.array([w for w, _ in B_GROUPS])[g][:, None]
    dil = np.array([d for _, d in B_GROUPS])[g][:, None]
    head_col = A_HEADS + g * B_KV_HEADS + h

    def table(token, col_head, extra):
        dist = lb + qi - token
        valid = (col_head == h[:, None]) & (dist >= 0) & (dist <= window) & (dist % dil == 0) & extra
        return _bias_table(rel_bias, dist, head_col, valid)

    n_old = lb - B_TAIL
    per = SAMPLE_NEW_MAX * B_KV_HEADS
    c = np.arange(n_old // B_DIL_MAX * per)[None, :]
    tok_d = c // per * B_DIL_MAX + c % per // B_KV_HEADS
    c = np.arange(B_TAIL * B_KV_HEADS)[None, :]
    tok_t = n_old + c // B_KV_HEADS
    cn = np.arange(LANES)[None, :]
    tok_n = lb + cn % n
    return (table(tok_d, np.arange(tok_d.shape[1])[None, :] % B_KV_HEADS, True),
            table(tok_t, c % B_KV_HEADS, True),
            table(tok_n, cn // n, cn < n * B_KV_HEADS))


def _mask_mem_sample():
    row_head = np.repeat(np.arange(M_HEADS), SAMPLE_ROWS)[:, None]
    col_head = np.arange(MEM_TOKENS * M_HEADS)[None, :] % M_HEADS
    return jnp.asarray(np.where(row_head == col_head, 0.0, -np.inf).astype(np.float32))


def _swap_halves(w):
    return jnp.concatenate([w[:, HEAD_DIM:], w[:, :HEAD_DIM]], axis=1)


def _slabs_to_rows(slabs):
    return jnp.moveaxis(slabs, 0, 1).reshape(slabs.shape[1], slabs.shape[0] * LANES)


def _heads_to_rows(o):
    return jnp.moveaxis(o, 0, 1).reshape(o.shape[1], o.shape[0] * HEAD_DIM)


def kernel(x_prompt, x_sample, cache_a_k, cache_a_v, cache_b_k, cache_b_v, cache_mem_k, cache_mem_v, mem_prompt,
           rel_bias, sinks_a, w_in, w_mem_kv, w_br_a, w_br_b, w_br_m, w_o, ln1_g, ln1_b, ln2_g, ln2_b, w_router,
           b_router, w_gu, b_gu, w_down, b_down):
    assert w_in.shape[0] == DEPTH == 1
    bsz, seq, _ = x_prompt.shape
    dec_b, dec_n, _ = x_sample.shape
    assert bsz == 1 and dec_n <= SAMPLE_NEW_MAX
    la, lb = cache_a_k.shape[2], cache_b_k.shape[2]
    assert la == A_WINDOW and lb == B_WINDOW_MAX

    w = w_in[0]
    c = np.cumsum((0, 512, 128, 128, 768, 256, 256, 512, 3072))
    w_qa, w_ka, w_va, w_qb, w_kb, w_vb, w_qm, w_g = (w[:, c[i]:c[i + 1]] for i in range(8))
    w_proj = jnp.concatenate([w_qb, w_kb, w_vb, w_qa, w_ka, _swap_halves(w_ka), w_va, _swap_halves(w_va), w_qm],
                             axis=1).astype(BF16)
    wr = jnp.pad(w_router[0].astype(F32), ((0, 0), (0, LANES - N_EXPERTS)))
    wr_hi = wr.astype(BF16)
    merge_w = dict(
        w_g=w_g.astype(BF16), w_a=w_br_a[0].astype(BF16), w_b=w_br_b[0].astype(BF16), w_m=w_br_m[0].astype(BF16),
        w_o=w_o[0].astype(BF16), ln1_g=ln1_g.astype(F32).reshape(1, D_MODEL), ln1_b=ln1_b.astype(F32).reshape(1, D_MODEL),
        wr_hi=wr_hi, wr_lo=(wr - wr_hi.astype(F32)).astype(BF16),
        b_r=jnp.pad(b_router.astype(F32).reshape(1, N_EXPERTS), ((0, 0), (0, LANES - N_EXPERTS)),
                    constant_values=NEG_INF))
    ln2g = ln2_g.astype(F32).reshape(1, D_MODEL)
    ln2b = ln2_b.astype(F32).reshape(1, D_MODEL)

    xp = x_prompt.reshape(seq, D_MODEL)
    qb_p, kvb_p, qa_p, kva_p, qm_p = _project(xp, w_proj, 512)
    mkv_p = _matmul_f32(mem_prompt.reshape(MEM_TOKENS, D_MODEL), w_mem_kv[0].astype(BF16))
    bias_a_p, sink_a_p = _tables_a_prompt(rel_bias, sinks_a[0])
    oa_p = _attn_a_prompt(qa_p, kva_p, bias_a_p, sink_a_p, 512)
    ob_p = _attn_b_prompt(qb_p, kvb_p, _tables_b_prompt(rel_bias))
    om_p = _attn_mem_prompt(qm_p, mkv_p, 512)
    zero_cnt = jnp.zeros((1, LANES), F32)
    h_p, route_p, gate_p, cnt_p = _merge(zero_cnt, xp, oa_p, ob_p, om_p, merge_w, 512)

    xs = jnp.pad(x_sample, ((0, 0), (0, SAMPLE_ROWS - dec_n), (0, 0))).reshape(dec_b * SAMPLE_ROWS, D_MODEL)
    qb_s, kvb_s, qa_s, kva_s, qm_s = _project(xs, w_proj, 512)
    bias_a_c, bias_a_n, sink_a_s = _tables_a_sample(rel_bias, sinks_a[0], la)
    oa_s = _attn_a_sample(qa_s, kva_s, cache_a_k[0].reshape(dec_b, la * A_KV_HEADS, HEAD_DIM),
                          cache_a_v[0].reshape(dec_b, la * A_KV_HEADS, HEAD_DIM), bias_a_c, bias_a_n, sink_a_s, 8)
    ob_s = _attn_b_sample(qb_s, kvb_s, cache_b_k[0].reshape(dec_b, lb * B_KV_HEADS, HEAD_DIM),
                          cache_b_v[0].reshape(dec_b, lb * B_KV_HEADS, HEAD_DIM), *_tables_b_sample(rel_bias, lb))
    om_s = _attn_mem_sample(qm_s, cache_mem_k[0].reshape(dec_b, MEM_TOKENS * M_HEADS, M_HEAD_DIM),
                            cache_mem_v[0].reshape(dec_b, MEM_TOKENS * M_HEADS, M_HEAD_DIM), _mask_mem_sample(), 8)
    unpad = lambda a: a.reshape(dec_b, SAMPLE_ROWS, a.shape[-1])[:, :dec_n].reshape(dec_b * dec_n, a.shape[-1])
    x_s = x_sample.reshape(dec_b * dec_n, D_MODEL)
    h_s, route_s, gate_s, cnt_all = _merge(cnt_p, x_s, unpad(_heads_to_rows(oa_s)), unpad(_heads_to_rows(ob_s)),
                                           unpad(om_s), merge_w, dec_b * dec_n)

    n_tok = seq + dec_b * dec_n
    n_rows = n_tok * TOP_K + N_EXPERTS * EXPERT_ROWS
    counts = cnt_all[0, :N_EXPERTS].astype(I32)
    padded = (counts + EXPERT_ROWS - 1) // EXPERT_ROWS * EXPERT_ROWS
    pad_end = jnp.cumsum(padded)
    pad_start = pad_end - padded
    n_valid = (pad_end[-1] // EXPERT_ROWS).reshape(1).astype(I32)
    block_row = jnp.arange(n_rows // EXPERT_ROWS, dtype=I32) * EXPERT_ROWS
    block_expert = jnp.minimum(jnp.sum((pad_end[None, :] <= block_row[:, None]).astype(I32), axis=1), N_EXPERTS - 1)
    expert_ids = jnp.arange(N_EXPERTS, dtype=I32)

    def dest(route):
        start = jnp.sum(jnp.where(route[:, :TOP_K, None] == expert_ids, pad_start, 0), axis=-1)
        return (start + route[:, TOP_K:2 * TOP_K]).reshape(-1).astype(I32)

    dest_p, dest_s = dest(route_p), dest(route_s)
    x_rows = jnp.zeros((n_rows, D_MODEL), F32)
    x_rows = _dispatch(dest_p, h_p, x_rows, 512)
    x_rows = _dispatch(dest_s, h_s, x_rows, dec_b * dec_n)
    y_rows = _experts(block_expert, n_valid, x_rows, w_gu[0], b_gu[0], w_down[0], b_down[0])
    y_p = _combine(dest_p, h_p, gate_p, ln2g, ln2b, y_rows, 256)
    y_s = _combine(dest_s, h_s, gate_s, ln2g, ln2b, y_rows, 256)

    kva_rows = _slabs_to_rows(kva_p[:, seq - la:])
    kvb_rows = _slabs_to_rows(kvb_p[:, seq - lb:])
    a_k_p = kva_rows[:, 0:128].reshape(1, 1, la, A_KV_HEADS, HEAD_DIM)
    a_v_p = kva_rows[:, 256:384].reshape(1, 1, la, A_KV_HEADS, HEAD_DIM)
    b_k_p = kvb_rows[:, 0:256].reshape(1, 1, lb, B_KV_HEADS, HEAD_DIM)
    b_v_p = kvb_rows[:, 256:512].reshape(1, 1, lb, B_KV_HEADS, HEAD_DIM)
    m_k_p = mkv_p[:, :M_HEADS * LANES].reshape(1, 1, MEM_TOKENS, M_HEADS, M_HEAD_DIM)
    m_v_p = mkv_p[:, M_HEADS * LANES:].reshape(1, 1, MEM_TOKENS, M_HEADS, M_HEAD_DIM)
    kva_s_rows = unpad(_slabs_to_rows(kva_s))
    kvb_s_rows = unpad(_slabs_to_rows(kvb_s))
    a_k_s = kva_s_rows[:, 0:128].reshape(1, dec_b, dec_n, A_KV_HEADS, HEAD_DIM)
    a_v_s = kva_s_rows[:, 256:384].reshape(1, dec_b, dec_n, A_KV_HEADS, HEAD_DIM)
    b_k_s = kvb_s_rows[:, 0:256].reshape(1, dec_b, dec_n, B_KV_HEADS, HEAD_DIM)
    b_v_s = kvb_s_rows[:, 256:512].reshape(1, dec_b, dec_n, B_KV_HEADS, HEAD_DIM)
    return (y_p.reshape(bsz, seq, D_MODEL), y_s.reshape(dec_b, dec_n, D_MODEL),
            a_k_p, a_v_p, b_k_p, b_v_p, m_k_p, m_v_p, a_k_s, a_v_s, b_k_s, b_v_s)
```

```python
import functools
import math

import numpy as np
import jax
import jax.numpy as jnp
from jax import lax
from jax.experimental import pallas as pl
from jax.experimental.pallas import tpu as pltpu

F32 = jnp.float32
BF16 = jnp.bfloat16
I32 = jnp.int32

D_MODEL = 1024
HEAD_DIM = 64
A_HEADS = 8
A_KV_HEADS = 2
A_WINDOW = 128
B_KV_HEADS = 4
B_GROUPS = ((128, 1), (512, 4), (2048, 16))
B_WINDOW_MAX = 2048
MEM_TOKENS = 256
M_HEADS = 4
M_HEAD_DIM = 128
REL_BUCKETS = 32
REL_MAX_EXACT = REL_BUCKETS // 2
REL_MAX_DISTANCE = B_WINDOW_MAX
REL_HEADS = A_HEADS + B_KV_HEADS * len(B_GROUPS)
N_EXPERTS = 32
TOP_K = 4
D_FF = D_MODEL
SWIGLU_LIMIT = 7.0
SWIGLU_ALPHA = 1.702
LN_EPS = 1e-5
DEPTH = 1
DEEPNORM_ALPHA = (2 * DEPTH) ** 0.25

LANES = 128
QBLK = 128
SAMPLE_ROWS = 8
EXPERT_ROWS = 512
VMEM_LIMIT = 56 * 1024 * 1024
NEG_INF = float("-inf")
B_DIL_MAX = B_GROUPS[-1][1]

PROJ_SLABS = (6, 4, 4, 4, 4)


def _cparams(sem):
    return pltpu.CompilerParams(dimension_semantics=sem, vmem_limit_bytes=VMEM_LIMIT)


def _dot(a, b):
    return jnp.dot(a, b, preferred_element_type=F32)


def _dot_t(a, b):
    return lax.dot_general(a, b, (((1,), (1,)), ((), ())), preferred_element_type=F32)


def _lo_mask(rows):
    return lax.broadcasted_iota(I32, (rows, LANES), 1) < HEAD_DIM


def _proj_body(x_ref, w_ref, *o_refs):
    x = x_ref[...].astype(BF16)
    col = 0
    for o_ref in o_refs:
        n = o_ref.shape[0] * LANES
        acc = _dot(x, w_ref[:, col:col + n])
        for s in range(o_ref.shape[0]):
            o_ref[s] = acc[:, s * LANES:(s + 1) * LANES]
        col += n


def _project(x, w_bf, tm):
    rows = x.shape[0]
    n_cols = w_bf.shape[1]
    assert rows % tm == 0 and n_cols == sum(PROJ_SLABS) * LANES
    return pl.pallas_call(
        _proj_body,
        grid=(rows // tm,),
        in_specs=[pl.BlockSpec((tm, D_MODEL), lambda i: (i, 0)),
                  pl.BlockSpec((D_MODEL, n_cols), lambda i: (0, 0))],
        out_specs=[pl.BlockSpec((n, tm, LANES), lambda i: (0, i, 0)) for n in PROJ_SLABS],
        out_shape=[jax.ShapeDtypeStruct((n, rows, LANES), F32) for n in PROJ_SLABS],
        compiler_params=_cparams(("parallel",)),
        name="in_proj",
    )(x, w_bf)


def _matmul_body(x_ref, w_ref, o_ref):
    o_ref[...] = _dot(x_ref[...].astype(BF16), w_ref[...])


def _matmul_f32(x, w_bf):
    return pl.pallas_call(
        _matmul_body,
        out_shape=jax.ShapeDtypeStruct((x.shape[0], w_bf.shape[1]), F32),
        compiler_params=_cparams(None),
        name="mem_kv_proj",
    )(x, w_bf)


def _softmax_sink(s, sink):
    m = jnp.maximum(jnp.max(s, axis=-1, keepdims=True), sink)
    p = jnp.exp(s - m)
    denom = jnp.sum(p, axis=-1, keepdims=True) + jnp.exp(sink - m)
    return p / denom


def _softmax_lse(s):
    m = jnp.max(s, axis=-1, keepdims=True)
    p = jnp.exp(s - m)
    l = jnp.sum(p, axis=-1, keepdims=True)
    return p / l, m + jnp.log(l)


def _softmax_parts(parts, sink=None):
    m = functools.reduce(jnp.maximum, [jnp.max(s, axis=-1, keepdims=True) for s in parts])
    if sink is not None:
        m = jnp.maximum(m, sink)
    ps = [jnp.exp(s - m) for s in parts]
    l = functools.reduce(jnp.add, [jnp.sum(p, axis=-1, keepdims=True) for p in ps])
    lse = m + jnp.log(l)
    if sink is not None:
        l = l + jnp.exp(sink - m)
    return [p / l for p in ps], lse


def _attn_a_prompt_body(q_ref, kvc_ref, kvp_ref, bias_ref, sink_ref, o_ref, *, nsub):
    first = pl.program_id(0) == 0
    lo = _lo_mask(QBLK)
    col = lax.broadcasted_iota(I32, (4 * QBLK, 2 * QBLK), 1)
    for j in range(nsub):
        rows = slice(j * QBLK, (j + 1) * QBLK)

        def keys(slab):
            cur = kvc_ref[slab, rows, :]
            prev = kvp_ref[slab] if j == 0 else kvc_ref[slab, (j - 1) * QBLK:j * QBLK, :]
            return jnp.concatenate([prev, cur], axis=0).astype(BF16)

        q = [q_ref[p, rows, :] for p in range(4)]
        zero = jnp.zeros_like(q[0])
        q_sets = (
            jnp.concatenate([jnp.where(lo, q[0], zero), jnp.where(lo, q[1], zero),
                             jnp.where(lo, zero, q[2]), jnp.where(lo, zero, q[3])], axis=0).astype(BF16),
            jnp.concatenate([jnp.where(lo, zero, q[0]), jnp.where(lo, zero, q[1]),
                             jnp.where(lo, q[2], zero), jnp.where(lo, q[3], zero)], axis=0).astype(BF16),
        )
        outs = []
        for st in range(2):
            s = _dot_t(q_sets[st], keys(st)) * (HEAD_DIM ** -0.5) + bias_ref[st]
            if j == 0:
                s = jnp.where(jnp.logical_and(first, col < QBLK), NEG_INF, s)
            p = _softmax_sink(s, sink_ref[st])
            outs.append(_dot(p.astype(BF16), keys(2 + st)))
        o1, o2 = outs
        blk = lambda o, r: o[r * QBLK:(r + 1) * QBLK]
        slabs = (jnp.where(lo, blk(o1, 0), blk(o2, 0)), jnp.where(lo, blk(o1, 1), blk(o2, 1)),
                 jnp.where(lo, blk(o2, 2), blk(o1, 2)), jnp.where(lo, blk(o2, 3), blk(o1, 3)))
        for p in range(4):
            o_ref[rows, p * LANES:(p + 1) * LANES] = slabs[p].astype(o_ref.dtype)


def _attn_a_prompt(qa, kva, bias, sink, tq):
    seq = qa.shape[1]
    nsub = tq // QBLK
    return pl.pallas_call(
        functools.partial(_attn_a_prompt_body, nsub=nsub),
        grid=(seq // tq,),
        in_specs=[pl.BlockSpec((4, tq, LANES), lambda n: (0, n, 0)),
                  pl.BlockSpec((4, tq, LANES), lambda n: (0, n, 0)),
                  pl.BlockSpec((4, QBLK, LANES), lambda n: (0, jnp.maximum(n * nsub - 1, 0), 0)),
                  pl.BlockSpec((2, 4 * QBLK, 2 * QBLK), lambda n: (0, 0, 0)),
                  pl.BlockSpec((2, 4 * QBLK, 1), lambda n: (0, 0, 0))],
        out_specs=pl.BlockSpec((tq, 4 * LANES), lambda n: (n, 0)),
        out_shape=jax.ShapeDtypeStruct((seq, 4 * LANES), BF16),
        compiler_params=_cparams(("parallel",)),
        name="attn_a_prompt",
    )(qa, kva, kva, bias, sink)


def _attn_b_prompt_body(q_ref, kvc_ref, kvp_ref, bias_ref, o_ref, kv_buf, o_buf, l_buf, *, sb):
    first = pl.program_id(0) == 0
    kv_buf[:, :sb, :] = kvp_ref[...]
    kv_buf[:, sb:, :] = kvc_ref[...]
    lo = _lo_mask(QBLK)
    col = lax.broadcasted_iota(I32, (2 * QBLK, 2 * QBLK), 1)
    units = sb // QBLK
    for g, (_, dil) in enumerate(B_GROUPS):
        blocks_per_class = units // dil

        def unit(u, carry, g=g, dil=dil, blocks_per_class=blocks_per_class):
            r = u // blocks_per_class
            m = u % blocks_per_class
            q_start = r + dil * QBLK * m
            k_start = sb - dil * QBLK + q_start
            if dil == 1:
                q_start = pl.multiple_of(q_start, QBLK)
                k_start = pl.multiple_of(k_start, QBLK)
                q_idx = pl.ds(q_start, QBLK)
                k_idx = pl.ds(k_start, 2 * QBLK)
            else:
                q_idx = pl.ds(q_start, QBLK, stride=dil)
                k_idx = pl.ds(k_start, 2 * QBLK, stride=dil)
            for sp in range(2):
                q = q_ref[2 * g + sp, q_idx, :]
                zero = jnp.zeros_like(q)
                q2 = jnp.concatenate([jnp.where(lo, q, zero), jnp.where(lo, zero, q)], axis=0).astype(BF16)
                k = kv_buf[sp, k_idx, :].astype(BF16)
                v = kv_buf[2 + sp, k_idx, :].astype(BF16)
                s = _dot_t(q2, k) * (HEAD_DIM ** -0.5) + bias_ref[g, sp]
                s = jnp.where(jnp.logical_and(jnp.logical_and(first, m == 0), col < QBLK), NEG_INF, s)
                p, lse = _softmax_lse(s)
                o = _dot(p.astype(BF16), v)
                o_buf[g, sp, q_idx, :] = jnp.where(lo, o[:QBLK], o[QBLK:])
                l_buf[g, sp, q_idx, :] = jnp.where(lo, lse[:QBLK], lse[QBLK:])
            return carry

        lax.fori_loop(0, units, unit, 0)
    for sp in range(2):
        l = [l_buf[g, sp] for g in range(3)]
        mx = jnp.maximum(jnp.maximum(l[0], l[1]), l[2])
        e = [jnp.exp(x - mx) for x in l]
        den = e[0] + e[1] + e[2]
        acc = (e[0] / den) * o_buf[0, sp] + (e[1] / den) * o_buf[1, sp] + (e[2] / den) * o_buf[2, sp]
        o_ref[:, sp * LANES:(sp + 1) * LANES] = acc.astype(o_ref.dtype)


def _attn_b_prompt(qb, kvb, bias):
    seq = qb.shape[1]
    sb = B_DIL_MAX * QBLK
    assert seq % sb == 0
    return pl.pallas_call(
        functools.partial(_attn_b_prompt_body, sb=sb),
        grid=(seq // sb,),
        in_specs=[pl.BlockSpec((6, sb, LANES), lambda n: (0, n, 0)),
                  pl.BlockSpec((4, sb, LANES), lambda n: (0, n, 0)),
                  pl.BlockSpec((4, sb, LANES), lambda n: (0, jnp.maximum(n - 1, 0), 0)),
                  pl.BlockSpec((3, 2, 2 * QBLK, 2 * QBLK), lambda n: (0, 0, 0, 0))],
        out_specs=pl.BlockSpec((sb, 2 * LANES), lambda n: (n, 0)),
        out_shape=jax.ShapeDtypeStruct((seq, 2 * LANES), BF16),
        scratch_shapes=[pltpu.VMEM((4, 2 * sb, LANES), F32),
                        pltpu.VMEM((3, 2, sb, LANES), F32),
                        pltpu.VMEM((3, 2, sb, LANES), F32)],
        compiler_params=_cparams(("parallel",)),
        name="attn_b_prompt",
    )(qb, kvb, kvb, bias)


def _attn_mem_prompt_body(q_ref, mkv_ref, o_ref):
    for h in range(M_HEADS):
        k = mkv_ref[:, h * LANES:(h + 1) * LANES].astype(BF16)
        v = mkv_ref[:, (M_HEADS + h) * LANES:(M_HEADS + h + 1) * LANES].astype(BF16)
        s = _dot_t(q_ref[h].astype(BF16), k) * (M_HEAD_DIM ** -0.5)
        p, _ = _softmax_lse(s)
        o_ref[:, h * LANES:(h + 1) * LANES] = _dot(p.astype(BF16), v).astype(o_ref.dtype)


def _attn_mem_prompt(qm, mkv, tq):
    seq = qm.shape[1]
    return pl.pallas_call(
        _attn_mem_prompt_body,
        grid=(seq // tq,),
        in_specs=[pl.BlockSpec((M_HEADS, tq, LANES), lambda n: (0, n, 0)),
                  pl.BlockSpec((MEM_TOKENS, 2 * M_HEADS * LANES), lambda n: (0, 0))],
        out_specs=pl.BlockSpec((tq, M_HEADS * LANES), lambda n: (n, 0)),
        out_shape=jax.ShapeDtypeStruct((seq, M_HEADS * LANES), BF16),
        compiler_params=_cparams(("parallel",)),
        name="attn_mem_prompt",
    )(qm, mkv)


def _attn_mem_sample_body(q_ref, mk_ref, mv_ref, mask_ref, o_ref, *, bb):
    def one(b, carry):
        rows = pl.ds(pl.multiple_of(b * SAMPLE_ROWS, SAMPLE_ROWS), SAMPLE_ROWS)
        q = jnp.concatenate([q_ref[h, rows, :] for h in range(M_HEADS)], axis=0).astype(BF16)
        s = _dot_t(q, mk_ref[b].astype(BF16)) * (M_HEAD_DIM ** -0.5) + mask_ref[...]
        p, _ = _softmax_lse(s)
        o = _dot(p.astype(BF16), mv_ref[b].astype(BF16))
        for h in range(M_HEADS):
            o_ref[rows, h * LANES:(h + 1) * LANES] = o[h * SAMPLE_ROWS:(h + 1) * SAMPLE_ROWS].astype(o_ref.dtype)
        return carry

    lax.fori_loop(0, bb, one, 0)


def _attn_mem_sample(qm, mk, mv, mask, bb):
    batch, n_rows = mk.shape[0], mk.shape[1]
    return pl.pallas_call(
        functools.partial(_attn_mem_sample_body, bb=bb),
        grid=(batch // bb,),
        in_specs=[pl.BlockSpec((M_HEADS, bb * SAMPLE_ROWS, LANES), lambda n: (0, n, 0)),
                  pl.BlockSpec((bb, n_rows, LANES), lambda n: (n, 0, 0)),
                  pl.BlockSpec((bb, n_rows, LANES), lambda n: (n, 0, 0)),
                  pl.BlockSpec(mask.shape, lambda n: (0, 0))],
        out_specs=pl.BlockSpec((bb * SAMPLE_ROWS, M_HEADS * LANES), lambda n: (n, 0)),
        out_shape=jax.ShapeDtypeStruct((batch * SAMPLE_ROWS, M_HEADS * LANES), BF16),
        compiler_params=_cparams(("parallel",)),
        name="attn_mem_sample",
    )(qm, mk, mv, mask)


def _head_rows(ref, slab, rows, half):
    x = ref[slab, rows, :]
    return x[:, half * HEAD_DIM:(half + 1) * HEAD_DIM]


def _pad_rows(x, n):
    return jnp.concatenate([x, jnp.zeros((n - x.shape[0], x.shape[1]), x.dtype)], axis=0)


def _attn_a_sample_body(q_ref, kvn_ref, ck_ref, cv_ref, bias_c_ref, bias_n_ref, sink_ref, o_ref, *, bb):
    rep = A_HEADS // A_KV_HEADS

    def one(b, carry):
        rows = pl.ds(pl.multiple_of(b * SAMPLE_ROWS, SAMPLE_ROWS), SAMPLE_ROWS)
        for g in range(A_KV_HEADS):
            heads = range(g * rep, (g + 1) * rep)
            q = jnp.concatenate([_head_rows(q_ref, h // 2, rows, h % 2) for h in heads], axis=0).astype(BF16)
            kn = _pad_rows(_head_rows(kvn_ref, 0, rows, g), LANES).astype(BF16)
            vn = _pad_rows(_head_rows(kvn_ref, 2, rows, g), LANES).astype(BF16)
            s_c = _dot(q, ck_ref[b, g].astype(BF16)) * (HEAD_DIM ** -0.5) + bias_c_ref[g]
            s_n = _dot_t(q, kn) * (HEAD_DIM ** -0.5) + bias_n_ref[g]
            (p_c, p_n), _ = _softmax_parts([s_c, s_n], sink_ref[g])
            o = _dot_t(p_c.astype(BF16), cv_ref[b, g].astype(BF16)) + _dot(p_n.astype(BF16), vn)
            for r, h in enumerate(heads):
                o_ref[h, rows, :] = o[r * SAMPLE_ROWS:(r + 1) * SAMPLE_ROWS].astype(o_ref.dtype)
        return carry

    lax.fori_loop(0, bb, one, 0)


def _attn_a_sample(qa, kva, ck, cv, bias_c, bias_n, sink, bb):
    batch = ck.shape[0]
    full = lambda a: pl.BlockSpec(a.shape, lambda n: (0,) * a.ndim)
    buf_spec = pl.BlockSpec((bb,) + ck.shape[1:], lambda n: (n, 0, 0, 0))
    return pl.pallas_call(
        functools.partial(_attn_a_sample_body, bb=bb),
        grid=(batch // bb,),
        in_specs=[pl.BlockSpec((4, bb * SAMPLE_ROWS, LANES), lambda n: (0, n, 0)),
                  pl.BlockSpec((4, bb * SAMPLE_ROWS, LANES), lambda n: (0, n, 0)),
                  buf_spec, buf_spec, full(bias_c), full(bias_n), full(sink)],
        out_specs=pl.BlockSpec((A_HEADS, bb * SAMPLE_ROWS, HEAD_DIM), lambda n: (0, n, 0)),
        out_shape=jax.ShapeDtypeStruct((A_HEADS, batch * SAMPLE_ROWS, HEAD_DIM), BF16),
        compiler_params=_cparams(("parallel",)),
        name="attn_a_sample",
    )(qa, kva, ck, cv, bias_c, bias_n, sink)


def _attn_b_sample_body(q_ref, kvn_ref, ck_ref, cv_ref, bias_c_ref, bias_n_ref, o_ref):
    n = SAMPLE_ROWS
    rows = slice(None)
    scale = HEAD_DIM ** -0.5
    for h in range(B_KV_HEADS):
        q = jnp.concatenate([_head_rows(q_ref, 2 * g + h // 2, rows, h % 2) for g in range(len(B_GROUPS))],
                            axis=0).astype(BF16)
        kn = _pad_rows(_head_rows(kvn_ref, h // 2, rows, h % 2), LANES).astype(BF16)
        vn = _pad_rows(_head_rows(kvn_ref, 2 + h // 2, rows, h % 2), LANES).astype(BF16)
        s_c = _dot(q, ck_ref[0, h].astype(BF16)) * scale + bias_c_ref[h]
        s_n = _dot_t(q, kn) * scale + bias_n_ref[h]
        (p_c, p_n), lse = _softmax_parts([s_c, s_n])
        o = _dot_t(p_c.astype(BF16), cv_ref[0, h].astype(BF16)) + _dot(p_n.astype(BF16), vn)
        o_g = [o[g * n:(g + 1) * n] for g in range(3)]
        l_g = [lse[g * n:(g + 1) * n] for g in range(3)]
        mx = jnp.maximum(jnp.maximum(l_g[0], l_g[1]), l_g[2])
        e = [jnp.exp(x - mx) for x in l_g]
        den = e[0] + e[1] + e[2]
        acc = (e[0] / den) * o_g[0] + (e[1] / den) * o_g[1] + (e[2] / den) * o_g[2]
        o_ref[h] = acc.astype(o_ref.dtype)


def _attn_b_sample(qb, kvb, ck, cv, bias_c, bias_n):
    batch = ck.shape[0]
    full = lambda a: pl.BlockSpec(a.shape, lambda n: (0,) * a.ndim)
    buf_spec = pl.BlockSpec((1,) + ck.shape[1:], lambda n: (n, 0, 0, 0))
    return pl.pallas_call(
        _attn_b_sample_body,
        grid=(batch,),
        in_specs=[pl.BlockSpec((6, SAMPLE_ROWS, LANES), lambda n: (0, n, 0)),
                  pl.BlockSpec((4, SAMPLE_ROWS, LANES), lambda n: (0, n, 0)),
                  buf_spec, buf_spec, full(bias_c), full(bias_n)],
        out_specs=pl.BlockSpec((B_KV_HEADS, SAMPLE_ROWS, HEAD_DIM), lambda n: (0, n, 0)),
        out_shape=jax.ShapeDtypeStruct((B_KV_HEADS, batch * SAMPLE_ROWS, HEAD_DIM), BF16),
        compiler_params=_cparams(("parallel",)),
        name="attn_b_sample",
    )(qb, kvb, ck, cv, bias_c, bias_n)


def _layer_norm(x, g, b):
    mu = jnp.mean(x, axis=-1, keepdims=True)
    xc = x - mu
    var = jnp.mean(xc * xc, axis=-1, keepdims=True)
    return xc * lax.rsqrt(var + LN_EPS) * g + b


def _split_bf16(x):
    hi = x.astype(BF16)
    return hi, (x - hi.astype(F32)).astype(BF16)


def _merge_body(cnt_in_ref, x_ref, oa_ref, ob_ref, om_ref, wg_ref, wa_ref, wb_ref, wm_ref, wo_ref, g1_ref, b1_ref,
                wr_hi_ref, wr_lo_ref, br_ref, h_ref, route_ref, gate_ref, cnt_ref, *, tm):
    i = pl.program_id(0)
    x = x_ref[...]
    gates = jax.nn.sigmoid(_dot(x.astype(BF16), wg_ref[...]))
    u = (gates[:, :D_MODEL] * _dot(oa_ref[...], wa_ref[...])
         + gates[:, D_MODEL:2 * D_MODEL] * _dot(ob_ref[...], wb_ref[...])
         + gates[:, 2 * D_MODEL:] * _dot(om_ref[...], wm_ref[...]))
    mixed = _dot(u.astype(BF16), wo_ref[...])
    h = _layer_norm(DEEPNORM_ALPHA * x + mixed, g1_ref[...], b1_ref[...])
    h_ref[...] = h

    h_hi, h_lo = _split_bf16(h)
    logits = _dot(h_hi, wr_hi_ref[...]) + (_dot(h_hi, wr_lo_ref[...]) + _dot(h_lo, wr_hi_ref[...])) + br_ref[...]
    lane = lax.broadcasted_iota(I32, (tm, LANES), 1)
    lane_f = lane.astype(F32)
    work = logits
    vals, sels, idxs = [], [], []
    for _ in range(TOP_K):
        mx = jnp.max(work, axis=-1, keepdims=True)
        idx = jnp.min(jnp.where(work == mx, lane_f, float(LANES)), axis=-1, keepdims=True)
        sel = lane_f == idx
        vals.append(mx)
        idxs.append(idx)
        sels.append(sel)
        work = jnp.where(sel, NEG_INF, work)
    ex = [jnp.exp(v - vals[0]) for v in vals]
    den = ex[0] + ex[1] + ex[2] + ex[3]
    chosen = jnp.logical_or(jnp.logical_or(sels[0], sels[1]), jnp.logical_or(sels[2], sels[3]))
    onehot = jnp.where(chosen, 1.0, 0.0)

    @pl.when(i == 0)
    def _():
        cnt_ref[...] = cnt_in_ref[...]

    carry = cnt_ref[...]
    row = lax.broadcasted_iota(I32, (tm, tm), 0)
    colm = lax.broadcasted_iota(I32, (tm, tm), 1)
    tri = jnp.where(colm < row, 1.0, 0.0).astype(BF16)
    before = _dot(tri, onehot.astype(BF16)) + carry
    route = jnp.zeros((tm, LANES), F32)
    gate = jnp.zeros((tm, LANES), F32)
    for k in range(TOP_K):
        rank = jnp.sum(jnp.where(sels[k], before, 0.0), axis=-1, keepdims=True)
        route = jnp.where(lane == k, idxs[k], route)
        route = jnp.where(lane == TOP_K + k, rank, route)
        gate = jnp.where(lane == k, ex[k] / den, gate)
    route_ref[...] = route.astype(I32)
    gate_ref[...] = gate
    cnt_ref[...] = carry + jnp.sum(onehot, axis=0, keepdims=True)


def _merge(cnt_in, x, oa, ob, om, w, tm):
    rows = x.shape[0]
    full = lambda a: pl.BlockSpec(a.shape, lambda i: (0,) * a.ndim)
    row_blk = lambda c: pl.BlockSpec((tm, c), lambda i: (i, 0))
    weights = (w["w_g"], w["w_a"], w["w_b"], w["w_m"], w["w_o"], w["ln1_g"], w["ln1_b"],
               w["wr_hi"], w["wr_lo"], w["b_r"])
    return pl.pallas_call(
        functools.partial(_merge_body, tm=tm),
        grid=(rows // tm,),
        in_specs=[full(cnt_in), row_blk(D_MODEL), row_blk(oa.shape[1]), row_blk(ob.shape[1]), row_blk(om.shape[1])]
        + [full(a) for a in weights],
        out_specs=[row_blk(D_MODEL), row_blk(LANES), row_blk(LANES), pl.BlockSpec((1, LANES), lambda i: (0, 0))],
        out_shape=[jax.ShapeDtypeStruct((rows, D_MODEL), F32), jax.ShapeDtypeStruct((rows, LANES), I32),
                   jax.ShapeDtypeStruct((rows, LANES), F32), jax.ShapeDtypeStruct((1, LANES), F32)],
        compiler_params=_cparams(("arbitrary",)),
        name="merge_ln_router",
    )(cnt_in, x, oa, ob, om, *weights)


def _dispatch_body(dest_ref, h_ref, xin_ref, xout_ref, sem, *, tm):
    del xin_ref

    def copy(t, k):
        d = dest_ref[t * TOP_K + k]
        return pltpu.make_async_copy(h_ref.at[pl.ds(t, 1), :], xout_ref.at[pl.ds(d, 1), :], sem)

    def start(t, carry):
        for k in range(TOP_K):
            copy(t, k).start(priority=k % 2)
        return carry

    def wait(t, carry):
        for k in range(TOP_K):
            copy(t, k).wait()
        return carry

    lax.fori_loop(0, tm, start, 0)
    lax.fori_loop(0, tm, wait, 0)


def _dispatch(dest_flat, h, x_rows, tm):
    rows = h.shape[0]
    return pl.pallas_call(
        functools.partial(_dispatch_body, tm=tm),
        grid=(rows // tm,),
        in_specs=[pl.BlockSpec((tm * TOP_K,), lambda i: (i,), memory_space=pltpu.SMEM),
                  pl.BlockSpec((tm, D_MODEL), lambda i: (i, 0)),
                  pl.BlockSpec(memory_space=pl.ANY)],
        out_specs=pl.BlockSpec(memory_space=pl.ANY),
        out_shape=jax.ShapeDtypeStruct(x_rows.shape, x_rows.dtype),
        scratch_shapes=[pltpu.SemaphoreType.DMA(())],
        input_output_aliases={2: 0},
        compiler_params=_cparams(("arbitrary",)),
        name="moe_dispatch",
    )(dest_flat, h, x_rows)


def _expert_body(be_ref, nv_ref, x_ref, wgu_ref, bgu_ref, wd_ref, bd_ref, y_ref, wgu_bf, wd_bf):
    i = pl.program_id(0)
    valid = i < nv_ref[0]

    @pl.when(valid)
    def _():
        new_expert = jnp.logical_or(i == 0, be_ref[i] != be_ref[jnp.maximum(i - 1, 0)])

        @pl.when(new_expert)
        def _():
            wgu_bf[...] = wgu_ref[0].astype(BF16)
            wd_bf[...] = wd_ref[0].astype(BF16)

        gu = _dot(x_ref[...].astype(BF16), wgu_bf[...]) + bgu_ref[0]
        g = jnp.minimum(gu[:, :D_FF], SWIGLU_LIMIT)
        u = jnp.clip(gu[:, D_FF:], -SWIGLU_LIMIT, SWIGLU_LIMIT)
        hidden = (u + 1.0) * g * jax.nn.sigmoid(SWIGLU_ALPHA * g)
        y_ref[...] = _dot(hidden.astype(BF16), wd_bf[...]) + bd_ref[0]

    @pl.when(jnp.logical_not(valid))
    def _():
        y_ref[...] = jnp.zeros_like(y_ref)


def _experts(block_expert, n_valid, x_rows, w_gu, b_gu, w_down, b_down):
    n_blocks = x_rows.shape[0] // EXPERT_ROWS
    last = lambda i, nv: jnp.minimum(i, nv[0] - 1)
    grid_spec = pltpu.PrefetchScalarGridSpec(
        num_scalar_prefetch=2,
        grid=(n_blocks,),
        in_specs=[pl.BlockSpec((EXPERT_ROWS, D_MODEL), lambda i, be, nv: (last(i, nv), 0)),
                  pl.BlockSpec((1, D_MODEL, 2 * D_FF), lambda i, be, nv: (be[last(i, nv)], 0, 0)),
                  pl.BlockSpec((1, 1, 2 * D_FF), lambda i, be, nv: (be[last(i, nv)], 0, 0)),
                  pl.BlockSpec((1, D_FF, D_MODEL), lambda i, be, nv: (be[last(i, nv)], 0, 0)),
                  pl.BlockSpec((1, 1, D_MODEL), lambda i, be, nv: (be[last(i, nv)], 0, 0))],
        out_specs=pl.BlockSpec((EXPERT_ROWS, D_MODEL), lambda i, be, nv: (i, 0)),
        scratch_shapes=[pltpu.VMEM((D_MODEL, 2 * D_FF), BF16), pltpu.VMEM((D_FF, D_MODEL), BF16)],
    )
    return pl.pallas_call(
        _expert_body,
        grid_spec=grid_spec,
        out_shape=jax.ShapeDtypeStruct(x_rows.shape, F32),
        compiler_params=_cparams(("arbitrary",)),
        name="moe_experts",
    )(block_expert, n_valid, x_rows, w_gu, b_gu.reshape(N_EXPERTS, 1, 2 * D_FF), w_down,
      b_down.reshape(N_EXPERTS, 1, D_MODEL))


def _combine_body(dest_ref, dest_next_ref, h_ref, gate_ref, g2_ref, b2_ref, yrows_ref, o_ref, buf, sems, *, tm):
    i = pl.program_id(0)
    n = pl.num_programs(0)
    slot = lax.rem(i, 2)

    def copy(d_ref, s, t, k):
        d = d_ref[t * TOP_K + k]
        return pltpu.make_async_copy(yrows_ref.at[pl.ds(d, 1), :], buf.at[s, k, pl.ds(t, 1), :], sems.at[s])

    def start_tile(d_ref, s):
        def body(t, carry):
            for k in range(TOP_K):
                copy(d_ref, s, t, k).start(priority=k % 2)
            return carry
        lax.fori_loop(0, tm, body, 0)

    @pl.when(i == 0)
    def _():
        start_tile(dest_ref, 0)

    @pl.when(i + 1 < n)
    def _():
        start_tile(dest_next_ref, 1 - slot)

    def wait(t, carry):
        for k in range(TOP_K):
            copy(dest_ref, slot, t, k).wait()
        return carry

    lax.fori_loop(0, tm, wait, 0)
    gate = gate_ref[...]
    f = gate[:, 0:1] * buf[slot, 0]
    for k in range(1, TOP_K):
        f = f + gate[:, k:k + 1] * buf[slot, k]
    o_ref[...] = _layer_norm(DEEPNORM_ALPHA * h_ref[...] + f, g2_ref[...], b2_ref[...])


def _combine(dest_flat, h, gate, ln2_g, ln2_b, y_rows, tm):
    rows = h.shape[0]
    n_tiles = rows // tm
    return pl.pallas_call(
        functools.partial(_combine_body, tm=tm),
        grid=(n_tiles,),
        in_specs=[pl.BlockSpec((tm * TOP_K,), lambda i: (i,), memory_space=pltpu.SMEM),
                  pl.BlockSpec((tm * TOP_K,), lambda i: (jnp.minimum(i + 1, n_tiles - 1),), memory_space=pltpu.SMEM),
                  pl.BlockSpec((tm, D_MODEL), lambda i: (i, 0)),
                  pl.BlockSpec((tm, LANES), lambda i: (i, 0)),
                  pl.BlockSpec((1, D_MODEL), lambda i: (0, 0)),
                  pl.BlockSpec((1, D_MODEL), lambda i: (0, 0)),
                  pl.BlockSpec(memory_space=pl.ANY)],
        out_specs=pl.BlockSpec((tm, D_MODEL), lambda i: (i, 0)),
        out_shape=jax.ShapeDtypeStruct((rows, D_MODEL), F32),
        scratch_shapes=[pltpu.VMEM((2, TOP_K, tm, D_MODEL), F32), pltpu.SemaphoreType.DMA((2,))],
        compiler_params=_cparams(("arbitrary",)),
        name="moe_combine_ln2",
    )(dest_flat, dest_flat, h, gate, ln2_g, ln2_b, y_rows)


def _t5_bucket_static(dist):
    d = np.maximum(dist, 0)
    d_large = np.maximum(d, REL_MAX_EXACT).astype(np.float32)
    val = (np.log(d_large / np.float32(REL_MAX_EXACT)) / np.float32(math.log(REL_MAX_DISTANCE / REL_MAX_EXACT))
           * np.float32(REL_BUCKETS - REL_MAX_EXACT))
    inner = (d > REL_MAX_EXACT) & (d < REL_MAX_DISTANCE)
    assert not np.any(inner & (np.abs(val - np.round(val)) < 1e-5))
    large = REL_MAX_EXACT + val.astype(np.int32)
    return np.where(d < REL_MAX_EXACT, d, np.minimum(large, REL_BUCKETS - 1)).astype(np.int32)


def _bias_table(rel_bias, dist, head, valid):
    hi = lax.Precision.HIGHEST
    head_rows = jnp.dot(jnp.asarray(np.eye(REL_HEADS, dtype=np.float32)[np.asarray(head)]),
                        rel_bias.astype(F32).T, precision=hi)
    onehot = (jnp.asarray(_t5_bucket_static(np.asarray(dist)))[..., None]
              == jnp.arange(REL_BUCKETS, dtype=I32)).astype(F32)
    tab = jnp.sum(onehot * head_rows[:, None, :], axis=-1)
    return jnp.where(jnp.asarray(np.asarray(valid)), tab, NEG_INF)


_A_SET_HEADS = ((0, 2, 5, 7), (1, 3, 4, 6))


def _tables_a_prompt(rel_bias, sinks):
    qi = np.arange(QBLK)[:, None]
    kj = np.arange(2 * QBLK)[None, :]
    dist = np.tile(QBLK + qi - kj, (4, 1))
    valid = (dist >= 0) & (dist < A_WINDOW)
    bias = jnp.stack([_bias_table(rel_bias, dist, np.repeat(heads, QBLK), valid) for heads in _A_SET_HEADS])
    s = sinks.astype(F32)
    sink = jnp.stack([jnp.repeat(jnp.stack([s[h] for h in heads]), QBLK).reshape(4 * QBLK, 1)
                      for heads in _A_SET_HEADS])
    return bias, sink


def _tables_b_prompt(rel_bias):
    qi = np.arange(QBLK)[:, None]
    kj = np.arange(2 * QBLK)[None, :]
    sub = np.tile(QBLK + qi - kj, (2, 1))
    out = []
    for g, (window, dil) in enumerate(B_GROUPS):
        valid = (sub >= 0) & (sub <= window // dil)
        out.append(jnp.stack([
            _bias_table(rel_bias, sub * dil, np.repeat(A_HEADS + g * B_KV_HEADS + 2 * sp + np.arange(2), QBLK), valid)
            for sp in range(2)]))
    return jnp.stack(out)


def _tables_a_sample(rel_bias, sinks, la):
    n = SAMPLE_ROWS
    rep = A_HEADS // A_KV_HEADS
    qi = np.tile(np.arange(n), rep)[:, None]
    dist_c = la + qi - np.arange(la)[None, :]
    coln = np.arange(LANES)[None, :]
    dist_n = qi - coln
    valid_c = (dist_c >= 0) & (dist_c < A_WINDOW)
    valid_n = (dist_n >= 0) & (dist_n < A_WINDOW) & (coln < n)
    heads = [np.repeat(np.arange(g * rep, (g + 1) * rep), n) for g in range(A_KV_HEADS)]
    bias_c = jnp.stack([_bias_table(rel_bias, dist_c, hd, valid_c) for hd in heads])
    bias_n = jnp.stack([_bias_table(rel_bias, dist_n, hd, valid_n) for hd in heads])
    sink = jnp.repeat(sinks.astype(F32), n).reshape(A_KV_HEADS, rep * n, 1)
    return bias_c, bias_n, sink


def _tables_b_sample(rel_bias, lb):
    n = SAMPLE_ROWS
    n_groups = len(B_GROUPS)
    g = np.repeat(np.arange(n_groups), n)
    qi = np.tile(np.arange(n), n_groups)[:, None]
    window = np.array([w for w, _ in B_GROUPS])[g][:, None]
    dil = np.array([d for _, d in B_GROUPS])[g][:, None]
    coln = np.arange(LANES)[None, :]

    def table(h, dist, extra):
        valid = (dist >= 0) & (dist <= window) & (dist % dil == 0) & extra
        return _bias_table(rel_bias, dist, A_HEADS + g * B_KV_HEADS + h, valid)

    bias_c = jnp.stack([table(h, lb + qi - np.arange(lb)[None, :], True) for h in range(B_KV_HEADS)])
    bias_n = jnp.stack([table(h, qi - coln, coln < n) for h in range(B_KV_HEADS)])
    return bias_c, bias_n


def _mask_mem_sample():
    row_head = np.repeat(np.arange(M_HEADS), SAMPLE_ROWS)[:, None]
    col_head = np.arange(MEM_TOKENS * M_HEADS)[None, :] % M_HEADS
    return jnp.asarray(np.where(row_head == col_head, 0.0, -np.inf).astype(np.float32))


def _swap_halves(w):
    return jnp.concatenate([w[:, HEAD_DIM:], w[:, :HEAD_DIM]], axis=1)


def _slabs_to_rows(slabs):
    return jnp.moveaxis(slabs, 0, 1).reshape(slabs.shape[1], slabs.shape[0] * LANES)


def _heads_to_rows(o):
    return jnp.moveaxis(o, 0, 1).reshape(o.shape[1], o.shape[0] * HEAD_DIM)


def kernel(x_prompt, x_sample, cache_a_k, cache_a_v, cache_b_k, cache_b_v, cache_mem_k, cache_mem_v, mem_prompt,
           rel_bias, sinks_a, w_in, w_mem_kv, w_br_a, w_br_b, w_br_m, w_o, ln1_g, ln1_b, ln2_g, ln2_b, w_router,
           b_router, w_gu, b_gu, w_down, b_down):
    assert w_in.shape[0] == DEPTH == 1
    bsz, seq, _ = x_prompt.shape
    dec_b, dec_n, _ = x_sample.shape
    assert bsz == 1 and dec_n <= SAMPLE_ROWS
    la, lb = cache_a_k.shape[2], cache_b_k.shape[2]
    assert la == A_WINDOW and lb == B_WINDOW_MAX

    w = w_in[0]
    c = np.cumsum((0, 512, 128, 128, 768, 256, 256, 512, 3072))
    w_qa, w_ka, w_va, w_qb, w_kb, w_vb, w_qm, w_g = (w[:, c[i]:c[i + 1]] for i in range(8))
    w_proj = jnp.concatenate([w_qb, w_kb, w_vb, w_qa, w_ka, _swap_halves(w_ka), w_va, _swap_halves(w_va), w_qm],
                             axis=1).astype(BF16)
    wr = jnp.pad(w_router[0].astype(F32), ((0, 0), (0, LANES - N_EXPERTS)))
    wr_hi = wr.astype(BF16)
    merge_w = dict(
        w_g=w_g.astype(BF16), w_a=w_br_a[0].astype(BF16), w_b=w_br_b[0].astype(BF16), w_m=w_br_m[0].astype(BF16),
        w_o=w_o[0].astype(BF16), ln1_g=ln1_g.astype(F32).reshape(1, D_MODEL), ln1_b=ln1_b.astype(F32).reshape(1, D_MODEL),
        wr_hi=wr_hi, wr_lo=(wr - wr_hi.astype(F32)).astype(BF16),
        b_r=jnp.pad(b_router.astype(F32).reshape(1, N_EXPERTS), ((0, 0), (0, LANES - N_EXPERTS)),
                    constant_values=NEG_INF))
    ln2g = ln2_g.astype(F32).reshape(1, D_MODEL)
    ln2b = ln2_b.astype(F32).reshape(1, D_MODEL)

    xp = x_prompt.reshape(seq, D_MODEL)
    qb_p, kvb_p, qa_p, kva_p, qm_p = _project(xp, w_proj, 512)
    mkv_p = _matmul_f32(mem_prompt.reshape(MEM_TOKENS, D_MODEL), w_mem_kv[0].astype(BF16))
    bias_a_p, sink_a_p = _tables_a_prompt(rel_bias, sinks_a[0])
    oa_p = _attn_a_prompt(qa_p, kva_p, bias_a_p, sink_a_p, 512)
    ob_p = _attn_b_prompt(qb_p, kvb_p, _tables_b_prompt(rel_bias))
    om_p = _attn_mem_prompt(qm_p, mkv_p, 512)
    zero_cnt = jnp.zeros((1, LANES), F32)
    h_p, route_p, gate_p, cnt_p = _merge(zero_cnt, xp, oa_p, ob_p, om_p, merge_w, 512)

    xs = jnp.pad(x_sample, ((0, 0), (0, SAMPLE_ROWS - dec_n), (0, 0))).reshape(dec_b * SAMPLE_ROWS, D_MODEL)
    qb_s, kvb_s, qa_s, kva_s, qm_s = _project(xs, w_proj, 512)
    bias_a_c, bias_a_n, sink_a_s = _tables_a_sample(rel_bias, sinks_a[0], la)
    to_hdt = lambda c: jnp.transpose(c[0], (0, 2, 3, 1))
    oa_s = _attn_a_sample(qa_s, kva_s, to_hdt(cache_a_k), to_hdt(cache_a_v), bias_a_c, bias_a_n, sink_a_s, 8)
    ob_s = _attn_b_sample(qb_s, kvb_s, to_hdt(cache_b_k), to_hdt(cache_b_v), *_tables_b_sample(rel_bias, lb))
    om_s = _attn_mem_sample(qm_s, cache_mem_k[0].reshape(dec_b, MEM_TOKENS * M_HEADS, M_HEAD_DIM),
                            cache_mem_v[0].reshape(dec_b, MEM_TOKENS * M_HEADS, M_HEAD_DIM), _mask_mem_sample(), 8)
    unpad = lambda a: a.reshape(dec_b, SAMPLE_ROWS, a.shape[-1])[:, :dec_n].reshape(dec_b * dec_n, a.shape[-1])
    x_s = x_sample.reshape(dec_b * dec_n, D_MODEL)
    h_s, route_s, gate_s, cnt_all = _merge(cnt_p, x_s, unpad(_heads_to_rows(oa_s)), unpad(_heads_to_rows(ob_s)),
                                           unpad(om_s), merge_w, dec_b * dec_n)

    n_tok = seq + dec_b * dec_n
    n_rows = n_tok * TOP_K + N_EXPERTS * EXPERT_ROWS
    counts = cnt_all[0, :N_EXPERTS].astype(I32)
    padded = (counts + EXPERT_ROWS - 1) // EXPERT_ROWS * EXPERT_ROWS
    pad_end = jnp.cumsum(padded)
    pad_start = pad_end - padded
    n_valid = (pad_end[-1] // EXPERT_ROWS).reshape(1).astype(I32)
    block_row = jnp.arange(n_rows // EXPERT_ROWS, dtype=I32) * EXPERT_ROWS
    block_expert = jnp.minimum(jnp.sum((pad_end[None, :] <= block_row[:, None]).astype(I32), axis=1), N_EXPERTS - 1)
    expert_ids = jnp.arange(N_EXPERTS, dtype=I32)

    def dest(route):
        start = jnp.sum(jnp.where(route[:, :TOP_K, None] == expert_ids, pad_start, 0), axis=-1)
        return (start + route[:, TOP_K:2 * TOP_K]).reshape(-1).astype(I32)

    dest_p, dest_s = dest(route_p), dest(route_s)
    x_rows = jnp.zeros((n_rows, D_MODEL), F32)
    x_rows = _dispatch(dest_p, h_p, x_rows, 512)
    x_rows = _dispatch(dest_s, h_s, x_rows, dec_b * dec_n)
    y_rows = _experts(block_expert, n_valid, x_rows, w_gu[0], b_gu[0], w_down[0], b_down[0])
    y_p = _combine(dest_p, h_p, gate_p, ln2g, ln2b, y_rows, 256)
    y_s = _combine(dest_s, h_s, gate_s, ln2g, ln2b, y_rows, 256)

    kva_rows = _slabs_to_rows(kva_p[:, seq - la:])
    kvb_rows = _slabs_to_rows(kvb_p[:, seq - lb:])
    a_k_p = kva_rows[:, 0:128].reshape(1, 1, la, A_KV_HEADS, HEAD_DIM)
    a_v_p = kva_rows[:, 256:384].reshape(1, 1, la, A_KV_HEADS, HEAD_DIM)
    b_k_p = kvb_rows[:, 0:256].reshape(1, 1, lb, B_KV_HEADS, HEAD_DIM)
    b_v_p = kvb_rows[:, 256:512].reshape(1, 1, lb, B_KV_HEADS, HEAD_DIM)
    m_k_p = mkv_p[:, :M_HEADS * LANES].reshape(1, 1, MEM_TOKENS, M_HEADS, M_HEAD_DIM)
    m_v_p = mkv_p[:, M_HEADS * LANES:].reshape(1, 1, MEM_TOKENS, M_HEADS, M_HEAD_DIM)
    kva_s_rows = unpad(_slabs_to_rows(kva_s))
    kvb_s_rows = unpad(_slabs_to_rows(kvb_s))
    a_k_s = kva_s_rows[:, 0:128].reshape(1, dec_b, dec_n, A_KV_HEADS, HEAD_DIM)
    a_v_s = kva_s_rows[:, 256:384].reshape(1, dec_b, dec_n, A_KV_HEADS, HEAD_DIM)
    b_k_s = kvb_s_rows[:, 0:256].reshape(1, dec_b, dec_n, B_KV_HEADS, HEAD_DIM)
    b_v_s = kvb_s_rows[:, 256:512].reshape(1, dec_b, dec_n, B_KV_HEADS, HEAD_DIM)
    return (y_p.reshape(bsz, seq, D_MODEL), y_s.reshape(dec_b, dec_n, D_MODEL),
            a_k_p, a_v_p, b_k_p, b_v_p, m_k_p, m_v_p, a_k_s, a_v_s, b_k_s, b_v_s)
```

```python
import functools
import math

import numpy as np
import jax
import jax.numpy as jnp
from jax import lax
from jax.experimental import pallas as pl
from jax.experimental.pallas import tpu as pltpu

F32 = jnp.float32
BF16 = jnp.bfloat16
I32 = jnp.int32

D_MODEL = 1024
HEAD_DIM = 64
A_HEADS = 8
A_KV_HEADS = 2
A_WINDOW = 128
B_KV_HEADS = 4
B_GROUPS = ((128, 1), (512, 4), (2048, 16))
B_WINDOW_MAX = 2048
MEM_TOKENS = 256
M_HEADS = 4
M_HEAD_DIM = 128
REL_BUCKETS = 32
REL_MAX_EXACT = REL_BUCKETS // 2
REL_MAX_DISTANCE = B_WINDOW_MAX
REL_HEADS = A_HEADS + B_KV_HEADS * len(B_GROUPS)
N_EXPERTS = 32
TOP_K = 4
D_FF = D_MODEL
SWIGLU_LIMIT = 7.0
SWIGLU_ALPHA = 1.702
LN_EPS = 1e-5
DEPTH = 1
DEEPNORM_ALPHA = (2 * DEPTH) ** 0.25

LANES = 128
TILE_ROWS = 8
QBLK = 128
SAMPLE_ROWS = 8
EXPERT_ROWS = 512
VMEM_LIMIT = 56 * 1024 * 1024
NEG_INF = float("-inf")
B_DIL_MAX = B_GROUPS[-1][1]
SAMPLE_UNROLL = 4
B_UNIT_UNROLL = 8

PROJ_SLABS = (6, 4, 4, 4, 4)


def _cparams(sem):
    return pltpu.CompilerParams(dimension_semantics=sem, vmem_limit_bytes=VMEM_LIMIT)


def _dot(a, b):
    return jnp.dot(a, b, preferred_element_type=F32)


def _dot_t(a, b):
    return lax.dot_general(a, b, (((1,), (1,)), ((), ())), preferred_element_type=F32)


def _lo_mask(rows):
    return lax.broadcasted_iota(I32, (rows, LANES), 1) < HEAD_DIM


def _proj_body(x_ref, w_ref, *o_refs):
    x = x_ref[...].astype(BF16)
    col = 0
    for o_ref in o_refs:
        n = o_ref.shape[0] * LANES
        acc = _dot(x, w_ref[:, col:col + n])
        for s in range(o_ref.shape[0]):
            o_ref[s] = acc[:, s * LANES:(s + 1) * LANES]
        col += n


def _project(x, w_bf, tm):
    rows = x.shape[0]
    n_cols = w_bf.shape[1]
    assert rows % tm == 0 and n_cols == sum(PROJ_SLABS) * LANES
    return pl.pallas_call(
        _proj_body,
        grid=(rows // tm,),
        in_specs=[pl.BlockSpec((tm, D_MODEL), lambda i: (i, 0)),
                  pl.BlockSpec((D_MODEL, n_cols), lambda i: (0, 0))],
        out_specs=[pl.BlockSpec((n, tm, LANES), lambda i: (0, i, 0)) for n in PROJ_SLABS],
        out_shape=[jax.ShapeDtypeStruct((n, rows, LANES), F32) for n in PROJ_SLABS],
        compiler_params=_cparams(("parallel",)),
        name="in_proj",
    )(x, w_bf)


def _matmul_body(x_ref, w_ref, o_ref):
    o_ref[...] = _dot(x_ref[...].astype(BF16), w_ref[...])


def _matmul_f32(x, w_bf):
    return pl.pallas_call(
        _matmul_body,
        out_shape=jax.ShapeDtypeStruct((x.shape[0], w_bf.shape[1]), F32),
        compiler_params=_cparams(None),
        name="mem_kv_proj",
    )(x, w_bf)


def _softmax_sink(s, sink):
    m = jnp.maximum(jnp.max(s, axis=-1, keepdims=True), sink)
    p = jnp.exp(s - m)
    denom = jnp.sum(p, axis=-1, keepdims=True) + jnp.exp(sink - m)
    return p / denom


def _softmax_lse(s):
    m = jnp.max(s, axis=-1, keepdims=True)
    p = jnp.exp(s - m)
    l = jnp.sum(p, axis=-1, keepdims=True)
    return p / l, m + jnp.log(l)


def _softmax_parts(parts, sink=None):
    m = functools.reduce(jnp.maximum, [jnp.max(s, axis=-1, keepdims=True) for s in parts])
    if sink is not None:
        m = jnp.maximum(m, sink)
    ps = [jnp.exp(s - m) for s in parts]
    l = functools.reduce(jnp.add, [jnp.sum(p, axis=-1, keepdims=True) for p in ps])
    lse = m + jnp.log(l)
    if sink is not None:
        l = l + jnp.exp(sink - m)
    return [p / l for p in ps], lse


def _attn_a_prompt_body(q_ref, kvc_ref, kvp_ref, bias_ref, sink_ref, o_ref, *, nsub):
    first = pl.program_id(0) == 0
    lo = _lo_mask(QBLK)
    col = lax.broadcasted_iota(I32, (4 * QBLK, 2 * QBLK), 1)
    for j in range(nsub):
        rows = slice(j * QBLK, (j + 1) * QBLK)

        def keys(slab):
            cur = kvc_ref[slab, rows, :]
            prev = kvp_ref[slab] if j == 0 else kvc_ref[slab, (j - 1) * QBLK:j * QBLK, :]
            return jnp.concatenate([prev, cur], axis=0).astype(BF16)

        q = [q_ref[p, rows, :] for p in range(4)]
        zero = jnp.zeros_like(q[0])
        q_sets = (
            jnp.concatenate([jnp.where(lo, q[0], zero), jnp.where(lo, q[1], zero),
                             jnp.where(lo, zero, q[2]), jnp.where(lo, zero, q[3])], axis=0).astype(BF16),
            jnp.concatenate([jnp.where(lo, zero, q[0]), jnp.where(lo, zero, q[1]),
                             jnp.where(lo, q[2], zero), jnp.where(lo, q[3], zero)], axis=0).astype(BF16),
        )
        outs = []
        for st in range(2):
            s = _dot_t(q_sets[st], keys(st)) * (HEAD_DIM ** -0.5) + bias_ref[st]
            if j == 0:
                s = jnp.where(jnp.logical_and(first, col < QBLK), NEG_INF, s)
            p = _softmax_sink(s, sink_ref[st])
            outs.append(_dot(p.astype(BF16), keys(2 + st)))
        o1, o2 = outs
        blk = lambda o, r: o[r * QBLK:(r + 1) * QBLK]
        slabs = (jnp.where(lo, blk(o1, 0), blk(o2, 0)), jnp.where(lo, blk(o1, 1), blk(o2, 1)),
                 jnp.where(lo, blk(o2, 2), blk(o1, 2)), jnp.where(lo, blk(o2, 3), blk(o1, 3)))
        for p in range(4):
            o_ref[rows, p * LANES:(p + 1) * LANES] = slabs[p].astype(o_ref.dtype)


def _attn_a_prompt(qa, kva, bias, sink, tq):
    seq = qa.shape[1]
    nsub = tq // QBLK
    return pl.pallas_call(
        functools.partial(_attn_a_prompt_body, nsub=nsub),
        grid=(seq // tq,),
        in_specs=[pl.BlockSpec((4, tq, LANES), lambda n: (0, n, 0)),
                  pl.BlockSpec((4, tq, LANES), lambda n: (0, n, 0)),
                  pl.BlockSpec((4, QBLK, LANES), lambda n: (0, jnp.maximum(n * nsub - 1, 0), 0)),
                  pl.BlockSpec((2, 4 * QBLK, 2 * QBLK), lambda n: (0, 0, 0)),
                  pl.BlockSpec((2, 4 * QBLK, 1), lambda n: (0, 0, 0))],
        out_specs=pl.BlockSpec((tq, 4 * LANES), lambda n: (n, 0)),
        out_shape=jax.ShapeDtypeStruct((seq, 4 * LANES), BF16),
        compiler_params=_cparams(("parallel",)),
        name="attn_a_prompt",
    )(qa, kva, kva, bias, sink)


def _attn_b_prompt_body(q_ref, kvc_ref, kvp_ref, bias_ref, o_ref, kv_buf, o_buf, l_buf, *, sb):
    first = pl.program_id(0) == 0
    kv_buf[:, :sb, :] = kvp_ref[...]
    kv_buf[:, sb:, :] = kvc_ref[...]
    lo = _lo_mask(QBLK)
    col = lax.broadcasted_iota(I32, (2 * QBLK, 2 * QBLK), 1)
    units = sb // QBLK
    for g, (_, dil) in enumerate(B_GROUPS):
        blocks_per_class = units // dil

        def unit(u, carry, g=g, dil=dil, blocks_per_class=blocks_per_class):
            r = u // blocks_per_class
            m = u % blocks_per_class
            q_start = r + dil * QBLK * m
            k_start = sb - dil * QBLK + q_start
            if dil == 1:
                q_start = pl.multiple_of(q_start, QBLK)
                k_start = pl.multiple_of(k_start, QBLK)
                q_idx = pl.ds(q_start, QBLK)
                k_idx = pl.ds(k_start, 2 * QBLK)
            else:
                q_idx = pl.ds(q_start, QBLK, stride=dil)
                k_idx = pl.ds(k_start, 2 * QBLK, stride=dil)
            for sp in range(2):
                q = q_ref[2 * g + sp, q_idx, :]
                zero = jnp.zeros_like(q)
                q2 = jnp.concatenate([jnp.where(lo, q, zero), jnp.where(lo, zero, q)], axis=0).astype(BF16)
                k = kv_buf[sp, k_idx, :].astype(BF16)
                v = kv_buf[2 + sp, k_idx, :].astype(BF16)
                s = _dot_t(q2, k) * (HEAD_DIM ** -0.5) + bias_ref[g, sp]
                s = jnp.where(jnp.logical_and(jnp.logical_and(first, m == 0), col < QBLK), NEG_INF, s)
                p, lse = _softmax_lse(s)
                o = _dot(p.astype(BF16), v)
                o_buf[g, sp, q_idx, :] = jnp.where(lo, o[:QBLK], o[QBLK:])
                l_buf[g, sp, q_idx, :] = jnp.where(lo, lse[:QBLK], lse[QBLK:])
            return carry

        lax.fori_loop(0, units, unit, 0, unroll=B_UNIT_UNROLL)
    for sp in range(2):
        l = [l_buf[g, sp] for g in range(3)]
        mx = jnp.maximum(jnp.maximum(l[0], l[1]), l[2])
        e = [jnp.exp(x - mx) for x in l]
        den = e[0] + e[1] + e[2]
        acc = (e[0] / den) * o_buf[0, sp] + (e[1] / den) * o_buf[1, sp] + (e[2] / den) * o_buf[2, sp]
        o_ref[:, sp * LANES:(sp + 1) * LANES] = acc.astype(o_ref.dtype)


def _attn_b_prompt(qb, kvb, bias):
    seq = qb.shape[1]
    sb = B_DIL_MAX * QBLK
    assert seq % sb == 0
    return pl.pallas_call(
        functools.partial(_attn_b_prompt_body, sb=sb),
        grid=(seq // sb,),
        in_specs=[pl.BlockSpec((6, sb, LANES), lambda n: (0, n, 0)),
                  pl.BlockSpec((4, sb, LANES), lambda n: (0, n, 0)),
                  pl.BlockSpec((4, sb, LANES), lambda n: (0, jnp.maximum(n - 1, 0), 0)),
                  pl.BlockSpec((3, 2, 2 * QBLK, 2 * QBLK), lambda n: (0, 0, 0, 0))],
        out_specs=pl.BlockSpec((sb, 2 * LANES), lambda n: (n, 0)),
        out_shape=jax.ShapeDtypeStruct((seq, 2 * LANES), BF16),
        scratch_shapes=[pltpu.VMEM((4, 2 * sb, LANES), F32),
                        pltpu.VMEM((3, 2, sb, LANES), F32),
                        pltpu.VMEM((3, 2, sb, LANES), F32)],
        compiler_params=_cparams(("parallel",)),
        name="attn_b_prompt",
    )(qb, kvb, kvb, bias)


def _attn_mem_prompt_body(q_ref, mkv_ref, o_ref):
    for h in range(M_HEADS):
        k = mkv_ref[:, h * LANES:(h + 1) * LANES].astype(BF16)
        v = mkv_ref[:, (M_HEADS + h) * LANES:(M_HEADS + h + 1) * LANES].astype(BF16)
        s = _dot_t(q_ref[h].astype(BF16), k) * (M_HEAD_DIM ** -0.5)
        p, _ = _softmax_lse(s)
        o_ref[:, h * LANES:(h + 1) * LANES] = _dot(p.astype(BF16), v).astype(o_ref.dtype)


def _attn_mem_prompt(qm, mkv, tq):
    seq = qm.shape[1]
    return pl.pallas_call(
        _attn_mem_prompt_body,
        grid=(seq // tq,),
        in_specs=[pl.BlockSpec((M_HEADS, tq, LANES), lambda n: (0, n, 0)),
                  pl.BlockSpec((MEM_TOKENS, 2 * M_HEADS * LANES), lambda n: (0, 0))],
        out_specs=pl.BlockSpec((tq, M_HEADS * LANES), lambda n: (n, 0)),
        out_shape=jax.ShapeDtypeStruct((seq, M_HEADS * LANES), BF16),
        compiler_params=_cparams(("parallel",)),
        name="attn_mem_prompt",
    )(qm, mkv)


def _attn_mem_sample_body(q_ref, mk_ref, mv_ref, mask_ref, o_ref, *, bb):
    def one(b, carry):
        rows = pl.ds(pl.multiple_of(b * SAMPLE_ROWS, SAMPLE_ROWS), SAMPLE_ROWS)
        q = jnp.concatenate([q_ref[h, rows, :] for h in range(M_HEADS)], axis=0).astype(BF16)
        s = _dot_t(q, mk_ref[b].astype(BF16)) * (M_HEAD_DIM ** -0.5) + mask_ref[...]
        p, _ = _softmax_lse(s)
        o = _dot(p.astype(BF16), mv_ref[b].astype(BF16))
        for h in range(M_HEADS):
            o_ref[rows, h * LANES:(h + 1) * LANES] = o[h * SAMPLE_ROWS:(h + 1) * SAMPLE_ROWS].astype(o_ref.dtype)
        return carry

    lax.fori_loop(0, bb, one, 0, unroll=SAMPLE_UNROLL)


def _attn_mem_sample(qm, mk, mv, mask, bb):
    batch, n_rows = mk.shape[0], mk.shape[1]
    return pl.pallas_call(
        functools.partial(_attn_mem_sample_body, bb=bb),
        grid=(batch // bb,),
        in_specs=[pl.BlockSpec((M_HEADS, bb * SAMPLE_ROWS, LANES), lambda n: (0, n, 0)),
                  pl.BlockSpec((bb, n_rows, LANES), lambda n: (n, 0, 0)),
                  pl.BlockSpec((bb, n_rows, LANES), lambda n: (n, 0, 0)),
                  pl.BlockSpec(mask.shape, lambda n: (0, 0))],
        out_specs=pl.BlockSpec((bb * SAMPLE_ROWS, M_HEADS * LANES), lambda n: (n, 0)),
        out_shape=jax.ShapeDtypeStruct((batch * SAMPLE_ROWS, M_HEADS * LANES), BF16),
        compiler_params=_cparams(("parallel",)),
        name="attn_mem_sample",
    )(qm, mk, mv, mask)


def _head_rows(ref, slab, rows, half):
    x = ref[slab, rows, :]
    return x[:, half * HEAD_DIM:(half + 1) * HEAD_DIM]


def _pad_rows(x, n):
    return jnp.concatenate([x, jnp.zeros((n - x.shape[0], x.shape[1]), x.dtype)], axis=0)


def _attn_a_sample_body(q_ref, kvn_ref, ck_ref, cv_ref, bias_c_ref, bias_n_ref, sink_ref, o_ref, *, bb):
    rep = A_HEADS // A_KV_HEADS

    def one(b, carry):
        rows = pl.ds(pl.multiple_of(b * SAMPLE_ROWS, SAMPLE_ROWS), SAMPLE_ROWS)
        for g in range(A_KV_HEADS):
            heads = range(g * rep, (g + 1) * rep)
            q = jnp.concatenate([_head_rows(q_ref, h // 2, rows, h % 2) for h in heads], axis=0).astype(BF16)
            kn = _pad_rows(_head_rows(kvn_ref, 0, rows, g), LANES).astype(BF16)
            vn = _pad_rows(_head_rows(kvn_ref, 2, rows, g), LANES).astype(BF16)
            s_c = _dot(q, ck_ref[b, g].astype(BF16)) * (HEAD_DIM ** -0.5) + bias_c_ref[g]
            s_n = _dot_t(q, kn) * (HEAD_DIM ** -0.5) + bias_n_ref[g]
            (p_c, p_n), _ = _softmax_parts([s_c, s_n], sink_ref[g])
            o = _dot_t(p_c.astype(BF16), cv_ref[b, g].astype(BF16)) + _dot(p_n.astype(BF16), vn)
            for r, h in enumerate(heads):
                o_ref[h, rows, :] = o[r * SAMPLE_ROWS:(r + 1) * SAMPLE_ROWS].astype(o_ref.dtype)
        return carry

    lax.fori_loop(0, bb, one, 0, unroll=SAMPLE_UNROLL)


def _attn_a_sample(qa, kva, ck, cv, bias_c, bias_n, sink, bb):
    batch = ck.shape[0]
    full = lambda a: pl.BlockSpec(a.shape, lambda n: (0,) * a.ndim)
    buf_spec = pl.BlockSpec((bb,) + ck.shape[1:], lambda n: (n, 0, 0, 0))
    return pl.pallas_call(
        functools.partial(_attn_a_sample_body, bb=bb),
        grid=(batch // bb,),
        in_specs=[pl.BlockSpec((4, bb * SAMPLE_ROWS, LANES), lambda n: (0, n, 0)),
                  pl.BlockSpec((4, bb * SAMPLE_ROWS, LANES), lambda n: (0, n, 0)),
                  buf_spec, buf_spec, full(bias_c), full(bias_n), full(sink)],
        out_specs=pl.BlockSpec((A_HEADS, bb * SAMPLE_ROWS, HEAD_DIM), lambda n: (0, n, 0)),
        out_shape=jax.ShapeDtypeStruct((A_HEADS, batch * SAMPLE_ROWS, HEAD_DIM), BF16),
        compiler_params=_cparams(("parallel",)),
        name="attn_a_sample",
    )(qa, kva, ck, cv, bias_c, bias_n, sink)


def _attn_b_sample_body(q_ref, kvn_ref, ck_ref, cv_ref, bias_c_ref, bias_n_ref, o_ref, *, bb):
    n = SAMPLE_ROWS
    scale = HEAD_DIM ** -0.5
    for b in range(bb):
        rows = slice(b * n, (b + 1) * n)
        for h in range(B_KV_HEADS):
            q = jnp.concatenate([_head_rows(q_ref, 2 * g + h // 2, rows, h % 2) for g in range(len(B_GROUPS))],
                                axis=0).astype(BF16)
            kn = _pad_rows(_head_rows(kvn_ref, h // 2, rows, h % 2), LANES).astype(BF16)
            vn = _pad_rows(_head_rows(kvn_ref, 2 + h // 2, rows, h % 2), LANES).astype(BF16)
            s_c = _dot(q, ck_ref[b, h].astype(BF16)) * scale + bias_c_ref[h]
            s_n = _dot_t(q, kn) * scale + bias_n_ref[h]
            (p_c, p_n), lse = _softmax_parts([s_c, s_n])
            o = _dot_t(p_c.astype(BF16), cv_ref[b, h].astype(BF16)) + _dot(p_n.astype(BF16), vn)
            o_g = [o[g * n:(g + 1) * n] for g in range(3)]
            l_g = [lse[g * n:(g + 1) * n] for g in range(3)]
            mx = jnp.maximum(jnp.maximum(l_g[0], l_g[1]), l_g[2])
            e = [jnp.exp(x - mx) for x in l_g]
            den = e[0] + e[1] + e[2]
            acc = (e[0] / den) * o_g[0] + (e[1] / den) * o_g[1] + (e[2] / den) * o_g[2]
            o_ref[h, rows, :] = acc.astype(o_ref.dtype)


def _attn_b_sample(qb, kvb, ck, cv, bias_c, bias_n, bb):
    batch = ck.shape[0]
    full = lambda a: pl.BlockSpec(a.shape, lambda n: (0,) * a.ndim)
    buf_spec = pl.BlockSpec((bb,) + ck.shape[1:], lambda n: (n, 0, 0, 0))
    return pl.pallas_call(
        functools.partial(_attn_b_sample_body, bb=bb),
        grid=(batch // bb,),
        in_specs=[pl.BlockSpec((6, bb * SAMPLE_ROWS, LANES), lambda n: (0, n, 0)),
                  pl.BlockSpec((4, bb * SAMPLE_ROWS, LANES), lambda n: (0, n, 0)),
                  buf_spec, buf_spec, full(bias_c), full(bias_n)],
        out_specs=pl.BlockSpec((B_KV_HEADS, bb * SAMPLE_ROWS, HEAD_DIM), lambda n: (0, n, 0)),
        out_shape=jax.ShapeDtypeStruct((B_KV_HEADS, batch * SAMPLE_ROWS, HEAD_DIM), BF16),
        compiler_params=_cparams(("parallel",)),
        name="attn_b_sample",
    )(qb, kvb, ck, cv, bias_c, bias_n)


def _layer_norm(x, g, b):
    mu = jnp.mean(x, axis=-1, keepdims=True)
    xc = x - mu
    var = jnp.mean(xc * xc, axis=-1, keepdims=True)
    return xc * lax.rsqrt(var + LN_EPS) * g + b


def _split_bf16(x):
    hi = x.astype(BF16)
    return hi, (x - hi.astype(F32)).astype(BF16)


def _merge_body(cnt_in_ref, x_ref, oa_ref, ob_ref, om_ref, wg_ref, wa_ref, wb_ref, wm_ref, wo_ref, g1_ref, b1_ref,
                wr_hi_ref, wr_lo_ref, br_ref, h_ref, route_ref, gate_ref, cnt_ref, *zero_fill, tm, zero_blocks, per_step):
    i = pl.program_id(0)
    if zero_blocks:
        rows_ref, zbuf, zsem = zero_fill
        blk = EXPERT_ROWS * TILE_ROWS

        @pl.when(i == 0)
        def _():
            zbuf[...] = jnp.zeros_like(zbuf)

        def zero_copy(j):
            start = pl.multiple_of((i * per_step + j) * blk, blk)
            return pltpu.make_async_copy(zbuf, rows_ref.at[pl.ds(start, blk), :], zsem)

        def zero_each(fn):
            def body(j, carry):
                @pl.when(i * per_step + j < zero_blocks)
                def _():
                    fn(zero_copy(j))
                return carry
            lax.fori_loop(0, per_step, body, 0)

        zero_each(lambda c: c.start())
    x = x_ref[...]
    gates = jax.nn.sigmoid(_dot(x.astype(BF16), wg_ref[...]))
    u = (gates[:, :D_MODEL] * _dot(oa_ref[...], wa_ref[...])
         + gates[:, D_MODEL:2 * D_MODEL] * _dot(ob_ref[...], wb_ref[...])
         + gates[:, 2 * D_MODEL:] * _dot(om_ref[...], wm_ref[...]))
    mixed = _dot(u.astype(BF16), wo_ref[...])
    h = _layer_norm(DEEPNORM_ALPHA * x + mixed, g1_ref[...], b1_ref[...])
    for c in range(TILE_ROWS):
        h_ref[pl.ds(c, tm, stride=TILE_ROWS), :] = h[:, c * LANES:(c + 1) * LANES]

    h_hi, h_lo = _split_bf16(h)
    logits = _dot(h_hi, wr_hi_ref[...]) + (_dot(h_hi, wr_lo_ref[...]) + _dot(h_lo, wr_hi_ref[...])) + br_ref[...]
    lane = lax.broadcasted_iota(I32, (tm, LANES), 1)
    lane_f = lane.astype(F32)
    work = logits
    vals, sels, idxs = [], [], []
    for _ in range(TOP_K):
        mx = jnp.max(work, axis=-1, keepdims=True)
        idx = jnp.min(jnp.where(work == mx, lane_f, float(LANES)), axis=-1, keepdims=True)
        sel = lane_f == idx
        vals.append(mx)
        idxs.append(idx)
        sels.append(sel)
        work = jnp.where(sel, NEG_INF, work)
    ex = [jnp.exp(v - vals[0]) for v in vals]
    den = ex[0] + ex[1] + ex[2] + ex[3]
    chosen = jnp.logical_or(jnp.logical_or(sels[0], sels[1]), jnp.logical_or(sels[2], sels[3]))
    onehot = jnp.where(chosen, 1.0, 0.0)

    @pl.when(i == 0)
    def _():
        cnt_ref[...] = cnt_in_ref[...]

    carry = cnt_ref[...]
    row = lax.broadcasted_iota(I32, (tm, tm), 0)
    colm = lax.broadcasted_iota(I32, (tm, tm), 1)
    tri = jnp.where(colm < row, 1.0, 0.0).astype(BF16)
    before = _dot(tri, onehot.astype(BF16)) + carry
    route = jnp.zeros((tm, LANES), F32)
    gate = jnp.zeros((tm, LANES), F32)
    for k in range(TOP_K):
        rank = jnp.sum(jnp.where(sels[k], before, 0.0), axis=-1, keepdims=True)
        route = jnp.where(lane == k, idxs[k], route)
        route = jnp.where(lane == TOP_K + k, rank, route)
        gate = jnp.where(lane == k, ex[k] / den, gate)
    route_ref[...] = route.astype(I32)
    gate_ref[...] = gate
    cnt_ref[...] = carry + jnp.sum(onehot, axis=0, keepdims=True)
    if zero_blocks:
        zero_each(lambda c: c.wait())


def _merge(cnt_in, x, oa, ob, om, w, tm, zero_rows=0):
    rows = x.shape[0]
    assert zero_rows % EXPERT_ROWS == 0
    zero_blocks = zero_rows // EXPERT_ROWS
    extra_specs = [pl.BlockSpec(memory_space=pl.ANY)] if zero_blocks else []
    extra_shapes = [jax.ShapeDtypeStruct((zero_rows * TILE_ROWS, LANES), F32)] if zero_blocks else []
    scratch = ([pltpu.VMEM((EXPERT_ROWS * TILE_ROWS, LANES), F32), pltpu.SemaphoreType.DMA(())]
               if zero_blocks else [])
    full = lambda a: pl.BlockSpec(a.shape, lambda i: (0,) * a.ndim)
    row_blk = lambda c: pl.BlockSpec((tm, c), lambda i: (i, 0))
    weights = (w["w_g"], w["w_a"], w["w_b"], w["w_m"], w["w_o"], w["ln1_g"], w["ln1_b"],
               w["wr_hi"], w["wr_lo"], w["b_r"])
    return pl.pallas_call(
        functools.partial(_merge_body, tm=tm, zero_blocks=zero_blocks, per_step=-(-zero_blocks // (rows // tm))),
        grid=(rows // tm,),
        in_specs=[full(cnt_in), row_blk(D_MODEL), row_blk(oa.shape[1]), row_blk(ob.shape[1]), row_blk(om.shape[1])]
        + [full(a) for a in weights],
        out_specs=[pl.BlockSpec((tm * TILE_ROWS, LANES), lambda i: (i, 0)), row_blk(LANES), row_blk(LANES),
                   pl.BlockSpec((1, LANES), lambda i: (0, 0))] + extra_specs,
        out_shape=[jax.ShapeDtypeStruct((rows * TILE_ROWS, LANES), F32), jax.ShapeDtypeStruct((rows, LANES), I32),
                   jax.ShapeDtypeStruct((rows, LANES), F32), jax.ShapeDtypeStruct((1, LANES), F32)] + extra_shapes,
        scratch_shapes=scratch,
        compiler_params=_cparams(("arbitrary",)),
        name="merge_ln_router",
    )(cnt_in, x, oa, ob, om, *weights)


def _token_tile(ref, t):
    return ref.at[pl.ds(pl.multiple_of(t * TILE_ROWS, TILE_ROWS), TILE_ROWS), :]


def _natural_rows(ref, n, lead=()):
    return jnp.concatenate([ref[lead + (pl.ds(c, n, stride=TILE_ROWS), slice(None))] for c in range(TILE_ROWS)], axis=1)


def _dispatch_body(dest_ref, h_ref, xin_ref, xout_ref, sem, *, tm):
    del xin_ref

    def copy(t, k):
        return pltpu.make_async_copy(_token_tile(h_ref, t), _token_tile(xout_ref, dest_ref[t * TOP_K + k]), sem)

    def start(t, carry):
        for k in range(TOP_K):
            copy(t, k).start()
        return carry

    def wait(t, carry):
        for k in range(TOP_K):
            copy(t, k).wait()
        return carry

    lax.fori_loop(0, tm, start, 0)
    lax.fori_loop(0, tm, wait, 0)


def _dispatch(dest_flat, h, x_rows, tm):
    rows = h.shape[0] // TILE_ROWS
    return pl.pallas_call(
        functools.partial(_dispatch_body, tm=tm),
        grid=(rows // tm,),
        in_specs=[pl.BlockSpec((tm * TOP_K,), lambda i: (i,), memory_space=pltpu.SMEM),
                  pl.BlockSpec((tm * TILE_ROWS, LANES), lambda i: (i, 0)),
                  pl.BlockSpec(memory_space=pl.ANY)],
        out_specs=pl.BlockSpec(memory_space=pl.ANY),
        out_shape=jax.ShapeDtypeStruct(x_rows.shape, x_rows.dtype),
        scratch_shapes=[pltpu.SemaphoreType.DMA(())],
        input_output_aliases={2: 0},
        compiler_params=_cparams(("arbitrary",)),
        name="moe_dispatch",
    )(dest_flat, h, x_rows)


def _expert_body(be_ref, nv_ref, x_ref, wgu_ref, bgu_ref, wd_ref, bd_ref, y_ref, wgu_bf, wd_bf):
    i = pl.program_id(0)
    valid = i < nv_ref[0]

    @pl.when(valid)
    def _():
        new_expert = jnp.logical_or(i == 0, be_ref[i] != be_ref[jnp.maximum(i - 1, 0)])

        @pl.when(new_expert)
        def _():
            wgu_bf[...] = wgu_ref[0].astype(BF16)
            wd_bf[...] = wd_ref[0].astype(BF16)

        gu = _dot(_natural_rows(x_ref, EXPERT_ROWS).astype(BF16), wgu_bf[...]) + bgu_ref[0]
        g = jnp.minimum(gu[:, :D_FF], SWIGLU_LIMIT)
        u = jnp.clip(gu[:, D_FF:], -SWIGLU_LIMIT, SWIGLU_LIMIT)
        hidden = (u + 1.0) * g * jax.nn.sigmoid(SWIGLU_ALPHA * g)
        y = _dot(hidden.astype(BF16), wd_bf[...]) + bd_ref[0]
        for c in range(TILE_ROWS):
            y_ref[pl.ds(c, EXPERT_ROWS, stride=TILE_ROWS), :] = y[:, c * LANES:(c + 1) * LANES]

    @pl.when(jnp.logical_not(valid))
    def _():
        y_ref[...] = jnp.zeros_like(y_ref)


def _experts(block_expert, n_valid, x_rows, w_gu, b_gu, w_down, b_down):
    blk_rows = EXPERT_ROWS * TILE_ROWS
    n_blocks = x_rows.shape[0] // blk_rows
    last = lambda i, nv: jnp.minimum(i, nv[0] - 1)
    grid_spec = pltpu.PrefetchScalarGridSpec(
        num_scalar_prefetch=2,
        grid=(n_blocks,),
        in_specs=[pl.BlockSpec((blk_rows, LANES), lambda i, be, nv: (last(i, nv), 0)),
                  pl.BlockSpec((1, D_MODEL, 2 * D_FF), lambda i, be, nv: (be[last(i, nv)], 0, 0)),
                  pl.BlockSpec((1, 1, 2 * D_FF), lambda i, be, nv: (be[last(i, nv)], 0, 0)),
                  pl.BlockSpec((1, D_FF, D_MODEL), lambda i, be, nv: (be[last(i, nv)], 0, 0)),
                  pl.BlockSpec((1, 1, D_MODEL), lambda i, be, nv: (be[last(i, nv)], 0, 0))],
        out_specs=pl.BlockSpec((blk_rows, LANES), lambda i, be, nv: (i, 0)),
        scratch_shapes=[pltpu.VMEM((D_MODEL, 2 * D_FF), BF16), pltpu.VMEM((D_FF, D_MODEL), BF16)],
    )
    return pl.pallas_call(
        _expert_body,
        grid_spec=grid_spec,
        out_shape=jax.ShapeDtypeStruct(x_rows.shape, F32),
        compiler_params=_cparams(("arbitrary",)),
        name="moe_experts",
    )(block_expert, n_valid, x_rows, w_gu, b_gu.reshape(N_EXPERTS, 1, 2 * D_FF), w_down,
      b_down.reshape(N_EXPERTS, 1, D_MODEL))


def _combine_body(dest_ref, dest_next_ref, h_ref, gate_ref, g2_ref, b2_ref, yrows_ref, o_ref, buf, sems, *, tm):
    i = pl.program_id(0)
    n = pl.num_programs(0)
    slot = lax.rem(i, 2)

    def copy(d_ref, s, t, k):
        return pltpu.make_async_copy(_token_tile(yrows_ref, d_ref[t * TOP_K + k]), _token_tile(buf.at[s, k], t),
                                     sems.at[s])

    def start_tile(d_ref, s):
        def body(t, carry):
            for k in range(TOP_K):
                copy(d_ref, s, t, k).start()
            return carry
        lax.fori_loop(0, tm, body, 0)

    @pl.when(i == 0)
    def _():
        start_tile(dest_ref, 0)

    @pl.when(i + 1 < n)
    def _():
        start_tile(dest_next_ref, 1 - slot)

    def wait(t, carry):
        for k in range(TOP_K):
            copy(dest_ref, slot, t, k).wait()
        return carry

    lax.fori_loop(0, tm, wait, 0)
    gate = gate_ref[...]
    f = gate[:, 0:1] * _natural_rows(buf, tm, (slot, 0))
    for k in range(1, TOP_K):
        f = f + gate[:, k:k + 1] * _natural_rows(buf, tm, (slot, k))
    o_ref[...] = _layer_norm(DEEPNORM_ALPHA * _natural_rows(h_ref, tm) + f, g2_ref[...], b2_ref[...])


def _combine(dest_flat, h, gate, ln2_g, ln2_b, y_rows, tm):
    rows = h.shape[0] // TILE_ROWS
    n_tiles = rows // tm
    return pl.pallas_call(
        functools.partial(_combine_body, tm=tm),
        grid=(n_tiles,),
        in_specs=[pl.BlockSpec((tm * TOP_K,), lambda i: (i,), memory_space=pltpu.SMEM),
                  pl.BlockSpec((tm * TOP_K,), lambda i: (jnp.minimum(i + 1, n_tiles - 1),), memory_space=pltpu.SMEM),
                  pl.BlockSpec((tm * TILE_ROWS, LANES), lambda i: (i, 0)),
                  pl.BlockSpec((tm, LANES), lambda i: (i, 0)),
                  pl.BlockSpec((1, D_MODEL), lambda i: (0, 0)),
                  pl.BlockSpec((1, D_MODEL), lambda i: (0, 0)),
                  pl.BlockSpec(memory_space=pl.ANY)],
        out_specs=pl.BlockSpec((tm, D_MODEL), lambda i: (i, 0)),
        out_shape=jax.ShapeDtypeStruct((rows, D_MODEL), F32),
        scratch_shapes=[pltpu.VMEM((2, TOP_K, tm * TILE_ROWS, LANES), F32), pltpu.SemaphoreType.DMA((2,))],
        compiler_params=_cparams(("arbitrary",)),
        name="moe_combine_ln2",
    )(dest_flat, dest_flat, h, gate, ln2_g, ln2_b, y_rows)


def _t5_bucket_static(dist):
    d = np.maximum(dist, 0)
    d_large = np.maximum(d, REL_MAX_EXACT).astype(np.float32)
    val = (np.log(d_large / np.float32(REL_MAX_EXACT)) / np.float32(math.log(REL_MAX_DISTANCE / REL_MAX_EXACT))
           * np.float32(REL_BUCKETS - REL_MAX_EXACT))
    inner = (d > REL_MAX_EXACT) & (d < REL_MAX_DISTANCE)
    assert not np.any(inner & (np.abs(val - np.round(val)) < 1e-5))
    large = REL_MAX_EXACT + val.astype(np.int32)
    return np.where(d < REL_MAX_EXACT, d, np.minimum(large, REL_BUCKETS - 1)).astype(np.int32)


def _bias_table(rel_bias, dist, head, valid):
    hi = lax.Precision.HIGHEST
    head_rows = jnp.dot(jnp.asarray(np.eye(REL_HEADS, dtype=np.float32)[np.asarray(head)]),
                        rel_bias.astype(F32).T, precision=hi)
    onehot = (jnp.asarray(_t5_bucket_static(np.asarray(dist)))[..., None]
              == jnp.arange(REL_BUCKETS, dtype=I32)).astype(F32)
    tab = jnp.sum(onehot * head_rows[:, None, :], axis=-1)
    return jnp.where(jnp.asarray(np.asarray(valid)), tab, NEG_INF)


_A_SET_HEADS = ((0, 2, 5, 7), (1, 3, 4, 6))


def _tables_a_prompt(rel_bias, sinks):
    qi = np.arange(QBLK)[:, None]
    kj = np.arange(2 * QBLK)[None, :]
    dist = np.tile(QBLK + qi - kj, (4, 1))
    valid = (dist >= 0) & (dist < A_WINDOW)
    bias = jnp.stack([_bias_table(rel_bias, dist, np.repeat(heads, QBLK), valid) for heads in _A_SET_HEADS])
    s = sinks.astype(F32)
    sink = jnp.stack([jnp.repeat(jnp.stack([s[h] for h in heads]), QBLK).reshape(4 * QBLK, 1)
                      for heads in _A_SET_HEADS])
    return bias, sink


def _tables_b_prompt(rel_bias):
    qi = np.arange(QBLK)[:, None]
    kj = np.arange(2 * QBLK)[None, :]
    sub = np.tile(QBLK + qi - kj, (2, 1))
    out = []
    for g, (window, dil) in enumerate(B_GROUPS):
        valid = (sub >= 0) & (sub <= window // dil)
        out.append(jnp.stack([
            _bias_table(rel_bias, sub * dil, np.repeat(A_HEADS + g * B_KV_HEADS + 2 * sp + np.arange(2), QBLK), valid)
            for sp in range(2)]))
    return jnp.stack(out)


def _tables_a_sample(rel_bias, sinks, la):
    n = SAMPLE_ROWS
    rep = A_HEADS // A_KV_HEADS
    qi = np.tile(np.arange(n), rep)[:, None]
    dist_c = la + qi - np.arange(la)[None, :]
    coln = np.arange(LANES)[None, :]
    dist_n = qi - coln
    valid_c = (dist_c >= 0) & (dist_c < A_WINDOW)
    valid_n = (dist_n >= 0) & (dist_n < A_WINDOW) & (coln < n)
    heads = [np.repeat(np.arange(g * rep, (g + 1) * rep), n) for g in range(A_KV_HEADS)]
    bias_c = jnp.stack([_bias_table(rel_bias, dist_c, hd, valid_c) for hd in heads])
    bias_n = jnp.stack([_bias_table(rel_bias, dist_n, hd, valid_n) for hd in heads])
    sink = jnp.repeat(sinks.astype(F32), n).reshape(A_KV_HEADS, rep * n, 1)
    return bias_c, bias_n, sink


def _tables_b_sample(rel_bias, lb):
    n = SAMPLE_ROWS
    n_groups = len(B_GROUPS)
    g = np.repeat(np.arange(n_groups), n)
    qi = np.tile(np.arange(n), n_groups)[:, None]
    window = np.array([w for w, _ in B_GROUPS])[g][:, None]
    dil = np.array([d for _, d in B_GROUPS])[g][:, None]
    coln = np.arange(LANES)[None, :]

    def table(h, dist, extra):
        valid = (dist >= 0) & (dist <= window) & (dist % dil == 0) & extra
        return _bias_table(rel_bias, dist, A_HEADS + g * B_KV_HEADS + h, valid)

    bias_c = jnp.stack([table(h, lb + qi - np.arange(lb)[None, :], True) for h in range(B_KV_HEADS)])
    bias_n = jnp.stack([table(h, qi - coln, coln < n) for h in range(B_KV_HEADS)])
    return bias_c, bias_n


def _mask_mem_sample():
    row_head = np.repeat(np.arange(M_HEADS), SAMPLE_ROWS)[:, None]
    col_head = np.arange(MEM_TOKENS * M_HEADS)[None, :] % M_HEADS
    return jnp.asarray(np.where(row_head == col_head, 0.0, -np.inf).astype(np.float32))


def _swap_halves(w):
    return jnp.concatenate([w[:, HEAD_DIM:], w[:, :HEAD_DIM]], axis=1)


def _slabs_to_rows(slabs):
    return jnp.moveaxis(slabs, 0, 1).reshape(slabs.shape[1], slabs.shape[0] * LANES)


def _heads_to_rows(o):
    return jnp.moveaxis(o, 0, 1).reshape(o.shape[1], o.shape[0] * HEAD_DIM)


def kernel(x_prompt, x_sample, cache_a_k, cache_a_v, cache_b_k, cache_b_v, cache_mem_k, cache_mem_v, mem_prompt,
           rel_bias, sinks_a, w_in, w_mem_kv, w_br_a, w_br_b, w_br_m, w_o, ln1_g, ln1_b, ln2_g, ln2_b, w_router,
           b_router, w_gu, b_gu, w_down, b_down):
    assert w_in.shape[0] == DEPTH == 1
    bsz, seq, _ = x_prompt.shape
    dec_b, dec_n, _ = x_sample.shape
    assert bsz == 1 and dec_n <= SAMPLE_ROWS
    la, lb = cache_a_k.shape[2], cache_b_k.shape[2]
    assert la == A_WINDOW and lb == B_WINDOW_MAX

    w = w_in[0]
    c = np.cumsum((0, 512, 128, 128, 768, 256, 256, 512, 3072))
    w_qa, w_ka, w_va, w_qb, w_kb, w_vb, w_qm, w_g = (w[:, c[i]:c[i + 1]] for i in range(8))
    w_proj = jnp.concatenate([w_qb, w_kb, w_vb, w_qa, w_ka, _swap_halves(w_ka), w_va, _swap_halves(w_va), w_qm],
                             axis=1).astype(BF16)
    wr = jnp.pad(w_router[0].astype(F32), ((0, 0), (0, LANES - N_EXPERTS)))
    wr_hi = wr.astype(BF16)
    merge_w = dict(
        w_g=w_g.astype(BF16), w_a=w_br_a[0].astype(BF16), w_b=w_br_b[0].astype(BF16), w_m=w_br_m[0].astype(BF16),
        w_o=w_o[0].astype(BF16), ln1_g=ln1_g.astype(F32).reshape(1, D_MODEL), ln1_b=ln1_b.astype(F32).reshape(1, D_MODEL),
        wr_hi=wr_hi, wr_lo=(wr - wr_hi.astype(F32)).astype(BF16),
        b_r=jnp.pad(b_router.astype(F32).reshape(1, N_EXPERTS), ((0, 0), (0, LANES - N_EXPERTS)),
                    constant_values=NEG_INF))
    ln2g = ln2_g.astype(F32).reshape(1, D_MODEL)
    ln2b = ln2_b.astype(F32).reshape(1, D_MODEL)

    xp = x_prompt.reshape(seq, D_MODEL)
    qb_p, kvb_p, qa_p, kva_p, qm_p = _project(xp, w_proj, 512)
    mkv_p = _matmul_f32(mem_prompt.reshape(MEM_TOKENS, D_MODEL), w_mem_kv[0].astype(BF16))
    bias_a_p, sink_a_p = _tables_a_prompt(rel_bias, sinks_a[0])
    oa_p = _attn_a_prompt(qa_p, kva_p, bias_a_p, sink_a_p, 512)
    ob_p = _attn_b_prompt(qb_p, kvb_p, _tables_b_prompt(rel_bias))
    om_p = _attn_mem_prompt(qm_p, mkv_p, 512)
    zero_cnt = jnp.zeros((1, LANES), F32)
    n_tok = seq + dec_b * dec_n
    n_rows = n_tok * TOP_K + N_EXPERTS * EXPERT_ROWS
    h_p, route_p, gate_p, cnt_p, x_rows = _merge(zero_cnt, xp, oa_p, ob_p, om_p, merge_w, 512, zero_rows=n_rows)

    xs = jnp.pad(x_sample, ((0, 0), (0, SAMPLE_ROWS - dec_n), (0, 0))).reshape(dec_b * SAMPLE_ROWS, D_MODEL)
    qb_s, kvb_s, qa_s, kva_s, qm_s = _project(xs, w_proj, 512)
    bias_a_c, bias_a_n, sink_a_s = _tables_a_sample(rel_bias, sinks_a[0], la)
    to_hdt = lambda c: jnp.transpose(c[0], (0, 2, 3, 1))
    oa_s = _attn_a_sample(qa_s, kva_s, to_hdt(cache_a_k), to_hdt(cache_a_v), bias_a_c, bias_a_n, sink_a_s, 8)
    ob_s = _attn_b_sample(qb_s, kvb_s, to_hdt(cache_b_k), to_hdt(cache_b_v), *_tables_b_sample(rel_bias, lb), 2)
    om_s = _attn_mem_sample(qm_s, cache_mem_k[0].reshape(dec_b, MEM_TOKENS * M_HEADS, M_HEAD_DIM),
                            cache_mem_v[0].reshape(dec_b, MEM_TOKENS * M_HEADS, M_HEAD_DIM), _mask_mem_sample(), 8)
    unpad = lambda a: a.reshape(dec_b, SAMPLE_ROWS, a.shape[-1])[:, :dec_n].reshape(dec_b * dec_n, a.shape[-1])
    x_s = x_sample.reshape(dec_b * dec_n, D_MODEL)
    h_s, route_s, gate_s, cnt_all = _merge(cnt_p, x_s, unpad(_heads_to_rows(oa_s)), unpad(_heads_to_rows(ob_s)),
                                           unpad(om_s), merge_w, dec_b * dec_n)

    counts = cnt_all[0, :N_EXPERTS].astype(I32)
    padded = (counts + EXPERT_ROWS - 1) // EXPERT_ROWS * EXPERT_ROWS
    pad_end = jnp.cumsum(padded)
    pad_start = pad_end - padded
    n_valid = (pad_end[-1] // EXPERT_ROWS).reshape(1).astype(I32)
    block_row = jnp.arange(n_rows // EXPERT_ROWS, dtype=I32) * EXPERT_ROWS
    block_expert = jnp.minimum(jnp.sum((pad_end[None, :] <= block_row[:, None]).astype(I32), axis=1), N_EXPERTS - 1)
    expert_ids = jnp.arange(N_EXPERTS, dtype=I32)

    def dest(route):
        start = jnp.sum(jnp.where(route[:, :TOP_K, None] == expert_ids, pad_start, 0), axis=-1)
        return (start + route[:, TOP_K:2 * TOP_K]).reshape(-1).astype(I32)

    dest_p, dest_s = dest(route_p), dest(route_s)
    x_rows = _dispatch(dest_p, h_p, x_rows, 512)
    x_rows = _dispatch(dest_s, h_s, x_rows, dec_b * dec_n)
    y_rows = _experts(block_expert, n_valid, x_rows, w_gu[0], b_gu[0], w_down[0], b_down[0])
    y_p = _combine(dest_p, h_p, gate_p, ln2g, ln2b, y_rows, 256)
    y_s = _combine(dest_s, h_s, gate_s, ln2g, ln2b, y_rows, 256)

    kva_rows = _slabs_to_rows(kva_p[:, seq - la:])
    kvb_rows = _slabs_to_rows(kvb_p[:, seq - lb:])
    a_k_p = kva_rows[:, 0:128].reshape(1, 1, la, A_KV_HEADS, HEAD_DIM)
    a_v_p = kva_rows[:, 256:384].reshape(1, 1, la, A_KV_HEADS, HEAD_DIM)
    b_k_p = kvb_rows[:, 0:256].reshape(1, 1, lb, B_KV_HEADS, HEAD_DIM)
    b_v_p = kvb_rows[:, 256:512].reshape(1, 1, lb, B_KV_HEADS, HEAD_DIM)
    m_k_p = mkv_p[:, :M_HEADS * LANES].reshape(1, 1, MEM_TOKENS, M_HEADS, M_HEAD_DIM)
    m_v_p = mkv_p[:, M_HEADS * LANES:].reshape(1, 1, MEM_TOKENS, M_HEADS, M_HEAD_DIM)
    kva_s_rows = unpad(_slabs_to_rows(kva_s))
    kvb_s_rows = unpad(_slabs_to_rows(kvb_s))
    a_k_s = kva_s_rows[:, 0:128].reshape(1, dec_b, dec_n, A_KV_HEADS, HEAD_DIM)
    a_v_s = kva_s_rows[:, 256:384].reshape(1, dec_b, dec_n, A_KV_HEADS, HEAD_DIM)
    b_k_s = kvb_s_rows[:, 0:256].reshape(1, dec_b, dec_n, B_KV_HEADS, HEAD_DIM)
    b_v_s = kvb_s_rows[:, 256:512].reshape(1, dec_b, dec_n, B_KV_HEADS, HEAD_DIM)
    return (y_p.reshape(bsz, seq, D_MODEL), y_s.reshape(dec_b, dec_n, D_MODEL),
            a_k_p, a_v_p, b_k_p, b_v_p, m_k_p, m_v_p, a_k_s, a_v_s, b_k_s, b_v_s)
```

```python
import functools
import math

import numpy as np
import jax
import jax.numpy as jnp
from jax import lax
from jax.experimental import pallas as pl
from jax.experimental.pallas import tpu as pltpu

F32 = jnp.float32
BF16 = jnp.bfloat16
I32 = jnp.int32

D_MODEL = 1024
HEAD_DIM = 64
A_HEADS = 8
A_KV_HEADS = 2
A_WINDOW = 128
B_KV_HEADS = 4
B_GROUPS = ((128, 1), (512, 4), (2048, 16))
B_WINDOW_MAX = 2048
MEM_TOKENS = 256
M_HEADS = 4
M_HEAD_DIM = 128
REL_BUCKETS = 32
REL_MAX_EXACT = REL_BUCKETS // 2
REL_MAX_DISTANCE = B_WINDOW_MAX
REL_HEADS = A_HEADS + B_KV_HEADS * len(B_GROUPS)
N_EXPERTS = 32
TOP_K = 4
D_FF = D_MODEL
SWIGLU_LIMIT = 7.0
SWIGLU_ALPHA = 1.702
LN_EPS = 1e-5
DEPTH = 1
DEEPNORM_ALPHA = (2 * DEPTH) ** 0.25

LANES = 128
TILE_ROWS = 8
QBLK = 128
SAMPLE_ROWS = 8
EXPERT_ROWS = 512
VMEM_LIMIT = 56 * 1024 * 1024
NEG_INF = float("-inf")
B_DIL_MAX = B_GROUPS[-1][1]
ROW_DMA_UNROLL = 8
SAMPLE_UNROLL = 4
B_UNIT_UNROLL = 8

PROJ_SLABS = (6, 4, 4, 4, 4)


def _cparams(sem):
    return pltpu.CompilerParams(dimension_semantics=sem, vmem_limit_bytes=VMEM_LIMIT)


def _dot(a, b):
    return jnp.dot(a, b, preferred_element_type=F32)


def _dot_t(a, b):
    return lax.dot_general(a, b, (((1,), (1,)), ((), ())), preferred_element_type=F32)


def _lo_mask(rows):
    return lax.broadcasted_iota(I32, (rows, LANES), 1) < HEAD_DIM


def _proj_body(x_ref, w_ref, *o_refs):
    x = x_ref[...].astype(BF16)
    col = 0
    for o_ref in o_refs:
        n = o_ref.shape[0] * LANES
        acc = _dot(x, w_ref[:, col:col + n])
        for s in range(o_ref.shape[0]):
            o_ref[s] = acc[:, s * LANES:(s + 1) * LANES]
        col += n


def _project(x, w_bf, tm):
    rows = x.shape[0]
    n_cols = w_bf.shape[1]
    assert rows % tm == 0 and n_cols == sum(PROJ_SLABS) * LANES
    return pl.pallas_call(
        _proj_body,
        grid=(rows // tm,),
        in_specs=[pl.BlockSpec((tm, D_MODEL), lambda i: (i, 0)),
                  pl.BlockSpec((D_MODEL, n_cols), lambda i: (0, 0))],
        out_specs=[pl.BlockSpec((n, tm, LANES), lambda i: (0, i, 0)) for n in PROJ_SLABS],
        out_shape=[jax.ShapeDtypeStruct((n, rows, LANES), F32) for n in PROJ_SLABS],
        compiler_params=_cparams(("parallel",)),
        name="in_proj",
    )(x, w_bf)


def _matmul_body(x_ref, w_ref, o_ref):
    o_ref[...] = _dot(x_ref[...].astype(BF16), w_ref[...])


def _matmul_f32(x, w_bf):
    return pl.pallas_call(
        _matmul_body,
        out_shape=jax.ShapeDtypeStruct((x.shape[0], w_bf.shape[1]), F32),
        compiler_params=_cparams(None),
        name="mem_kv_proj",
    )(x, w_bf)


def _softmax_sink(s, sink):
    m = jnp.maximum(jnp.max(s, axis=-1, keepdims=True), sink)
    p = jnp.exp(s - m)
    denom = jnp.sum(p, axis=-1, keepdims=True) + jnp.exp(sink - m)
    return p / denom


def _softmax_lse(s):
    m = jnp.max(s, axis=-1, keepdims=True)
    p = jnp.exp(s - m)
    l = jnp.sum(p, axis=-1, keepdims=True)
    return p / l, m + jnp.log(l)


def _softmax_parts(parts, sink=None):
    m = functools.reduce(jnp.maximum, [jnp.max(s, axis=-1, keepdims=True) for s in parts])
    if sink is not None:
        m = jnp.maximum(m, sink)
    ps = [jnp.exp(s - m) for s in parts]
    l = functools.reduce(jnp.add, [jnp.sum(p, axis=-1, keepdims=True) for p in ps])
    lse = m + jnp.log(l)
    if sink is not None:
        l = l + jnp.exp(sink - m)
    return [p / l for p in ps], lse


def _attn_a_prompt_body(q_ref, kvc_ref, kvp_ref, bias_ref, sink_ref, o_ref, *, nsub):
    first = pl.program_id(0) == 0
    lo = _lo_mask(QBLK)
    col = lax.broadcasted_iota(I32, (4 * QBLK, 2 * QBLK), 1)
    for j in range(nsub):
        rows = slice(j * QBLK, (j + 1) * QBLK)

        def keys(slab):
            cur = kvc_ref[slab, rows, :]
            prev = kvp_ref[slab] if j == 0 else kvc_ref[slab, (j - 1) * QBLK:j * QBLK, :]
            return jnp.concatenate([prev, cur], axis=0).astype(BF16)

        q = [q_ref[p, rows, :] for p in range(4)]
        zero = jnp.zeros_like(q[0])
        q_sets = (
            jnp.concatenate([jnp.where(lo, q[0], zero), jnp.where(lo, q[1], zero),
                             jnp.where(lo, zero, q[2]), jnp.where(lo, zero, q[3])], axis=0).astype(BF16),
            jnp.concatenate([jnp.where(lo, zero, q[0]), jnp.where(lo, zero, q[1]),
                             jnp.where(lo, q[2], zero), jnp.where(lo, q[3], zero)], axis=0).astype(BF16),
        )
        outs = []
        for st in range(2):
            s = _dot_t(q_sets[st], keys(st)) * (HEAD_DIM ** -0.5) + bias_ref[st]
            if j == 0:
                s = jnp.where(jnp.logical_and(first, col < QBLK), NEG_INF, s)
            p = _softmax_sink(s, sink_ref[st])
            outs.append(_dot(p.astype(BF16), keys(2 + st)))
        o1, o2 = outs
        blk = lambda o, r: o[r * QBLK:(r + 1) * QBLK]
        slabs = (jnp.where(lo, blk(o1, 0), blk(o2, 0)), jnp.where(lo, blk(o1, 1), blk(o2, 1)),
                 jnp.where(lo, blk(o2, 2), blk(o1, 2)), jnp.where(lo, blk(o2, 3), blk(o1, 3)))
        for p in range(4):
            o_ref[rows, p * LANES:(p + 1) * LANES] = slabs[p].astype(o_ref.dtype)


def _attn_a_prompt(qa, kva, bias, sink, tq):
    seq = qa.shape[1]
    nsub = tq // QBLK
    return pl.pallas_call(
        functools.partial(_attn_a_prompt_body, nsub=nsub),
        grid=(seq // tq,),
        in_specs=[pl.BlockSpec((4, tq, LANES), lambda n: (0, n, 0)),
                  pl.BlockSpec((4, tq, LANES), lambda n: (0, n, 0)),
                  pl.BlockSpec((4, QBLK, LANES), lambda n: (0, jnp.maximum(n * nsub - 1, 0), 0)),
                  pl.BlockSpec((2, 4 * QBLK, 2 * QBLK), lambda n: (0, 0, 0)),
                  pl.BlockSpec((2, 4 * QBLK, 1), lambda n: (0, 0, 0))],
        out_specs=pl.BlockSpec((tq, 4 * LANES), lambda n: (n, 0)),
        out_shape=jax.ShapeDtypeStruct((seq, 4 * LANES), BF16),
        compiler_params=_cparams(("parallel",)),
        name="attn_a_prompt",
    )(qa, kva, kva, bias, sink)


def _attn_b_prompt_body(q_ref, kvc_ref, kvp_ref, bias_ref, o_ref, kv_buf, o_buf, l_buf, *, sb):
    first = pl.program_id(0) == 0
    kv_buf[:, :sb, :] = kvp_ref[...]
    kv_buf[:, sb:, :] = kvc_ref[...]
    lo = _lo_mask(QBLK)
    col = lax.broadcasted_iota(I32, (2 * QBLK, 2 * QBLK), 1)
    units = sb // QBLK
    for g, (_, dil) in enumerate(B_GROUPS):
        blocks_per_class = units // dil

        def unit(u, carry, g=g, dil=dil, blocks_per_class=blocks_per_class):
            r = u // blocks_per_class
            m = u % blocks_per_class
            q_start = r + dil * QBLK * m
            k_start = sb - dil * QBLK + q_start
            if dil == 1:
                q_start = pl.multiple_of(q_start, QBLK)
                k_start = pl.multiple_of(k_start, QBLK)
                q_idx = pl.ds(q_start, QBLK)
                k_idx = pl.ds(k_start, 2 * QBLK)
            else:
                q_idx = pl.ds(q_start, QBLK, stride=dil)
                k_idx = pl.ds(k_start, 2 * QBLK, stride=dil)
            for sp in range(2):
                q = q_ref[2 * g + sp, q_idx, :]
                zero = jnp.zeros_like(q)
                q2 = jnp.concatenate([jnp.where(lo, q, zero), jnp.where(lo, zero, q)], axis=0).astype(BF16)
                k = kv_buf[sp, k_idx, :].astype(BF16)
                v = kv_buf[2 + sp, k_idx, :].astype(BF16)
                s = _dot_t(q2, k) * (HEAD_DIM ** -0.5) + bias_ref[g, sp]
                s = jnp.where(jnp.logical_and(jnp.logical_and(first, m == 0), col < QBLK), NEG_INF, s)
                p, lse = _softmax_lse(s)
                o = _dot(p.astype(BF16), v)
                o_buf[g, sp, q_idx, :] = jnp.where(lo, o[:QBLK], o[QBLK:])
                l_buf[g, sp, q_idx, :] = jnp.where(lo, lse[:QBLK], lse[QBLK:])
            return carry

        lax.fori_loop(0, units, unit, 0, unroll=B_UNIT_UNROLL)
    for sp in range(2):
        l = [l_buf[g, sp] for g in range(3)]
        mx = jnp.maximum(jnp.maximum(l[0], l[1]), l[2])
        e = [jnp.exp(x - mx) for x in l]
        den = e[0] + e[1] + e[2]
        acc = (e[0] / den) * o_buf[0, sp] + (e[1] / den) * o_buf[1, sp] + (e[2] / den) * o_buf[2, sp]
        o_ref[:, sp * LANES:(sp + 1) * LANES] = acc.astype(o_ref.dtype)


def _attn_b_prompt(qb, kvb, bias):
    seq = qb.shape[1]
    sb = B_DIL_MAX * QBLK
    assert seq % sb == 0
    return pl.pallas_call(
        functools.partial(_attn_b_prompt_body, sb=sb),
        grid=(seq // sb,),
        in_specs=[pl.BlockSpec((6, sb, LANES), lambda n: (0, n, 0)),
                  pl.BlockSpec((4, sb, LANES), lambda n: (0, n, 0)),
                  pl.BlockSpec((4, sb, LANES), lambda n: (0, jnp.maximum(n - 1, 0), 0)),
                  pl.BlockSpec((3, 2, 2 * QBLK, 2 * QBLK), lambda n: (0, 0, 0, 0))],
        out_specs=pl.BlockSpec((sb, 2 * LANES), lambda n: (n, 0)),
        out_shape=jax.ShapeDtypeStruct((seq, 2 * LANES), BF16),
        scratch_shapes=[pltpu.VMEM((4, 2 * sb, LANES), F32),
                        pltpu.VMEM((3, 2, sb, LANES), F32),
                        pltpu.VMEM((3, 2, sb, LANES), F32)],
        compiler_params=_cparams(("parallel",)),
        name="attn_b_prompt",
    )(qb, kvb, kvb, bias)


def _attn_mem_prompt_body(q_ref, mkv_ref, o_ref):
    for h in range(M_HEADS):
        k = mkv_ref[:, h * LANES:(h + 1) * LANES].astype(BF16)
        v = mkv_ref[:, (M_HEADS + h) * LANES:(M_HEADS + h + 1) * LANES].astype(BF16)
        s = _dot_t(q_ref[h].astype(BF16), k) * (M_HEAD_DIM ** -0.5)
        p, _ = _softmax_lse(s)
        o_ref[:, h * LANES:(h + 1) * LANES] = _dot(p.astype(BF16), v).astype(o_ref.dtype)


def _attn_mem_prompt(qm, mkv, tq):
    seq = qm.shape[1]
    return pl.pallas_call(
        _attn_mem_prompt_body,
        grid=(seq // tq,),
        in_specs=[pl.BlockSpec((M_HEADS, tq, LANES), lambda n: (0, n, 0)),
                  pl.BlockSpec((MEM_TOKENS, 2 * M_HEADS * LANES), lambda n: (0, 0))],
        out_specs=pl.BlockSpec((tq, M_HEADS * LANES), lambda n: (n, 0)),
        out_shape=jax.ShapeDtypeStruct((seq, M_HEADS * LANES), BF16),
        compiler_params=_cparams(("parallel",)),
        name="attn_mem_prompt",
    )(qm, mkv)


def _attn_mem_sample_body(q_ref, mk_ref, mv_ref, mask_ref, o_ref, *, bb):
    def one(b, carry):
        rows = pl.ds(pl.multiple_of(b * SAMPLE_ROWS, SAMPLE_ROWS), SAMPLE_ROWS)
        q = jnp.concatenate([q_ref[h, rows, :] for h in range(M_HEADS)], axis=0).astype(BF16)
        s = _dot_t(q, mk_ref[b].astype(BF16)) * (M_HEAD_DIM ** -0.5) + mask_ref[...]
        p, _ = _softmax_lse(s)
        o = _dot(p.astype(BF16), mv_ref[b].astype(BF16))
        for h in range(M_HEADS):
            o_ref[rows, h * LANES:(h + 1) * LANES] = o[h * SAMPLE_ROWS:(h + 1) * SAMPLE_ROWS].astype(o_ref.dtype)
        return carry

    lax.fori_loop(0, bb, one, 0, unroll=SAMPLE_UNROLL)


def _attn_mem_sample(qm, mk, mv, mask, bb):
    batch, n_rows = mk.shape[0], mk.shape[1]
    return pl.pallas_call(
        functools.partial(_attn_mem_sample_body, bb=bb),
        grid=(batch // bb,),
        in_specs=[pl.BlockSpec((M_HEADS, bb * SAMPLE_ROWS, LANES), lambda n: (0, n, 0)),
                  pl.BlockSpec((bb, n_rows, LANES), lambda n: (n, 0, 0)),
                  pl.BlockSpec((bb, n_rows, LANES), lambda n: (n, 0, 0)),
                  pl.BlockSpec(mask.shape, lambda n: (0, 0))],
        out_specs=pl.BlockSpec((bb * SAMPLE_ROWS, M_HEADS * LANES), lambda n: (n, 0)),
        out_shape=jax.ShapeDtypeStruct((batch * SAMPLE_ROWS, M_HEADS * LANES), BF16),
        compiler_params=_cparams(("parallel",)),
        name="attn_mem_sample",
    )(qm, mk, mv, mask)


def _head_rows(ref, slab, rows, half):
    x = ref[slab, rows, :]
    return x[:, half * HEAD_DIM:(half + 1) * HEAD_DIM]


def _pad_rows(x, n):
    return jnp.concatenate([x, jnp.zeros((n - x.shape[0], x.shape[1]), x.dtype)], axis=0)


def _attn_a_sample_body(q_ref, kvn_ref, ck_ref, cv_ref, bias_c_ref, bias_n_ref, sink_ref, o_ref, *, bb):
    rep = A_HEADS // A_KV_HEADS

    def one(b, carry):
        rows = pl.ds(pl.multiple_of(b * SAMPLE_ROWS, SAMPLE_ROWS), SAMPLE_ROWS)
        for g in range(A_KV_HEADS):
            heads = range(g * rep, (g + 1) * rep)
            q = jnp.concatenate([_head_rows(q_ref, h // 2, rows, h % 2) for h in heads], axis=0).astype(BF16)
            kn = _pad_rows(_head_rows(kvn_ref, 0, rows, g), LANES).astype(BF16)
            vn = _pad_rows(_head_rows(kvn_ref, 2, rows, g), LANES).astype(BF16)
            s_c = _dot(q, ck_ref[b, g].astype(BF16)) * (HEAD_DIM ** -0.5) + bias_c_ref[g]
            s_n = _dot_t(q, kn) * (HEAD_DIM ** -0.5) + bias_n_ref[g]
            (p_c, p_n), _ = _softmax_parts([s_c, s_n], sink_ref[g])
            o = _dot_t(p_c.astype(BF16), cv_ref[b, g].astype(BF16)) + _dot(p_n.astype(BF16), vn)
            for r, h in enumerate(heads):
                o_ref[h, rows, :] = o[r * SAMPLE_ROWS:(r + 1) * SAMPLE_ROWS].astype(o_ref.dtype)
        return carry

    lax.fori_loop(0, bb, one, 0, unroll=SAMPLE_UNROLL)


def _attn_a_sample(qa, kva, ck, cv, bias_c, bias_n, sink, bb):
    batch = ck.shape[0]
    full = lambda a: pl.BlockSpec(a.shape, lambda n: (0,) * a.ndim)
    buf_spec = pl.BlockSpec((bb,) + ck.shape[1:], lambda n: (n, 0, 0, 0))
    return pl.pallas_call(
        functools.partial(_attn_a_sample_body, bb=bb),
        grid=(batch // bb,),
        in_specs=[pl.BlockSpec((4, bb * SAMPLE_ROWS, LANES), lambda n: (0, n, 0)),
                  pl.BlockSpec((4, bb * SAMPLE_ROWS, LANES), lambda n: (0, n, 0)),
                  buf_spec, buf_spec, full(bias_c), full(bias_n), full(sink)],
        out_specs=pl.BlockSpec((A_HEADS, bb * SAMPLE_ROWS, HEAD_DIM), lambda n: (0, n, 0)),
        out_shape=jax.ShapeDtypeStruct((A_HEADS, batch * SAMPLE_ROWS, HEAD_DIM), BF16),
        compiler_params=_cparams(("parallel",)),
        name="attn_a_sample",
    )(qa, kva, ck, cv, bias_c, bias_n, sink)


def _attn_b_sample_body(q_ref, kvn_ref, ck_ref, cv_ref, bias_c_ref, bias_n_ref, o_ref, *, bb):
    n = SAMPLE_ROWS
    scale = HEAD_DIM ** -0.5
    for b in range(bb):
        rows = slice(b * n, (b + 1) * n)
        for h in range(B_KV_HEADS):
            q = jnp.concatenate([_head_rows(q_ref, 2 * g + h // 2, rows, h % 2) for g in range(len(B_GROUPS))],
                                axis=0).astype(BF16)
            kn = _pad_rows(_head_rows(kvn_ref, h // 2, rows, h % 2), LANES).astype(BF16)
            vn = _pad_rows(_head_rows(kvn_ref, 2 + h // 2, rows, h % 2), LANES).astype(BF16)
            s_c = _dot(q, ck_ref[b, h].astype(BF16)) * scale + bias_c_ref[h]
            s_n = _dot_t(q, kn) * scale + bias_n_ref[h]
            (p_c, p_n), lse = _softmax_parts([s_c, s_n])
            o = _dot_t(p_c.astype(BF16), cv_ref[b, h].astype(BF16)) + _dot(p_n.astype(BF16), vn)
            o_g = [o[g * n:(g + 1) * n] for g in range(3)]
            l_g = [lse[g * n:(g + 1) * n] for g in range(3)]
            mx = jnp.maximum(jnp.maximum(l_g[0], l_g[1]), l_g[2])
            e = [jnp.exp(x - mx) for x in l_g]
            den = e[0] + e[1] + e[2]
            acc = (e[0] / den) * o_g[0] + (e[1] / den) * o_g[1] + (e[2] / den) * o_g[2]
            o_ref[h, rows, :] = acc.astype(o_ref.dtype)


def _attn_b_sample(qb, kvb, ck, cv, bias_c, bias_n, bb):
    batch = ck.shape[0]
    full = lambda a: pl.BlockSpec(a.shape, lambda n: (0,) * a.ndim)
    buf_spec = pl.BlockSpec((bb,) + ck.shape[1:], lambda n: (n, 0, 0, 0))
    return pl.pallas_call(
        functools.partial(_attn_b_sample_body, bb=bb),
        grid=(batch // bb,),
        in_specs=[pl.BlockSpec((6, bb * SAMPLE_ROWS, LANES), lambda n: (0, n, 0)),
                  pl.BlockSpec((4, bb * SAMPLE_ROWS, LANES), lambda n: (0, n, 0)),
                  buf_spec, buf_spec, full(bias_c), full(bias_n)],
        out_specs=pl.BlockSpec((B_KV_HEADS, bb * SAMPLE_ROWS, HEAD_DIM), lambda n: (0, n, 0)),
        out_shape=jax.ShapeDtypeStruct((B_KV_HEADS, batch * SAMPLE_ROWS, HEAD_DIM), BF16),
        compiler_params=_cparams(("parallel",)),
        name="attn_b_sample",
    )(qb, kvb, ck, cv, bias_c, bias_n)


def _layer_norm(x, g, b):
    mu = jnp.mean(x, axis=-1, keepdims=True)
    xc = x - mu
    var = jnp.mean(xc * xc, axis=-1, keepdims=True)
    return xc * lax.rsqrt(var + LN_EPS) * g + b


def _split_bf16(x):
    hi = x.astype(BF16)
    return hi, (x - hi.astype(F32)).astype(BF16)


def _merge_body(cnt_in_ref, x_ref, oa_ref, ob_ref, om_ref, wg_ref, wa_ref, wb_ref, wm_ref, wo_ref, g1_ref, b1_ref,
                wr_hi_ref, wr_lo_ref, br_ref, h_ref, route_ref, gate_ref, cnt_ref, *zero_fill, tm, zero_blocks, per_step):
    i = pl.program_id(0)
    if zero_blocks:
        rows_ref, zbuf, zsem = zero_fill
        blk = EXPERT_ROWS * TILE_ROWS

        @pl.when(i == 0)
        def _():
            zbuf[...] = jnp.zeros_like(zbuf)

        def zero_copy(j):
            start = pl.multiple_of((i * per_step + j) * blk, blk)
            return pltpu.make_async_copy(zbuf, rows_ref.at[pl.ds(start, blk), :], zsem)

        def zero_each(fn):
            def body(j, carry):
                @pl.when(i * per_step + j < zero_blocks)
                def _():
                    fn(zero_copy(j))
                return carry
            lax.fori_loop(0, per_step, body, 0)

        zero_each(lambda c: c.start())
    x = x_ref[...]
    gates = jax.nn.sigmoid(_dot(x.astype(BF16), wg_ref[...]))
    u = (gates[:, :D_MODEL] * _dot(oa_ref[...], wa_ref[...])
         + gates[:, D_MODEL:2 * D_MODEL] * _dot(ob_ref[...], wb_ref[...])
         + gates[:, 2 * D_MODEL:] * _dot(om_ref[...], wm_ref[...]))
    mixed = _dot(u.astype(BF16), wo_ref[...])
    h = _layer_norm(DEEPNORM_ALPHA * x + mixed, g1_ref[...], b1_ref[...])
    for c in range(TILE_ROWS):
        h_ref[pl.ds(c, tm, stride=TILE_ROWS), :] = h[:, c * LANES:(c + 1) * LANES]

    h_hi, h_lo = _split_bf16(h)
    logits = _dot(h_hi, wr_hi_ref[...]) + (_dot(h_hi, wr_lo_ref[...]) + _dot(h_lo, wr_hi_ref[...])) + br_ref[...]
    lane = lax.broadcasted_iota(I32, (tm, LANES), 1)
    lane_f = lane.astype(F32)
    work = logits
    vals, sels, idxs = [], [], []
    for _ in range(TOP_K):
        mx = jnp.max(work, axis=-1, keepdims=True)
        idx = jnp.min(jnp.where(work == mx, lane_f, float(LANES)), axis=-1, keepdims=True)
        sel = lane_f == idx
        vals.append(mx)
        idxs.append(idx)
        sels.append(sel)
        work = jnp.where(sel, NEG_INF, work)
    ex = [jnp.exp(v - vals[0]) for v in vals]
    den = ex[0] + ex[1] + ex[2] + ex[3]
    chosen = jnp.logical_or(jnp.logical_or(sels[0], sels[1]), jnp.logical_or(sels[2], sels[3]))
    onehot = jnp.where(chosen, 1.0, 0.0)

    @pl.when(i == 0)
    def _():
        cnt_ref[...] = cnt_in_ref[...]

    carry = cnt_ref[...]
    row = lax.broadcasted_iota(I32, (tm, tm), 0)
    colm = lax.broadcasted_iota(I32, (tm, tm), 1)
    tri = jnp.where(colm < row, 1.0, 0.0).astype(BF16)
    before = _dot(tri, onehot.astype(BF16)) + carry
    route = jnp.zeros((tm, LANES), F32)
    gate = jnp.zeros((tm, LANES), F32)
    for k in range(TOP_K):
        rank = jnp.sum(jnp.where(sels[k], before, 0.0), axis=-1, keepdims=True)
        route = jnp.where(lane == k, idxs[k], route)
        route = jnp.where(lane == TOP_K + k, rank, route)
        gate = jnp.where(lane == k, ex[k] / den, gate)
    route_ref[...] = route.astype(I32)
    gate_ref[...] = gate
    cnt_ref[...] = carry + jnp.sum(onehot, axis=0, keepdims=True)
    if zero_blocks:
        zero_each(lambda c: c.wait())


def _merge(cnt_in, x, oa, ob, om, w, tm, zero_rows=0):
    rows = x.shape[0]
    assert zero_rows % EXPERT_ROWS == 0
    zero_blocks = zero_rows // EXPERT_ROWS
    extra_specs = [pl.BlockSpec(memory_space=pl.ANY)] if zero_blocks else []
    extra_shapes = [jax.ShapeDtypeStruct((zero_rows * TILE_ROWS, LANES), F32)] if zero_blocks else []
    scratch = ([pltpu.VMEM((EXPERT_ROWS * TILE_ROWS, LANES), F32), pltpu.SemaphoreType.DMA(())]
               if zero_blocks else [])
    full = lambda a: pl.BlockSpec(a.shape, lambda i: (0,) * a.ndim)
    row_blk = lambda c: pl.BlockSpec((tm, c), lambda i: (i, 0))
    weights = (w["w_g"], w["w_a"], w["w_b"], w["w_m"], w["w_o"], w["ln1_g"], w["ln1_b"],
               w["wr_hi"], w["wr_lo"], w["b_r"])
    return pl.pallas_call(
        functools.partial(_merge_body, tm=tm, zero_blocks=zero_blocks, per_step=-(-zero_blocks // (rows // tm))),
        grid=(rows // tm,),
        in_specs=[full(cnt_in), row_blk(D_MODEL), row_blk(oa.shape[1]), row_blk(ob.shape[1]), row_blk(om.shape[1])]
        + [full(a) for a in weights],
        out_specs=[pl.BlockSpec((tm * TILE_ROWS, LANES), lambda i: (i, 0)), row_blk(LANES), row_blk(LANES),
                   pl.BlockSpec((1, LANES), lambda i: (0, 0))] + extra_specs,
        out_shape=[jax.ShapeDtypeStruct((rows * TILE_ROWS, LANES), F32), jax.ShapeDtypeStruct((rows, LANES), I32),
                   jax.ShapeDtypeStruct((rows, LANES), F32), jax.ShapeDtypeStruct((1, LANES), F32)] + extra_shapes,
        scratch_shapes=scratch,
        compiler_params=_cparams(("arbitrary",)),
        name="merge_ln_router",
    )(cnt_in, x, oa, ob, om, *weights)


def _token_tile(ref, t):
    return ref.at[pl.ds(pl.multiple_of(t * TILE_ROWS, TILE_ROWS), TILE_ROWS), :]


def _natural_rows(ref, n, lead=()):
    return jnp.concatenate([ref[lead + (pl.ds(c, n, stride=TILE_ROWS), slice(None))] for c in range(TILE_ROWS)], axis=1)


def _dispatch_body(dest_ref, h_ref, xin_ref, xout_ref, sem, *, tm):
    del xin_ref

    def copy(t, k):
        return pltpu.make_async_copy(_token_tile(h_ref, t), _token_tile(xout_ref, dest_ref[t * TOP_K + k]), sem)

    def start(t, carry):
        for k in range(TOP_K):
            copy(t, k).start()
        return carry

    def wait(t, carry):
        for k in range(TOP_K):
            copy(t, k).wait()
        return carry

    lax.fori_loop(0, tm, start, 0, unroll=ROW_DMA_UNROLL)
    lax.fori_loop(0, tm, wait, 0, unroll=ROW_DMA_UNROLL)


def _dispatch(dest_flat, h, x_rows, tm):
    rows = h.shape[0] // TILE_ROWS
    return pl.pallas_call(
        functools.partial(_dispatch_body, tm=tm),
        grid=(rows // tm,),
        in_specs=[pl.BlockSpec((tm * TOP_K,), lambda i: (i,), memory_space=pltpu.SMEM),
                  pl.BlockSpec((tm * TILE_ROWS, LANES), lambda i: (i, 0)),
                  pl.BlockSpec(memory_space=pl.ANY)],
        out_specs=pl.BlockSpec(memory_space=pl.ANY),
        out_shape=jax.ShapeDtypeStruct(x_rows.shape, x_rows.dtype),
        scratch_shapes=[pltpu.SemaphoreType.DMA(())],
        input_output_aliases={2: 0},
        compiler_params=_cparams(("arbitrary",)),
        name="moe_dispatch",
    )(dest_flat, h, x_rows)


def _expert_body(be_ref, nv_ref, slot_ref, next_ref, x_ref, wgu_hbm, bgu_ref, wd_hbm, bd_ref, y_ref,
                 wgu_f32, wd_f32, wgu_bf, wd_bf, sems):
    i = pl.program_id(0)
    valid = i < nv_ref[0]

    def weight_copies(e, slot):
        return (pltpu.make_async_copy(wgu_hbm.at[e], wgu_f32.at[slot], sems.at[0, slot]),
                pltpu.make_async_copy(wd_hbm.at[e], wd_f32.at[slot], sems.at[1, slot]))

    @pl.when(valid)
    def _():
        e = be_ref[i]
        slot = slot_ref[i]
        new_expert = jnp.logical_or(i == 0, e != be_ref[jnp.maximum(i - 1, 0)])

        @pl.when(i == 0)
        def _():
            for c in weight_copies(e, slot):
                c.start()

        @pl.when(new_expert)
        def _():
            for c in weight_copies(e, slot):
                c.wait()

            @pl.when(next_ref[i] >= 0)
            def _():
                for c in weight_copies(next_ref[i], 1 - slot):
                    c.start()

            wgu_bf[...] = wgu_f32[slot].astype(BF16)
            wd_bf[...] = wd_f32[slot].astype(BF16)

        gu = _dot(_natural_rows(x_ref, EXPERT_ROWS).astype(BF16), wgu_bf[...]) + bgu_ref[0]
        g = jnp.minimum(gu[:, :D_FF], SWIGLU_LIMIT)
        u = jnp.clip(gu[:, D_FF:], -SWIGLU_LIMIT, SWIGLU_LIMIT)
        hidden = (u + 1.0) * g * jax.nn.sigmoid(SWIGLU_ALPHA * g)
        y = _dot(hidden.astype(BF16), wd_bf[...]) + bd_ref[0]
        for c in range(TILE_ROWS):
            y_ref[pl.ds(c, EXPERT_ROWS, stride=TILE_ROWS), :] = y[:, c * LANES:(c + 1) * LANES]

    @pl.when(jnp.logical_not(valid))
    def _():
        y_ref[...] = jnp.zeros_like(y_ref)


def _experts(block_expert, n_valid, block_slot, block_next, x_rows, w_gu, b_gu, w_down, b_down):
    blk_rows = EXPERT_ROWS * TILE_ROWS
    n_blocks = x_rows.shape[0] // blk_rows
    last = lambda i, nv: jnp.minimum(i, nv[0] - 1)
    grid_spec = pltpu.PrefetchScalarGridSpec(
        num_scalar_prefetch=4,
        grid=(n_blocks,),
        in_specs=[pl.BlockSpec((blk_rows, LANES), lambda i, be, nv, sl, nx: (last(i, nv), 0)),
                  pl.BlockSpec(memory_space=pl.ANY),
                  pl.BlockSpec((1, 1, 2 * D_FF), lambda i, be, nv, sl, nx: (be[last(i, nv)], 0, 0)),
                  pl.BlockSpec(memory_space=pl.ANY),
                  pl.BlockSpec((1, 1, D_MODEL), lambda i, be, nv, sl, nx: (be[last(i, nv)], 0, 0))],
        out_specs=pl.BlockSpec((blk_rows, LANES), lambda i, be, nv, sl, nx: (i, 0)),
        scratch_shapes=[pltpu.VMEM((2, D_MODEL, 2 * D_FF), F32), pltpu.VMEM((2, D_FF, D_MODEL), F32),
                        pltpu.VMEM((D_MODEL, 2 * D_FF), BF16), pltpu.VMEM((D_FF, D_MODEL), BF16),
                        pltpu.SemaphoreType.DMA((2, 2))],
    )
    return pl.pallas_call(
        _expert_body,
        grid_spec=grid_spec,
        out_shape=jax.ShapeDtypeStruct(x_rows.shape, F32),
        compiler_params=_cparams(("arbitrary",)),
        name="moe_experts",
    )(block_expert, n_valid, block_slot, block_next, x_rows, w_gu, b_gu.reshape(N_EXPERTS, 1, 2 * D_FF), w_down,
      b_down.reshape(N_EXPERTS, 1, D_MODEL))


def _combine_body(dest_ref, dest_next_ref, h_ref, gate_ref, g2_ref, b2_ref, yrows_ref, o_ref, buf, sems, *, tm):
    i = pl.program_id(0)
    n = pl.num_programs(0)
    slot = lax.rem(i, 2)

    def copy(d_ref, s, t, k):
        return pltpu.make_async_copy(_token_tile(yrows_ref, d_ref[t * TOP_K + k]), _token_tile(buf.at[s, k], t),
                                     sems.at[s])

    def start_tile(d_ref, s):
        def body(t, carry):
            for k in range(TOP_K):
                copy(d_ref, s, t, k).start()
            return carry
        lax.fori_loop(0, tm, body, 0, unroll=ROW_DMA_UNROLL)

    @pl.when(i == 0)
    def _():
        start_tile(dest_ref, 0)

    @pl.when(i + 1 < n)
    def _():
        start_tile(dest_next_ref, 1 - slot)

    def wait(t, carry):
        for k in range(TOP_K):
            copy(dest_ref, slot, t, k).wait()
        return carry

    lax.fori_loop(0, tm, wait, 0, unroll=ROW_DMA_UNROLL)
    gate = gate_ref[...]
    f = gate[:, 0:1] * _natural_rows(buf, tm, (slot, 0))
    for k in range(1, TOP_K):
        f = f + gate[:, k:k + 1] * _natural_rows(buf, tm, (slot, k))
    o_ref[...] = _layer_norm(DEEPNORM_ALPHA * _natural_rows(h_ref, tm) + f, g2_ref[...], b2_ref[...])


def _combine(dest_flat, h, gate, ln2_g, ln2_b, y_rows, tm):
    rows = h.shape[0] // TILE_ROWS
    n_tiles = rows // tm
    return pl.pallas_call(
        functools.partial(_combine_body, tm=tm),
        grid=(n_tiles,),
        in_specs=[pl.BlockSpec((tm * TOP_K,), lambda i: (i,), memory_space=pltpu.SMEM),
                  pl.BlockSpec((tm * TOP_K,), lambda i: (jnp.minimum(i + 1, n_tiles - 1),), memory_space=pltpu.SMEM),
                  pl.BlockSpec((tm * TILE_ROWS, LANES), lambda i: (i, 0)),
                  pl.BlockSpec((tm, LANES), lambda i: (i, 0)),
                  pl.BlockSpec((1, D_MODEL), lambda i: (0, 0)),
                  pl.BlockSpec((1, D_MODEL), lambda i: (0, 0)),
                  pl.BlockSpec(memory_space=pl.ANY)],
        out_specs=pl.BlockSpec((tm, D_MODEL), lambda i: (i, 0)),
        out_shape=jax.ShapeDtypeStruct((rows, D_MODEL), F32),
        scratch_shapes=[pltpu.VMEM((2, TOP_K, tm * TILE_ROWS, LANES), F32), pltpu.SemaphoreType.DMA((2,))],
        compiler_params=_cparams(("arbitrary",)),
        name="moe_combine_ln2",
    )(dest_flat, dest_flat, h, gate, ln2_g, ln2_b, y_rows)


def _t5_bucket_static(dist):
    d = np.maximum(dist, 0)
    d_large = np.maximum(d, REL_MAX_EXACT).astype(np.float32)
    val = (np.log(d_large / np.float32(REL_MAX_EXACT)) / np.float32(math.log(REL_MAX_DISTANCE / REL_MAX_EXACT))
           * np.float32(REL_BUCKETS - REL_MAX_EXACT))
    inner = (d > REL_MAX_EXACT) & (d < REL_MAX_DISTANCE)
    assert not np.any(inner & (np.abs(val - np.round(val)) < 1e-5))
    large = REL_MAX_EXACT + val.astype(np.int32)
    return np.where(d < REL_MAX_EXACT, d, np.minimum(large, REL_BUCKETS - 1)).astype(np.int32)


def _bias_table(rel_bias, dist, head, valid):
    hi = lax.Precision.HIGHEST
    head_rows = jnp.dot(jnp.asarray(np.eye(REL_HEADS, dtype=np.float32)[np.asarray(head)]),
                        rel_bias.astype(F32).T, precision=hi)
    onehot = (jnp.asarray(_t5_bucket_static(np.asarray(dist)))[..., None]
              == jnp.arange(REL_BUCKETS, dtype=I32)).astype(F32)
    tab = jnp.sum(onehot * head_rows[:, None, :], axis=-1)
    return jnp.where(jnp.asarray(np.asarray(valid)), tab, NEG_INF)


_A_SET_HEADS = ((0, 2, 5, 7), (1, 3, 4, 6))


def _tables_a_prompt(rel_bias, sinks):
    qi = np.arange(QBLK)[:, None]
    kj = np.arange(2 * QBLK)[None, :]
    dist = np.tile(QBLK + qi - kj, (4, 1))
    valid = (dist >= 0) & (dist < A_WINDOW)
    bias = jnp.stack([_bias_table(rel_bias, dist, np.repeat(heads, QBLK), valid) for heads in _A_SET_HEADS])
    s = sinks.astype(F32)
    sink = jnp.stack([jnp.repeat(jnp.stack([s[h] for h in heads]), QBLK).reshape(4 * QBLK, 1)
                      for heads in _A_SET_HEADS])
    return bias, sink


def _tables_b_prompt(rel_bias):
    qi = np.arange(QBLK)[:, None]
    kj = np.arange(2 * QBLK)[None, :]
    sub = np.tile(QBLK + qi - kj, (2, 1))
    out = []
    for g, (window, dil) in enumerate(B_GROUPS):
        valid = (sub >= 0) & (sub <= window // dil)
        out.append(jnp.stack([
            _bias_table(rel_bias, sub * dil, np.repeat(A_HEADS + g * B_KV_HEADS + 2 * sp + np.arange(2), QBLK), valid)
            for sp in range(2)]))
    return jnp.stack(out)


def _tables_a_sample(rel_bias, sinks, la):
    n = SAMPLE_ROWS
    rep = A_HEADS // A_KV_HEADS
    qi = np.tile(np.arange(n), rep)[:, None]
    dist_c = la + qi - np.arange(la)[None, :]
    coln = np.arange(LANES)[None, :]
    dist_n = qi - coln
    valid_c = (dist_c >= 0) & (dist_c < A_WINDOW)
    valid_n = (dist_n >= 0) & (dist_n < A_WINDOW) & (coln < n)
    heads = [np.repeat(np.arange(g * rep, (g + 1) * rep), n) for g in range(A_KV_HEADS)]
    bias_c = jnp.stack([_bias_table(rel_bias, dist_c, hd, valid_c) for hd in heads])
    bias_n = jnp.stack([_bias_table(rel_bias, dist_n, hd, valid_n) for hd in heads])
    sink = jnp.repeat(sinks.astype(F32), n).reshape(A_KV_HEADS, rep * n, 1)
    return bias_c, bias_n, sink


def _tables_b_sample(rel_bias, lb):
    n = SAMPLE_ROWS
    n_groups = len(B_GROUPS)
    g = np.repeat(np.arange(n_groups), n)
    qi = np.tile(np.arange(n), n_groups)[:, None]
    window = np.array([w for w, _ in B_GROUPS])[g][:, None]
    dil = np.array([d for _, d in B_GROUPS])[g][:, None]
    coln = np.arange(LANES)[None, :]

    def table(h, dist, extra):
        valid = (dist >= 0) & (dist <= window) & (dist % dil == 0) & extra
        return _bias_table(rel_bias, dist, A_HEADS + g * B_KV_HEADS + h, valid)

    bias_c = jnp.stack([table(h, lb + qi - np.arange(lb)[None, :], True) for h in range(B_KV_HEADS)])
    bias_n = jnp.stack([table(h, qi - coln, coln < n) for h in range(B_KV_HEADS)])
    return bias_c, bias_n


def _mask_mem_sample():
    row_head = np.repeat(np.arange(M_HEADS), SAMPLE_ROWS)[:, None]
    col_head = np.arange(MEM_TOKENS * M_HEADS)[None, :] % M_HEADS
    return jnp.asarray(np.where(row_head == col_head, 0.0, -np.inf).astype(np.float32))


def _swap_halves(w):
    return jnp.concatenate([w[:, HEAD_DIM:], w[:, :HEAD_DIM]], axis=1)


def _slabs_to_rows(slabs):
    return jnp.moveaxis(slabs, 0, 1).reshape(slabs.shape[1], slabs.shape[0] * LANES)


def _heads_to_rows(o):
    return jnp.moveaxis(o, 0, 1).reshape(o.shape[1], o.shape[0] * HEAD_DIM)


def kernel(x_prompt, x_sample, cache_a_k, cache_a_v, cache_b_k, cache_b_v, cache_mem_k, cache_mem_v, mem_prompt,
           rel_bias, sinks_a, w_in, w_mem_kv, w_br_a, w_br_b, w_br_m, w_o, ln1_g, ln1_b, ln2_g, ln2_b, w_router,
           b_router, w_gu, b_gu, w_down, b_down):
    assert w_in.shape[0] == DEPTH == 1
    bsz, seq, _ = x_prompt.shape
    dec_b, dec_n, _ = x_sample.shape
    assert bsz == 1 and dec_n <= SAMPLE_ROWS
    la, lb = cache_a_k.shape[2], cache_b_k.shape[2]
    assert la == A_WINDOW and lb == B_WINDOW_MAX

    w = w_in[0]
    c = np.cumsum((0, 512, 128, 128, 768, 256, 256, 512, 3072))
    w_qa, w_ka, w_va, w_qb, w_kb, w_vb, w_qm, w_g = (w[:, c[i]:c[i + 1]] for i in range(8))
    w_proj = jnp.concatenate([w_qb, w_kb, w_vb, w_qa, w_ka, _swap_halves(w_ka), w_va, _swap_halves(w_va), w_qm],
                             axis=1).astype(BF16)
    wr = jnp.pad(w_router[0].astype(F32), ((0, 0), (0, LANES - N_EXPERTS)))
    wr_hi = wr.astype(BF16)
    merge_w = dict(
        w_g=w_g.astype(BF16), w_a=w_br_a[0].astype(BF16), w_b=w_br_b[0].astype(BF16), w_m=w_br_m[0].astype(BF16),
        w_o=w_o[0].astype(BF16), ln1_g=ln1_g.astype(F32).reshape(1, D_MODEL), ln1_b=ln1_b.astype(F32).reshape(1, D_MODEL),
        wr_hi=wr_hi, wr_lo=(wr - wr_hi.astype(F32)).astype(BF16),
        b_r=jnp.pad(b_router.astype(F32).reshape(1, N_EXPERTS), ((0, 0), (0, LANES - N_EXPERTS)),
                    constant_values=NEG_INF))
    ln2g = ln2_g.astype(F32).reshape(1, D_MODEL)
    ln2b = ln2_b.astype(F32).reshape(1, D_MODEL)

    xp = x_prompt.reshape(seq, D_MODEL)
    qb_p, kvb_p, qa_p, kva_p, qm_p = _project(xp, w_proj, 512)
    mkv_p = _matmul_f32(mem_prompt.reshape(MEM_TOKENS, D_MODEL), w_mem_kv[0].astype(BF16))
    bias_a_p, sink_a_p = _tables_a_prompt(rel_bias, sinks_a[0])
    oa_p = _attn_a_prompt(qa_p, kva_p, bias_a_p, sink_a_p, 512)
    ob_p = _attn_b_prompt(qb_p, kvb_p, _tables_b_prompt(rel_bias))
    om_p = _attn_mem_prompt(qm_p, mkv_p, 512)
    zero_cnt = jnp.zeros((1, LANES), F32)
    n_tok = seq + dec_b * dec_n
    n_rows = n_tok * TOP_K + N_EXPERTS * EXPERT_ROWS
    h_p, route_p, gate_p, cnt_p, x_rows = _merge(zero_cnt, xp, oa_p, ob_p, om_p, merge_w, 512, zero_rows=n_rows)

    xs = jnp.pad(x_sample, ((0, 0), (0, SAMPLE_ROWS - dec_n), (0, 0))).reshape(dec_b * SAMPLE_ROWS, D_MODEL)
    qb_s, kvb_s, qa_s, kva_s, qm_s = _project(xs, w_proj, 512)
    bias_a_c, bias_a_n, sink_a_s = _tables_a_sample(rel_bias, sinks_a[0], la)
    to_hdt = lambda c: jnp.transpose(c[0], (0, 2, 3, 1))
    oa_s = _attn_a_sample(qa_s, kva_s, to_hdt(cache_a_k), to_hdt(cache_a_v), bias_a_c, bias_a_n, sink_a_s, 8)
    ob_s = _attn_b_sample(qb_s, kvb_s, to_hdt(cache_b_k), to_hdt(cache_b_v), *_tables_b_sample(rel_bias, lb), 2)
    om_s = _attn_mem_sample(qm_s, cache_mem_k[0].reshape(dec_b, MEM_TOKENS * M_HEADS, M_HEAD_DIM),
                            cache_mem_v[0].reshape(dec_b, MEM_TOKENS * M_HEADS, M_HEAD_DIM), _mask_mem_sample(), 8)
    unpad = lambda a: a.reshape(dec_b, SAMPLE_ROWS, a.shape[-1])[:, :dec_n].reshape(dec_b * dec_n, a.shape[-1])
    x_s = x_sample.reshape(dec_b * dec_n, D_MODEL)
    h_s, route_s, gate_s, cnt_all = _merge(cnt_p, x_s, unpad(_heads_to_rows(oa_s)), unpad(_heads_to_rows(ob_s)),
                                           unpad(om_s), merge_w, dec_b * dec_n)

    counts = cnt_all[0, :N_EXPERTS].astype(I32)
    padded = (counts + EXPERT_ROWS - 1) // EXPERT_ROWS * EXPERT_ROWS
    pad_end = jnp.cumsum(padded)
    pad_start = pad_end - padded
    n_valid = (pad_end[-1] // EXPERT_ROWS).reshape(1).astype(I32)
    block_row = jnp.arange(n_rows // EXPERT_ROWS, dtype=I32) * EXPERT_ROWS
    block_expert = jnp.minimum(jnp.sum((pad_end[None, :] <= block_row[:, None]).astype(I32), axis=1), N_EXPERTS - 1)
    expert_ids = jnp.arange(N_EXPERTS, dtype=I32)

    def dest(route):
        start = jnp.sum(jnp.where(route[:, :TOP_K, None] == expert_ids, pad_start, 0), axis=-1)
        return (start + route[:, TOP_K:2 * TOP_K]).reshape(-1).astype(I32)

    dest_p, dest_s = dest(route_p), dest(route_s)
    x_rows = _dispatch(dest_p, h_p, x_rows, 512)
    x_rows = _dispatch(dest_s, h_s, x_rows, dec_b * dec_n)
    changed = jnp.concatenate([jnp.zeros((1,), I32), (block_expert[1:] != block_expert[:-1]).astype(I32)])
    block_slot = jnp.cumsum(changed) % 2
    later = (expert_ids[None, :] > expert_ids[:, None]) & (counts[None, :] > 0)
    next_nonempty = jnp.min(jnp.where(later, expert_ids[None, :], N_EXPERTS), axis=1)
    next_nonempty = jnp.where(next_nonempty < N_EXPERTS, next_nonempty, -1)
    block_next = jnp.sum(jnp.where(block_expert[:, None] == expert_ids, next_nonempty, 0), axis=1)
    y_rows = _experts(block_expert, n_valid, block_slot.astype(I32), block_next.astype(I32), x_rows, w_gu[0], b_gu[0],
                      w_down[0], b_down[0])
    y_p = _combine(dest_p, h_p, gate_p, ln2g, ln2b, y_rows, 256)
    y_s = _combine(dest_s, h_s, gate_s, ln2g, ln2b, y_rows, 256)

    kva_rows = _slabs_to_rows(kva_p[:, seq - la:])
    kvb_rows = _slabs_to_rows(kvb_p[:, seq - lb:])
    a_k_p = kva_rows[:, 0:128].reshape(1, 1, la, A_KV_HEADS, HEAD_DIM)
    a_v_p = kva_rows[:, 256:384].reshape(1, 1, la, A_KV_HEADS, HEAD_DIM)
    b_k_p = kvb_rows[:, 0:256].reshape(1, 1, lb, B_KV_HEADS, HEAD_DIM)
    b_v_p = kvb_rows[:, 256:512].reshape(1, 1, lb, B_KV_HEADS, HEAD_DIM)
    m_k_p = mkv_p[:, :M_HEADS * LANES].reshape(1, 1, MEM_TOKENS, M_HEADS, M_HEAD_DIM)
    m_v_p = mkv_p[:, M_HEADS * LANES:].reshape(1, 1, MEM_TOKENS, M_HEADS, M_HEAD_DIM)
    kva_s_rows = unpad(_slabs_to_rows(kva_s))
    kvb_s_rows = unpad(_slabs_to_rows(kvb_s))
    a_k_s = kva_s_rows[:, 0:128].reshape(1, dec_b, dec_n, A_KV_HEADS, HEAD_DIM)
    a_v_s = kva_s_rows[:, 256:384].reshape(1, dec_b, dec_n, A_KV_HEADS, HEAD_DIM)
    b_k_s = kvb_s_rows[:, 0:256].reshape(1, dec_b, dec_n, B_KV_HEADS, HEAD_DIM)
    b_v_s = kvb_s_rows[:, 256:512].reshape(1, dec_b, dec_n, B_KV_HEADS, HEAD_DIM)
    return (y_p.reshape(bsz, seq, D_MODEL), y_s.reshape(dec_b, dec_n, D_MODEL),
            a_k_p, a_v_p, b_k_p, b_v_p, m_k_p, m_v_p, a_k_s, a_v_s, b_k_s, b_v_s)
```

```python
import functools
import math

import numpy as np
import jax
import jax.numpy as jnp
from jax import lax
from jax.experimental import pallas as pl
from jax.experimental.pallas import tpu as pltpu

F32 = jnp.float32
BF16 = jnp.bfloat16
I32 = jnp.int32

D_MODEL = 1024
HEAD_DIM = 64
A_HEADS = 8
A_KV_HEADS = 2
A_WINDOW = 128
B_KV_HEADS = 4
B_GROUPS = ((128, 1), (512, 4), (2048, 16))
B_WINDOW_MAX = 2048
MEM_TOKENS = 256
M_HEADS = 4
M_HEAD_DIM = 128
REL_BUCKETS = 32
REL_MAX_EXACT = REL_BUCKETS // 2
REL_MAX_DISTANCE = B_WINDOW_MAX
REL_HEADS = A_HEADS + B_KV_HEADS * len(B_GROUPS)
N_EXPERTS = 32
TOP_K = 4
D_FF = D_MODEL
SWIGLU_LIMIT = 7.0
SWIGLU_ALPHA = 1.702
LN_EPS = 1e-5
DEPTH = 1
DEEPNORM_ALPHA = (2 * DEPTH) ** 0.25

LANES = 128
TILE_ROWS = 8
QBLK = 128
SAMPLE_ROWS = 8
EXPERT_ROWS = 512
VMEM_LIMIT = 56 * 1024 * 1024
NEG_INF = float("-inf")
B_DIL_MAX = B_GROUPS[-1][1]
ROW_DMA_UNROLL = 8
SAMPLE_UNROLL = 4
B_UNIT_UNROLL = 8

PROJ_SLABS = (6, 4, 4, 4, 4)


def _cparams(sem):
    return pltpu.CompilerParams(dimension_semantics=sem, vmem_limit_bytes=VMEM_LIMIT)


def _dot(a, b):
    return jnp.dot(a, b, preferred_element_type=F32)


def _dot_t(a, b):
    return lax.dot_general(a, b, (((1,), (1,)), ((), ())), preferred_element_type=F32)


def _lo_mask(rows):
    return lax.broadcasted_iota(I32, (rows, LANES), 1) < HEAD_DIM


def _proj_body(x_ref, w_ref, *o_refs):
    x = x_ref[...].astype(BF16)
    col = 0
    for o_ref in o_refs:
        n = o_ref.shape[0] * LANES
        acc = _dot(x, w_ref[:, col:col + n])
        for s in range(o_ref.shape[0]):
            o_ref[s] = acc[:, s * LANES:(s + 1) * LANES]
        col += n


def _project(x, w_bf, tm):
    rows = x.shape[0]
    n_cols = w_bf.shape[1]
    assert rows % tm == 0 and n_cols == sum(PROJ_SLABS) * LANES
    return pl.pallas_call(
        _proj_body,
        grid=(rows // tm,),
        in_specs=[pl.BlockSpec((tm, D_MODEL), lambda i: (i, 0)),
                  pl.BlockSpec((D_MODEL, n_cols), lambda i: (0, 0))],
        out_specs=[pl.BlockSpec((n, tm, LANES), lambda i: (0, i, 0)) for n in PROJ_SLABS],
        out_shape=[jax.ShapeDtypeStruct((n, rows, LANES), F32) for n in PROJ_SLABS],
        compiler_params=_cparams(("parallel",)),
        name="in_proj",
    )(x, w_bf)


def _matmul_body(x_ref, w_ref, o_ref):
    o_ref[...] = _dot(x_ref[...].astype(BF16), w_ref[...])


def _matmul_f32(x, w_bf):
    return pl.pallas_call(
        _matmul_body,
        out_shape=jax.ShapeDtypeStruct((x.shape[0], w_bf.shape[1]), F32),
        compiler_params=_cparams(None),
        name="mem_kv_proj",
    )(x, w_bf)


def _softmax_sink(s, sink):
    m = jnp.maximum(jnp.max(s, axis=-1, keepdims=True), sink)
    p = jnp.exp(s - m)
    denom = jnp.sum(p, axis=-1, keepdims=True) + jnp.exp(sink - m)
    return p / denom


def _softmax_lse(s):
    m = jnp.max(s, axis=-1, keepdims=True)
    p = jnp.exp(s - m)
    l = jnp.sum(p, axis=-1, keepdims=True)
    return p / l, m + jnp.log(l)


def _softmax_parts(parts, sink=None):
    m = functools.reduce(jnp.maximum, [jnp.max(s, axis=-1, keepdims=True) for s in parts])
    if sink is not None:
        m = jnp.maximum(m, sink)
    ps = [jnp.exp(s - m) for s in parts]
    l = functools.reduce(jnp.add, [jnp.sum(p, axis=-1, keepdims=True) for p in ps])
    lse = m + jnp.log(l)
    if sink is not None:
        l = l + jnp.exp(sink - m)
    return [p / l for p in ps], lse


def _attn_a_prompt_body(q_ref, kvc_ref, kvp_ref, bias_ref, sink_ref, o_ref, *, nsub):
    first = pl.program_id(0) == 0
    lo = _lo_mask(QBLK)
    col = lax.broadcasted_iota(I32, (4 * QBLK, 2 * QBLK), 1)
    for j in range(nsub):
        rows = slice(j * QBLK, (j + 1) * QBLK)

        def keys(slab):
            cur = kvc_ref[slab, rows, :]
            prev = kvp_ref[slab] if j == 0 else kvc_ref[slab, (j - 1) * QBLK:j * QBLK, :]
            return jnp.concatenate([prev, cur], axis=0).astype(BF16)

        q = [q_ref[p, rows, :] for p in range(4)]
        zero = jnp.zeros_like(q[0])
        q_sets = (
            jnp.concatenate([jnp.where(lo, q[0], zero), jnp.where(lo, q[1], zero),
                             jnp.where(lo, zero, q[2]), jnp.where(lo, zero, q[3])], axis=0).astype(BF16),
            jnp.concatenate([jnp.where(lo, zero, q[0]), jnp.where(lo, zero, q[1]),
                             jnp.where(lo, q[2], zero), jnp.where(lo, q[3], zero)], axis=0).astype(BF16),
        )
        outs = []
        for st in range(2):
            s = _dot_t(q_sets[st], keys(st)) * (HEAD_DIM ** -0.5) + bias_ref[st]
            if j == 0:
                s = jnp.where(jnp.logical_and(first, col < QBLK), NEG_INF, s)
            p = _softmax_sink(s, sink_ref[st])
            outs.append(_dot(p.astype(BF16), keys(2 + st)))
        o1, o2 = outs
        blk = lambda o, r: o[r * QBLK:(r + 1) * QBLK]
        slabs = (jnp.where(lo, blk(o1, 0), blk(o2, 0)), jnp.where(lo, blk(o1, 1), blk(o2, 1)),
                 jnp.where(lo, blk(o2, 2), blk(o1, 2)), jnp.where(lo, blk(o2, 3), blk(o1, 3)))
        for p in range(4):
            o_ref[rows, p * LANES:(p + 1) * LANES] = slabs[p].astype(o_ref.dtype)


def _attn_a_prompt(qa, kva, bias, sink, tq):
    seq = qa.shape[1]
    nsub = tq // QBLK
    return pl.pallas_call(
        functools.partial(_attn_a_prompt_body, nsub=nsub),
        grid=(seq // tq,),
        in_specs=[pl.BlockSpec((4, tq, LANES), lambda n: (0, n, 0)),
                  pl.BlockSpec((4, tq, LANES), lambda n: (0, n, 0)),
                  pl.BlockSpec((4, QBLK, LANES), lambda n: (0, jnp.maximum(n * nsub - 1, 0), 0)),
                  pl.BlockSpec((2, 4 * QBLK, 2 * QBLK), lambda n: (0, 0, 0)),
                  pl.BlockSpec((2, 4 * QBLK, 1), lambda n: (0, 0, 0))],
        out_specs=pl.BlockSpec((tq, 4 * LANES), lambda n: (n, 0)),
        out_shape=jax.ShapeDtypeStruct((seq, 4 * LANES), BF16),
        compiler_params=_cparams(("parallel",)),
        name="attn_a_prompt",
    )(qa, kva, kva, bias, sink)


def _attn_b_prompt_body(q_ref, kvc_ref, kvp_ref, bias_ref, o_ref, kv_buf, o_buf, l_buf, *, sb):
    first = pl.program_id(0) == 0
    kv_buf[:, :sb, :] = kvp_ref[...]
    kv_buf[:, sb:, :] = kvc_ref[...]
    lo = _lo_mask(QBLK)
    col = lax.broadcasted_iota(I32, (2 * QBLK, 2 * QBLK), 1)
    units = sb // QBLK
    for g, (_, dil) in enumerate(B_GROUPS):
        blocks_per_class = units // dil

        def unit(u, carry, g=g, dil=dil, blocks_per_class=blocks_per_class):
            r = u // blocks_per_class
            m = u % blocks_per_class
            q_start = r + dil * QBLK * m
            k_start = sb - dil * QBLK + q_start
            if dil == 1:
                q_start = pl.multiple_of(q_start, QBLK)
                k_start = pl.multiple_of(k_start, QBLK)
                q_idx = pl.ds(q_start, QBLK)
                k_idx = pl.ds(k_start, 2 * QBLK)
            else:
                q_idx = pl.ds(q_start, QBLK, stride=dil)
                k_idx = pl.ds(k_start, 2 * QBLK, stride=dil)
            for sp in range(2):
                q = q_ref[2 * g + sp, q_idx, :]
                zero = jnp.zeros_like(q)
                q2 = jnp.concatenate([jnp.where(lo, q, zero), jnp.where(lo, zero, q)], axis=0).astype(BF16)
                k = kv_buf[sp, k_idx, :].astype(BF16)
                v = kv_buf[2 + sp, k_idx, :].astype(BF16)
                s = _dot_t(q2, k) * (HEAD_DIM ** -0.5) + bias_ref[g, sp]
                s = jnp.where(jnp.logical_and(jnp.logical_and(first, m == 0), col < QBLK), NEG_INF, s)
                p, lse = _softmax_lse(s)
                o = _dot(p.astype(BF16), v)
                o_buf[g, sp, q_idx, :] = jnp.where(lo, o[:QBLK], o[QBLK:])
                l_buf[g, sp, q_idx, :] = jnp.where(lo, lse[:QBLK], lse[QBLK:])
            return carry

        lax.fori_loop(0, units, unit, 0, unroll=B_UNIT_UNROLL)
    for sp in range(2):
        l = [l_buf[g, sp] for g in range(3)]
        mx = jnp.maximum(jnp.maximum(l[0], l[1]), l[2])
        e = [jnp.exp(x - mx) for x in l]
        den = e[0] + e[1] + e[2]
        acc = (e[0] / den) * o_buf[0, sp] + (e[1] / den) * o_buf[1, sp] + (e[2] / den) * o_buf[2, sp]
        o_ref[:, sp * LANES:(sp + 1) * LANES] = acc.astype(o_ref.dtype)


def _attn_b_prompt(qb, kvb, bias):
    seq = qb.shape[1]
    sb = B_DIL_MAX * QBLK
    assert seq % sb == 0
    return pl.pallas_call(
        functools.partial(_attn_b_prompt_body, sb=sb),
        grid=(seq // sb,),
        in_specs=[pl.BlockSpec((6, sb, LANES), lambda n: (0, n, 0)),
                  pl.BlockSpec((4, sb, LANES), lambda n: (0, n, 0)),
                  pl.BlockSpec((4, sb, LANES), lambda n: (0, jnp.maximum(n - 1, 0), 0)),
                  pl.BlockSpec((3, 2, 2 * QBLK, 2 * QBLK), lambda n: (0, 0, 0, 0))],
        out_specs=pl.BlockSpec((sb, 2 * LANES), lambda n: (n, 0)),
        out_shape=jax.ShapeDtypeStruct((seq, 2 * LANES), BF16),
        scratch_shapes=[pltpu.VMEM((4, 2 * sb, LANES), F32),
                        pltpu.VMEM((3, 2, sb, LANES), F32),
                        pltpu.VMEM((3, 2, sb, LANES), F32)],
        compiler_params=_cparams(("parallel",)),
        name="attn_b_prompt",
    )(qb, kvb, kvb, bias)


def _attn_mem_prompt_body(q_ref, mkv_ref, o_ref):
    for h in range(M_HEADS):
        k = mkv_ref[:, h * LANES:(h + 1) * LANES].astype(BF16)
        v = mkv_ref[:, (M_HEADS + h) * LANES:(M_HEADS + h + 1) * LANES].astype(BF16)
        s = _dot_t(q_ref[h].astype(BF16), k) * (M_HEAD_DIM ** -0.5)
        p, _ = _softmax_lse(s)
        o_ref[:, h * LANES:(h + 1) * LANES] = _dot(p.astype(BF16), v).astype(o_ref.dtype)


def _attn_mem_prompt(qm, mkv, tq):
    seq = qm.shape[1]
    return pl.pallas_call(
        _attn_mem_prompt_body,
        grid=(seq // tq,),
        in_specs=[pl.BlockSpec((M_HEADS, tq, LANES), lambda n: (0, n, 0)),
                  pl.BlockSpec((MEM_TOKENS, 2 * M_HEADS * LANES), lambda n: (0, 0))],
        out_specs=pl.BlockSpec((tq, M_HEADS * LANES), lambda n: (n, 0)),
        out_shape=jax.ShapeDtypeStruct((seq, M_HEADS * LANES), BF16),
        compiler_params=_cparams(("parallel",)),
        name="attn_mem_prompt",
    )(qm, mkv)


def _attn_mem_sample_body(q_ref, mk_ref, mv_ref, mask_ref, o_ref, *, bb):
    def one(b, carry):
        rows = pl.ds(pl.multiple_of(b * SAMPLE_ROWS, SAMPLE_ROWS), SAMPLE_ROWS)
        q = jnp.concatenate([q_ref[h, rows, :] for h in range(M_HEADS)], axis=0).astype(BF16)
        s = _dot_t(q, mk_ref[b].astype(BF16)) * (M_HEAD_DIM ** -0.5) + mask_ref[...]
        p, _ = _softmax_lse(s)
        o = _dot(p.astype(BF16), mv_ref[b].astype(BF16))
        for h in range(M_HEADS):
            o_ref[rows, h * LANES:(h + 1) * LANES] = o[h * SAMPLE_ROWS:(h + 1) * SAMPLE_ROWS].astype(o_ref.dtype)
        return carry

    lax.fori_loop(0, bb, one, 0, unroll=SAMPLE_UNROLL)


def _attn_mem_sample(qm, mk, mv, mask, bb):
    batch, n_rows = mk.shape[0], mk.shape[1]
    return pl.pallas_call(
        functools.partial(_attn_mem_sample_body, bb=bb),
        grid=(batch // bb,),
        in_specs=[pl.BlockSpec((M_HEADS, bb * SAMPLE_ROWS, LANES), lambda n: (0, n, 0)),
                  pl.BlockSpec((bb, n_rows, LANES), lambda n: (n, 0, 0)),
                  pl.BlockSpec((bb, n_rows, LANES), lambda n: (n, 0, 0)),
                  pl.BlockSpec(mask.shape, lambda n: (0, 0))],
        out_specs=pl.BlockSpec((bb * SAMPLE_ROWS, M_HEADS * LANES), lambda n: (n, 0)),
        out_shape=jax.ShapeDtypeStruct((batch * SAMPLE_ROWS, M_HEADS * LANES), BF16),
        compiler_params=_cparams(("parallel",)),
        name="attn_mem_sample",
    )(qm, mk, mv, mask)


def _head_rows(ref, slab, rows, half):
    x = ref[slab, rows, :]
    return x[:, half * HEAD_DIM:(half + 1) * HEAD_DIM]


def _pad_rows(x, n):
    return jnp.concatenate([x, jnp.zeros((n - x.shape[0], x.shape[1]), x.dtype)], axis=0)


def _attn_a_sample_body(q_ref, kvn_ref, ck_ref, cv_ref, bias_c_ref, bias_n_ref, sink_ref, o_ref, *, bb):
    rep = A_HEADS // A_KV_HEADS

    def one(b, carry):
        rows = pl.ds(pl.multiple_of(b * SAMPLE_ROWS, SAMPLE_ROWS), SAMPLE_ROWS)
        for g in range(A_KV_HEADS):
            heads = range(g * rep, (g + 1) * rep)
            q = jnp.concatenate([_head_rows(q_ref, h // 2, rows, h % 2) for h in heads], axis=0).astype(BF16)
            kn = _pad_rows(_head_rows(kvn_ref, 0, rows, g), LANES).astype(BF16)
            vn = _pad_rows(_head_rows(kvn_ref, 2, rows, g), LANES).astype(BF16)
            s_c = _dot(q, ck_ref[b, g].astype(BF16)) * (HEAD_DIM ** -0.5) + bias_c_ref[g]
            s_n = _dot_t(q, kn) * (HEAD_DIM ** -0.5) + bias_n_ref[g]
            (p_c, p_n), _ = _softmax_parts([s_c, s_n], sink_ref[g])
            o = _dot_t(p_c.astype(BF16), cv_ref[b, g].astype(BF16)) + _dot(p_n.astype(BF16), vn)
            for r, h in enumerate(heads):
                o_ref[h, rows, :] = o[r * SAMPLE_ROWS:(r + 1) * SAMPLE_ROWS].astype(o_ref.dtype)
        return carry

    lax.fori_loop(0, bb, one, 0, unroll=SAMPLE_UNROLL)


def _attn_a_sample(qa, kva, ck, cv, bias_c, bias_n, sink, bb):
    batch = ck.shape[0]
    full = lambda a: pl.BlockSpec(a.shape, lambda n: (0,) * a.ndim)
    buf_spec = pl.BlockSpec((bb,) + ck.shape[1:], lambda n: (n, 0, 0, 0))
    return pl.pallas_call(
        functools.partial(_attn_a_sample_body, bb=bb),
        grid=(batch // bb,),
        in_specs=[pl.BlockSpec((4, bb * SAMPLE_ROWS, LANES), lambda n: (0, n, 0)),
                  pl.BlockSpec((4, bb * SAMPLE_ROWS, LANES), lambda n: (0, n, 0)),
                  buf_spec, buf_spec, full(bias_c), full(bias_n), full(sink)],
        out_specs=pl.BlockSpec((A_HEADS, bb * SAMPLE_ROWS, HEAD_DIM), lambda n: (0, n, 0)),
        out_shape=jax.ShapeDtypeStruct((A_HEADS, batch * SAMPLE_ROWS, HEAD_DIM), BF16),
        compiler_params=_cparams(("parallel",)),
        name="attn_a_sample",
    )(qa, kva, ck, cv, bias_c, bias_n, sink)


def _attn_b_sample_body(q_ref, kvn_ref, ck_ref, cv_ref, bias_c_ref, bias_n_ref, o_ref, *, bb):
    n = SAMPLE_ROWS
    n_groups = len(B_GROUPS)
    lo = _lo_mask(n)
    scale = HEAD_DIM ** -0.5
    zero = jnp.zeros((n, LANES), F32)
    for b in range(bb):
        rows = slice(b * n, (b + 1) * n)
        q_rows = []
        for h in range(B_KV_HEADS):
            for g in range(n_groups):
                x = q_ref[2 * g + h // 2, rows, :]
                x = jnp.where(lo, x, zero) if h % 2 == 0 else jnp.where(lo, zero, x)
                q_rows.append(jnp.concatenate([x, zero] if h // 2 == 0 else [zero, x], axis=1))
        q = jnp.concatenate(q_rows, axis=0).astype(BF16)
        kn = _pad_rows(jnp.concatenate([kvn_ref[0, rows, :], kvn_ref[1, rows, :]], axis=1), LANES).astype(BF16)
        vn = _pad_rows(jnp.concatenate([kvn_ref[2, rows, :], kvn_ref[3, rows, :]], axis=1), LANES).astype(BF16)
        kt = ck_ref[b].reshape(B_KV_HEADS * HEAD_DIM, ck_ref.shape[-1]).astype(BF16)
        vt = cv_ref[b].reshape(B_KV_HEADS * HEAD_DIM, cv_ref.shape[-1]).astype(BF16)
        s_c = _dot(q, kt) * scale + bias_c_ref[...]
        s_n = _dot_t(q, kn) * scale + bias_n_ref[...]
        (p_c, p_n), lse = _softmax_parts([s_c, s_n])
        o = _dot_t(p_c.astype(BF16), vt) + _dot(p_n.astype(BF16), vn)
        merged = []
        for h in range(B_KV_HEADS):
            cols = slice((h // 2) * LANES, (h // 2 + 1) * LANES)
            blk = lambda g: slice((n_groups * h + g) * n, (n_groups * h + g + 1) * n)
            l_g = [lse[blk(g)] for g in range(n_groups)]
            mx = jnp.maximum(jnp.maximum(l_g[0], l_g[1]), l_g[2])
            e = [jnp.exp(x - mx) for x in l_g]
            den = e[0] + e[1] + e[2]
            merged.append((e[0] / den) * o[blk(0), cols] + (e[1] / den) * o[blk(1), cols]
                          + (e[2] / den) * o[blk(2), cols])
        for cg in range(2):
            o_ref[rows, cg * LANES:(cg + 1) * LANES] = jnp.where(lo, merged[2 * cg], merged[2 * cg + 1]).astype(o_ref.dtype)


def _attn_b_sample(qb, kvb, ck, cv, bias_c, bias_n, bb):
    batch = ck.shape[0]
    full = lambda a: pl.BlockSpec(a.shape, lambda n: (0,) * a.ndim)
    buf_spec = pl.BlockSpec((bb,) + ck.shape[1:], lambda n: (n, 0, 0, 0))
    return pl.pallas_call(
        functools.partial(_attn_b_sample_body, bb=bb),
        grid=(batch // bb,),
        in_specs=[pl.BlockSpec((6, bb * SAMPLE_ROWS, LANES), lambda n: (0, n, 0)),
                  pl.BlockSpec((4, bb * SAMPLE_ROWS, LANES), lambda n: (0, n, 0)),
                  buf_spec, buf_spec, full(bias_c), full(bias_n)],
        out_specs=pl.BlockSpec((bb * SAMPLE_ROWS, B_KV_HEADS * HEAD_DIM), lambda n: (n, 0)),
        out_shape=jax.ShapeDtypeStruct((batch * SAMPLE_ROWS, B_KV_HEADS * HEAD_DIM), BF16),
        compiler_params=_cparams(("parallel",)),
        name="attn_b_sample",
    )(qb, kvb, ck, cv, bias_c, bias_n)


def _layer_norm(x, g, b):
    mu = jnp.mean(x, axis=-1, keepdims=True)
    xc = x - mu
    var = jnp.mean(xc * xc, axis=-1, keepdims=True)
    return xc * lax.rsqrt(var + LN_EPS) * g + b


def _split_bf16(x):
    hi = x.astype(BF16)
    return hi, (x - hi.astype(F32)).astype(BF16)


def _merge_body(cnt_in_ref, x_ref, oa_ref, ob_ref, om_ref, wg_ref, wa_ref, wb_ref, wm_ref, wo_ref, g1_ref, b1_ref,
                wr_hi_ref, wr_lo_ref, br_ref, h_ref, route_ref, gate_ref, cnt_ref, *zero_fill, tm, zero_blocks, per_step):
    i = pl.program_id(0)
    if zero_blocks:
        rows_ref, zbuf, zsem = zero_fill
        blk = EXPERT_ROWS * TILE_ROWS

        @pl.when(i == 0)
        def _():
            zbuf[...] = jnp.zeros_like(zbuf)

        def zero_copy(j):
            start = pl.multiple_of((i * per_step + j) * blk, blk)
            return pltpu.make_async_copy(zbuf, rows_ref.at[pl.ds(start, blk), :], zsem)

        def zero_each(fn):
            def body(j, carry):
                @pl.when(i * per_step + j < zero_blocks)
                def _():
                    fn(zero_copy(j))
                return carry
            lax.fori_loop(0, per_step, body, 0)

        zero_each(lambda c: c.start())
    x = x_ref[...]
    gates = jax.nn.sigmoid(_dot(x.astype(BF16), wg_ref[...]))
    u = (gates[:, :D_MODEL] * _dot(oa_ref[...], wa_ref[...])
         + gates[:, D_MODEL:2 * D_MODEL] * _dot(ob_ref[...], wb_ref[...])
         + gates[:, 2 * D_MODEL:] * _dot(om_ref[...], wm_ref[...]))
    mixed = _dot(u.astype(BF16), wo_ref[...])
    h = _layer_norm(DEEPNORM_ALPHA * x + mixed, g1_ref[...], b1_ref[...])
    for c in range(TILE_ROWS):
        h_ref[pl.ds(c, tm, stride=TILE_ROWS), :] = h[:, c * LANES:(c + 1) * LANES]

    h_hi, h_lo = _split_bf16(h)
    logits = _dot(h_hi, wr_hi_ref[...]) + (_dot(h_hi, wr_lo_ref[...]) + _dot(h_lo, wr_hi_ref[...])) + br_ref[...]
    lane = lax.broadcasted_iota(I32, (tm, LANES), 1)
    lane_f = lane.astype(F32)
    work = logits
    vals, sels, idxs = [], [], []
    for _ in range(TOP_K):
        mx = jnp.max(work, axis=-1, keepdims=True)
        idx = jnp.min(jnp.where(work == mx, lane_f, float(LANES)), axis=-1, keepdims=True)
        sel = lane_f == idx
        vals.append(mx)
        idxs.append(idx)
        sels.append(sel)
        work = jnp.where(sel, NEG_INF, work)
    ex = [jnp.exp(v - vals[0]) for v in vals]
    den = ex[0] + ex[1] + ex[2] + ex[3]
    chosen = jnp.logical_or(jnp.logical_or(sels[0], sels[1]), jnp.logical_or(sels[2], sels[3]))
    onehot = jnp.where(chosen, 1.0, 0.0)

    @pl.when(i == 0)
    def _():
        cnt_ref[...] = cnt_in_ref[...]

    carry = cnt_ref[...]
    row = lax.broadcasted_iota(I32, (tm, tm), 0)
    colm = lax.broadcasted_iota(I32, (tm, tm), 1)
    tri = jnp.where(colm < row, 1.0, 0.0).astype(BF16)
    before = _dot(tri, onehot.astype(BF16)) + carry
    route = jnp.zeros((tm, LANES), F32)
    gate = jnp.zeros((tm, LANES), F32)
    for k in range(TOP_K):
        rank = jnp.sum(jnp.where(sels[k], before, 0.0), axis=-1, keepdims=True)
        route = jnp.where(lane == k, idxs[k], route)
        route = jnp.where(lane == TOP_K + k, rank, route)
        gate = jnp.where(lane == k, ex[k] / den, gate)
    route_ref[...] = route.astype(I32)
    gate_ref[...] = gate
    cnt_ref[...] = carry + jnp.sum(onehot, axis=0, keepdims=True)
    if zero_blocks:
        zero_each(lambda c: c.wait())


def _merge(cnt_in, x, oa, ob, om, w, tm, zero_rows=0):
    rows = x.shape[0]
    assert zero_rows % EXPERT_ROWS == 0
    zero_blocks = zero_rows // EXPERT_ROWS
    extra_specs = [pl.BlockSpec(memory_space=pl.ANY)] if zero_blocks else []
    extra_shapes = [jax.ShapeDtypeStruct((zero_rows * TILE_ROWS, LANES), F32)] if zero_blocks else []
    scratch = ([pltpu.VMEM((EXPERT_ROWS * TILE_ROWS, LANES), F32), pltpu.SemaphoreType.DMA(())]
               if zero_blocks else [])
    full = lambda a: pl.BlockSpec(a.shape, lambda i: (0,) * a.ndim)
    row_blk = lambda c: pl.BlockSpec((tm, c), lambda i: (i, 0))
    weights = (w["w_g"], w["w_a"], w["w_b"], w["w_m"], w["w_o"], w["ln1_g"], w["ln1_b"],
               w["wr_hi"], w["wr_lo"], w["b_r"])
    return pl.pallas_call(
        functools.partial(_merge_body, tm=tm, zero_blocks=zero_blocks, per_step=-(-zero_blocks // (rows // tm))),
        grid=(rows // tm,),
        in_specs=[full(cnt_in), row_blk(D_MODEL), row_blk(oa.shape[1]), row_blk(ob.shape[1]), row_blk(om.shape[1])]
        + [full(a) for a in weights],
        out_specs=[pl.BlockSpec((tm * TILE_ROWS, LANES), lambda i: (i, 0)), row_blk(LANES), row_blk(LANES),
                   pl.BlockSpec((1, LANES), lambda i: (0, 0))] + extra_specs,
        out_shape=[jax.ShapeDtypeStruct((rows * TILE_ROWS, LANES), F32), jax.ShapeDtypeStruct((rows, LANES), I32),
                   jax.ShapeDtypeStruct((rows, LANES), F32), jax.ShapeDtypeStruct((1, LANES), F32)] + extra_shapes,
        scratch_shapes=scratch,
        compiler_params=_cparams(("arbitrary",)),
        name="merge_ln_router",
    )(cnt_in, x, oa, ob, om, *weights)


def _token_tile(ref, t):
    return ref.at[pl.ds(pl.multiple_of(t * TILE_ROWS, TILE_ROWS), TILE_ROWS), :]


def _natural_rows(ref, n, lead=()):
    return jnp.concatenate([ref[lead + (pl.ds(c, n, stride=TILE_ROWS), slice(None))] for c in range(TILE_ROWS)], axis=1)


def _dispatch_body(dest_ref, h_ref, xin_ref, xout_ref, sem, *, tm):
    del xin_ref

    def copy(t, k):
        return pltpu.make_async_copy(_token_tile(h_ref, t), _token_tile(xout_ref, dest_ref[t * TOP_K + k]), sem)

    def start(t, carry):
        for k in range(TOP_K):
            copy(t, k).start(priority=k % 2)
        return carry

    def wait(t, carry):
        for k in range(TOP_K):
            copy(t, k).wait()
        return carry

    lax.fori_loop(0, tm, start, 0, unroll=ROW_DMA_UNROLL)
    lax.fori_loop(0, tm, wait, 0, unroll=ROW_DMA_UNROLL)


def _dispatch(dest_flat, h, x_rows, tm):
    rows = h.shape[0] // TILE_ROWS
    return pl.pallas_call(
        functools.partial(_dispatch_body, tm=tm),
        grid=(rows // tm,),
        in_specs=[pl.BlockSpec((tm * TOP_K,), lambda i: (i,), memory_space=pltpu.SMEM),
                  pl.BlockSpec((tm * TILE_ROWS, LANES), lambda i: (i, 0)),
                  pl.BlockSpec(memory_space=pl.ANY)],
        out_specs=pl.BlockSpec(memory_space=pl.ANY),
        out_shape=jax.ShapeDtypeStruct(x_rows.shape, x_rows.dtype),
        scratch_shapes=[pltpu.SemaphoreType.DMA(())],
        input_output_aliases={2: 0},
        compiler_params=_cparams(("arbitrary",)),
        name="moe_dispatch",
    )(dest_flat, h, x_rows)


def _expert_body(be_ref, nv_ref, slot_ref, next_ref, x_ref, wgu_hbm, bgu_ref, wd_hbm, bd_ref, y_ref,
                 wgu_f32, wd_f32, wgu_bf, wd_bf, sems):
    i = pl.program_id(0)
    valid = i < nv_ref[0]

    def weight_copies(e, slot):
        return (pltpu.make_async_copy(wgu_hbm.at[e], wgu_f32.at[slot], sems.at[0, slot]),
                pltpu.make_async_copy(wd_hbm.at[e], wd_f32.at[slot], sems.at[1, slot]))

    @pl.when(valid)
    def _():
        e = be_ref[i]
        slot = slot_ref[i]
        new_expert = jnp.logical_or(i == 0, e != be_ref[jnp.maximum(i - 1, 0)])

        @pl.when(i == 0)
        def _():
            for c in weight_copies(e, slot):
                c.start()

        @pl.when(new_expert)
        def _():
            for c in weight_copies(e, slot):
                c.wait()

            @pl.when(next_ref[i] >= 0)
            def _():
                for c in weight_copies(next_ref[i], 1 - slot):
                    c.start()

            wgu_bf[...] = wgu_f32[slot].astype(BF16)
            wd_bf[...] = wd_f32[slot].astype(BF16)

        gu = _dot(_natural_rows(x_ref, EXPERT_ROWS).astype(BF16), wgu_bf[...]) + bgu_ref[0]
        g = jnp.minimum(gu[:, :D_FF], SWIGLU_LIMIT)
        u = jnp.clip(gu[:, D_FF:], -SWIGLU_LIMIT, SWIGLU_LIMIT)
        hidden = (u + 1.0) * g * jax.nn.sigmoid(SWIGLU_ALPHA * g)
        y = _dot(hidden.astype(BF16), wd_bf[...]) + bd_ref[0]
        for c in range(TILE_ROWS):
            y_ref[pl.ds(c, EXPERT_ROWS, stride=TILE_ROWS), :] = y[:, c * LANES:(c + 1) * LANES]

    @pl.when(jnp.logical_not(valid))
    def _():
        y_ref[...] = jnp.zeros_like(y_ref)


def _experts(block_expert, n_valid, block_slot, block_next, x_rows, w_gu, b_gu, w_down, b_down):
    blk_rows = EXPERT_ROWS * TILE_ROWS
    n_blocks = x_rows.shape[0] // blk_rows
    last = lambda i, nv: jnp.minimum(i, nv[0] - 1)
    grid_spec = pltpu.PrefetchScalarGridSpec(
        num_scalar_prefetch=4,
        grid=(n_blocks,),
        in_specs=[pl.BlockSpec((blk_rows, LANES), lambda i, be, nv, sl, nx: (last(i, nv), 0)),
                  pl.BlockSpec(memory_space=pl.ANY),
                  pl.BlockSpec((1, 1, 2 * D_FF), lambda i, be, nv, sl, nx: (be[last(i, nv)], 0, 0)),
                  pl.BlockSpec(memory_space=pl.ANY),
                  pl.BlockSpec((1, 1, D_MODEL), lambda i, be, nv, sl, nx: (be[last(i, nv)], 0, 0))],
        out_specs=pl.BlockSpec((blk_rows, LANES), lambda i, be, nv, sl, nx: (i, 0)),
        scratch_shapes=[pltpu.VMEM((2, D_MODEL, 2 * D_FF), F32), pltpu.VMEM((2, D_FF, D_MODEL), F32),
                        pltpu.VMEM((D_MODEL, 2 * D_FF), BF16), pltpu.VMEM((D_FF, D_MODEL), BF16),
                        pltpu.SemaphoreType.DMA((2, 2))],
    )
    return pl.pallas_call(
        _expert_body,
        grid_spec=grid_spec,
        out_shape=jax.ShapeDtypeStruct(x_rows.shape, F32),
        compiler_params=_cparams(("arbitrary",)),
        name="moe_experts",
    )(block_expert, n_valid, block_slot, block_next, x_rows, w_gu, b_gu.reshape(N_EXPERTS, 1, 2 * D_FF), w_down,
      b_down.reshape(N_EXPERTS, 1, D_MODEL))


def _combine_body(dest_ref, dest_next_ref, h_ref, gate_ref, g2_ref, b2_ref, yrows_ref, o_ref, buf, sems, *, tm):
    i = pl.program_id(0)
    n = pl.num_programs(0)
    slot = lax.rem(i, 2)

    def copy(d_ref, s, t, k):
        return pltpu.make_async_copy(_token_tile(yrows_ref, d_ref[t * TOP_K + k]), _token_tile(buf.at[s, k], t),
                                     sems.at[s])

    def start_tile(d_ref, s):
        def body(t, carry):
            for k in range(TOP_K):
                copy(d_ref, s, t, k).start(priority=k % 2)
            return carry
        lax.fori_loop(0, tm, body, 0, unroll=ROW_DMA_UNROLL)

    @pl.when(i == 0)
    def _():
        start_tile(dest_ref, 0)

    @pl.when(i + 1 < n)
    def _():
        start_tile(dest_next_ref, 1 - slot)

    def wait(t, carry):
        for k in range(TOP_K):
            copy(dest_ref, slot, t, k).wait()
        return carry

    lax.fori_loop(0, tm, wait, 0, unroll=ROW_DMA_UNROLL)
    gate = gate_ref[...]
    f = gate[:, 0:1] * _natural_rows(buf, tm, (slot, 0))
    for k in range(1, TOP_K):
        f = f + gate[:, k:k + 1] * _natural_rows(buf, tm, (slot, k))
    o_ref[...] = _layer_norm(DEEPNORM_ALPHA * _natural_rows(h_ref, tm) + f, g2_ref[...], b2_ref[...])


def _combine(dest_flat, h, gate, ln2_g, ln2_b, y_rows, tm):
    rows = h.shape[0] // TILE_ROWS
    n_tiles = rows // tm
    return pl.pallas_call(
        functools.partial(_combine_body, tm=tm),
        grid=(n_tiles,),
        in_specs=[pl.BlockSpec((tm * TOP_K,), lambda i: (i,), memory_space=pltpu.SMEM),
                  pl.BlockSpec((tm * TOP_K,), lambda i: (jnp.minimum(i + 1, n_tiles - 1),), memory_space=pltpu.SMEM),
                  pl.BlockSpec((tm * TILE_ROWS, LANES), lambda i: (i, 0)),
                  pl.BlockSpec((tm, LANES), lambda i: (i, 0)),
                  pl.BlockSpec((1, D_MODEL), lambda i: (0, 0)),
                  pl.BlockSpec((1, D_MODEL), lambda i: (0, 0)),
                  pl.BlockSpec(memory_space=pl.ANY)],
        out_specs=pl.BlockSpec((tm, D_MODEL), lambda i: (i, 0)),
        out_shape=jax.ShapeDtypeStruct((rows, D_MODEL), F32),
        scratch_shapes=[pltpu.VMEM((2, TOP_K, tm * TILE_ROWS, LANES), F32), pltpu.SemaphoreType.DMA((2,))],
        compiler_params=_cparams(("arbitrary",)),
        name="moe_combine_ln2",
    )(dest_flat, dest_flat, h, gate, ln2_g, ln2_b, y_rows)


def _t5_bucket_static(dist):
    d = np.maximum(dist, 0)
    d_large = np.maximum(d, REL_MAX_EXACT).astype(np.float32)
    val = (np.log(d_large / np.float32(REL_MAX_EXACT)) / np.float32(math.log(REL_MAX_DISTANCE / REL_MAX_EXACT))
           * np.float32(REL_BUCKETS - REL_MAX_EXACT))
    inner = (d > REL_MAX_EXACT) & (d < REL_MAX_DISTANCE)
    assert not np.any(inner & (np.abs(val - np.round(val)) < 1e-5))
    large = REL_MAX_EXACT + val.astype(np.int32)
    return np.where(d < REL_MAX_EXACT, d, np.minimum(large, REL_BUCKETS - 1)).astype(np.int32)


def _bias_table(rel_bias, dist, head, valid):
    hi = lax.Precision.HIGHEST
    head_rows = jnp.dot(jnp.asarray(np.eye(REL_HEADS, dtype=np.float32)[np.asarray(head)]),
                        rel_bias.astype(F32).T, precision=hi)
    onehot = (jnp.asarray(_t5_bucket_static(np.asarray(dist)))[..., None]
              == jnp.arange(REL_BUCKETS, dtype=I32)).astype(F32)
    tab = jnp.sum(onehot * head_rows[:, None, :], axis=-1)
    return jnp.where(jnp.asarray(np.asarray(valid)), tab, NEG_INF)


_A_SET_HEADS = ((0, 2, 5, 7), (1, 3, 4, 6))


def _tables_a_prompt(rel_bias, sinks):
    qi = np.arange(QBLK)[:, None]
    kj = np.arange(2 * QBLK)[None, :]
    dist = np.tile(QBLK + qi - kj, (4, 1))
    valid = (dist >= 0) & (dist < A_WINDOW)
    bias = jnp.stack([_bias_table(rel_bias, dist, np.repeat(heads, QBLK), valid) for heads in _A_SET_HEADS])
    s = sinks.astype(F32)
    sink = jnp.stack([jnp.repeat(jnp.stack([s[h] for h in heads]), QBLK).reshape(4 * QBLK, 1)
                      for heads in _A_SET_HEADS])
    return bias, sink


def _tables_b_prompt(rel_bias):
    qi = np.arange(QBLK)[:, None]
    kj = np.arange(2 * QBLK)[None, :]
    sub = np.tile(QBLK + qi - kj, (2, 1))
    out = []
    for g, (window, dil) in enumerate(B_GROUPS):
        valid = (sub >= 0) & (sub <= window // dil)
        out.append(jnp.stack([
            _bias_table(rel_bias, sub * dil, np.repeat(A_HEADS + g * B_KV_HEADS + 2 * sp + np.arange(2), QBLK), valid)
            for sp in range(2)]))
    return jnp.stack(out)


def _tables_a_sample(rel_bias, sinks, la):
    n = SAMPLE_ROWS
    rep = A_HEADS // A_KV_HEADS
    qi = np.tile(np.arange(n), rep)[:, None]
    dist_c = la + qi - np.arange(la)[None, :]
    coln = np.arange(LANES)[None, :]
    dist_n = qi - coln
    valid_c = (dist_c >= 0) & (dist_c < A_WINDOW)
    valid_n = (dist_n >= 0) & (dist_n < A_WINDOW) & (coln < n)
    heads = [np.repeat(np.arange(g * rep, (g + 1) * rep), n) for g in range(A_KV_HEADS)]
    bias_c = jnp.stack([_bias_table(rel_bias, dist_c, hd, valid_c) for hd in heads])
    bias_n = jnp.stack([_bias_table(rel_bias, dist_n, hd, valid_n) for hd in heads])
    sink = jnp.repeat(sinks.astype(F32), n).reshape(A_KV_HEADS, rep * n, 1)
    return bias_c, bias_n, sink


def _tables_b_sample(rel_bias, lb):
    n = SAMPLE_ROWS
    n_groups = len(B_GROUPS)
    h = np.repeat(np.arange(B_KV_HEADS), n_groups * n)
    g = np.tile(np.repeat(np.arange(n_groups), n), B_KV_HEADS)
    qi = np.tile(np.arange(n), B_KV_HEADS * n_groups)[:, None]
    window = np.array([w for w, _ in B_GROUPS])[g][:, None]
    dil = np.array([d for _, d in B_GROUPS])[g][:, None]
    coln = np.arange(LANES)[None, :]

    def table(dist, extra):
        valid = (dist >= 0) & (dist <= window) & (dist % dil == 0) & extra
        return _bias_table(rel_bias, dist, A_HEADS + g * B_KV_HEADS + h, valid)

    return table(lb + qi - np.arange(lb)[None, :], True), table(qi - coln, coln < n)


def _mask_mem_sample():
    row_head = np.repeat(np.arange(M_HEADS), SAMPLE_ROWS)[:, None]
    col_head = np.arange(MEM_TOKENS * M_HEADS)[None, :] % M_HEADS
    return jnp.asarray(np.where(row_head == col_head, 0.0, -np.inf).astype(np.float32))


def _swap_halves(w):
    return jnp.concatenate([w[:, HEAD_DIM:], w[:, :HEAD_DIM]], axis=1)


def _slabs_to_rows(slabs):
    return jnp.moveaxis(slabs, 0, 1).reshape(slabs.shape[1], slabs.shape[0] * LANES)


def _heads_to_rows(o):
    return jnp.moveaxis(o, 0, 1).reshape(o.shape[1], o.shape[0] * HEAD_DIM)


def kernel(x_prompt, x_sample, cache_a_k, cache_a_v, cache_b_k, cache_b_v, cache_mem_k, cache_mem_v, mem_prompt,
           rel_bias, sinks_a, w_in, w_mem_kv, w_br_a, w_br_b, w_br_m, w_o, ln1_g, ln1_b, ln2_g, ln2_b, w_router,
           b_router, w_gu, b_gu, w_down, b_down):
    assert w_in.shape[0] == DEPTH == 1
    bsz, seq, _ = x_prompt.shape
    dec_b, dec_n, _ = x_sample.shape
    assert bsz == 1 and dec_n <= SAMPLE_ROWS
    la, lb = cache_a_k.shape[2], cache_b_k.shape[2]
    assert la == A_WINDOW and lb == B_WINDOW_MAX

    w = w_in[0]
    c = np.cumsum((0, 512, 128, 128, 768, 256, 256, 512, 3072))
    w_qa, w_ka, w_va, w_qb, w_kb, w_vb, w_qm, w_g = (w[:, c[i]:c[i + 1]] for i in range(8))
    w_proj = jnp.concatenate([w_qb, w_kb, w_vb, w_qa, w_ka, _swap_halves(w_ka), w_va, _swap_halves(w_va), w_qm],
                             axis=1).astype(BF16)
    wr = jnp.pad(w_router[0].astype(F32), ((0, 0), (0, LANES - N_EXPERTS)))
    wr_hi = wr.astype(BF16)
    merge_w = dict(
        w_g=w_g.astype(BF16), w_a=w_br_a[0].astype(BF16), w_b=w_br_b[0].astype(BF16), w_m=w_br_m[0].astype(BF16),
        w_o=w_o[0].astype(BF16), ln1_g=ln1_g.astype(F32).reshape(1, D_MODEL), ln1_b=ln1_b.astype(F32).reshape(1, D_MODEL),
        wr_hi=wr_hi, wr_lo=(wr - wr_hi.astype(F32)).astype(BF16),
        b_r=jnp.pad(b_router.astype(F32).reshape(1, N_EXPERTS), ((0, 0), (0, LANES - N_EXPERTS)),
                    constant_values=NEG_INF))
    ln2g = ln2_g.astype(F32).reshape(1, D_MODEL)
    ln2b = ln2_b.astype(F32).reshape(1, D_MODEL)

    xp = x_prompt.reshape(seq, D_MODEL)
    qb_p, kvb_p, qa_p, kva_p, qm_p = _project(xp, w_proj, 512)
    mkv_p = _matmul_f32(mem_prompt.reshape(MEM_TOKENS, D_MODEL), w_mem_kv[0].astype(BF16))
    bias_a_p, sink_a_p = _tables_a_prompt(rel_bias, sinks_a[0])
    oa_p = _attn_a_prompt(qa_p, kva_p, bias_a_p, sink_a_p, 512)
    ob_p = _attn_b_prompt(qb_p, kvb_p, _tables_b_prompt(rel_bias))
    om_p = _attn_mem_prompt(qm_p, mkv_p, 512)
    zero_cnt = jnp.zeros((1, LANES), F32)
    n_tok = seq + dec_b * dec_n
    n_rows = n_tok * TOP_K + N_EXPERTS * EXPERT_ROWS
    h_p, route_p, gate_p, cnt_p, x_rows = _merge(zero_cnt, xp, oa_p, ob_p, om_p, merge_w, 512, zero_rows=n_rows)

    xs = jnp.pad(x_sample, ((0, 0), (0, SAMPLE_ROWS - dec_n), (0, 0))).reshape(dec_b * SAMPLE_ROWS, D_MODEL)
    qb_s, kvb_s, qa_s, kva_s, qm_s = _project(xs, w_proj, 512)
    bias_a_c, bias_a_n, sink_a_s = _tables_a_sample(rel_bias, sinks_a[0], la)
    to_hdt = lambda c: jnp.transpose(c[0], (0, 2, 3, 1))
    oa_s = _attn_a_sample(qa_s, kva_s, to_hdt(cache_a_k), to_hdt(cache_a_v), bias_a_c, bias_a_n, sink_a_s, 8)
    ob_s = _attn_b_sample(qb_s, kvb_s, to_hdt(cache_b_k), to_hdt(cache_b_v), *_tables_b_sample(rel_bias, lb), 2)
    om_s = _attn_mem_sample(qm_s, cache_mem_k[0].reshape(dec_b, MEM_TOKENS * M_HEADS, M_HEAD_DIM),
                            cache_mem_v[0].reshape(dec_b, MEM_TOKENS * M_HEADS, M_HEAD_DIM), _mask_mem_sample(), 8)
    unpad = lambda a: a.reshape(dec_b, SAMPLE_ROWS, a.shape[-1])[:, :dec_n].reshape(dec_b * dec_n, a.shape[-1])
    x_s = x_sample.reshape(dec_b * dec_n, D_MODEL)
    h_s, route_s, gate_s, cnt_all = _merge(cnt_p, x_s, unpad(_heads_to_rows(oa_s)), unpad(ob_s),
                                           unpad(om_s), merge_w, dec_b * dec_n)

    counts = cnt_all[0, :N_EXPERTS].astype(I32)
    padded = (counts + EXPERT_ROWS - 1) // EXPERT_ROWS * EXPERT_ROWS
    pad_end = jnp.cumsum(padded)
    pad_start = pad_end - padded
    n_valid = (pad_end[-1] // EXPERT_ROWS).reshape(1).astype(I32)
    block_row = jnp.arange(n_rows // EXPERT_ROWS, dtype=I32) * EXPERT_ROWS
    block_expert = jnp.minimum(jnp.sum((pad_end[None, :] <= block_row[:, None]).astype(I32), axis=1), N_EXPERTS - 1)
    expert_ids = jnp.arange(N_EXPERTS, dtype=I32)

    def dest(route):
        start = jnp.sum(jnp.where(route[:, :TOP_K, None] == expert_ids, pad_start, 0), axis=-1)
        return (start + route[:, TOP_K:2 * TOP_K]).reshape(-1).astype(I32)

    dest_p, dest_s = dest(route_p), dest(route_s)
    x_rows = _dispatch(dest_p, h_p, x_rows, 512)
    x_rows = _dispatch(dest_s, h_s, x_rows, dec_b * dec_n)
    changed = jnp.concatenate([jnp.zeros((1,), I32), (block_expert[1:] != block_expert[:-1]).astype(I32)])
    block_slot = jnp.cumsum(changed) % 2
    later = (expert_ids[None, :] > expert_ids[:, None]) & (counts[None, :] > 0)
    next_nonempty = jnp.min(jnp.where(later, expert_ids[None, :], N_EXPERTS), axis=1)
    next_nonempty = jnp.where(next_nonempty < N_EXPERTS, next_nonempty, -1)
    block_next = jnp.sum(jnp.where(block_expert[:, None] == expert_ids, next_nonempty, 0), axis=1)
    y_rows = _experts(block_expert, n_valid, block_slot.astype(I32), block_next.astype(I32), x_rows, w_gu[0], b_gu[0],
                      w_down[0], b_down[0])
    y_p = _combine(dest_p, h_p, gate_p, ln2g, ln2b, y_rows, 256)
    y_s = _combine(dest_s, h_s, gate_s, ln2g, ln2b, y_rows, 256)

    kva_rows = _slabs_to_rows(kva_p[:, seq - la:])
    kvb_rows = _slabs_to_rows(kvb_p[:, seq - lb:])
    a_k_p = kva_rows[:, 0:128].reshape(1, 1, la, A_KV_HEADS, HEAD_DIM)
    a_v_p = kva_rows[:, 256:384].reshape(1, 1, la, A_KV_HEADS, HEAD_DIM)
    b_k_p = kvb_rows[:, 0:256].reshape(1, 1, lb, B_KV_HEADS, HEAD_DIM)
    b_v_p = kvb_rows[:, 256:512].reshape(1, 1, lb, B_KV_HEADS, HEAD_DIM)
    m_k_p = mkv_p[:, :M_HEADS * LANES].reshape(1, 1, MEM_TOKENS, M_HEADS, M_HEAD_DIM)
    m_v_p = mkv_p[:, M_HEADS * LANES:].reshape(1, 1, MEM_TOKENS, M_HEADS, M_HEAD_DIM)
    kva_s_rows = unpad(_slabs_to_rows(kva_s))
    kvb_s_rows = unpad(_slabs_to_rows(kvb_s))
    a_k_s = kva_s_rows[:, 0:128].reshape(1, dec_b, dec_n, A_KV_HEADS, HEAD_DIM)
    a_v_s = kva_s_rows[:, 256:384].reshape(1, dec_b, dec_n, A_KV_HEADS, HEAD_DIM)
    b_k_s = kvb_s_rows[:, 0:256].reshape(1, dec_b, dec_n, B_KV_HEADS, HEAD_DIM)
    b_v_s = kvb_s_rows[:, 256:512].reshape(1, dec_b, dec_n, B_KV_HEADS, HEAD_DIM)
    return (y_p.reshape(bsz, seq, D_MODEL), y_s.reshape(dec_b, dec_n, D_MODEL),
            a_k_p, a_v_p, b_k_p, b_v_p, m_k_p, m_v_p, a_k_s, a_v_s, b_k_s, b_v_s)
```

```python
import functools
import math

import numpy as np
import jax
import jax.numpy as jnp
from jax import lax
from jax.experimental import pallas as pl
from jax.experimental.pallas import tpu as pltpu

F32 = jnp.float32
BF16 = jnp.bfloat16
I32 = jnp.int32

D_MODEL = 1024
HEAD_DIM = 64
A_HEADS = 8
A_KV_HEADS = 2
A_WINDOW = 128
B_KV_HEADS = 4
B_GROUPS = ((128, 1), (512, 4), (2048, 16))
B_WINDOW_MAX = 2048
MEM_TOKENS = 256
M_HEADS = 4
M_HEAD_DIM = 128
REL_BUCKETS = 32
REL_MAX_EXACT = REL_BUCKETS // 2
REL_MAX_DISTANCE = B_WINDOW_MAX
REL_HEADS = A_HEADS + B_KV_HEADS * len(B_GROUPS)
N_EXPERTS = 32
TOP_K = 4
D_FF = D_MODEL
SWIGLU_LIMIT = 7.0
SWIGLU_ALPHA = 1.702
LN_EPS = 1e-5
DEPTH = 1
DEEPNORM_ALPHA = (2 * DEPTH) ** 0.25

LANES = 128
TILE_ROWS = 8
QBLK = 128
SAMPLE_ROWS = 8
EXPERT_ROWS = 512
VMEM_LIMIT = 56 * 1024 * 1024
NEG_INF = float("-inf")
B_DIL_MAX = B_GROUPS[-1][1]
ROW_DMA_UNROLL = 8
SAMPLE_UNROLL = 4
B_UNIT_UNROLL = 8

PROJ_SLABS = (6, 4, 4, 4, 4)


def _cparams(sem):
    return pltpu.CompilerParams(dimension_semantics=sem, vmem_limit_bytes=VMEM_LIMIT)


def _dot(a, b):
    return jnp.dot(a, b, preferred_element_type=F32)


def _dot_t(a, b):
    return lax.dot_general(a, b, (((1,), (1,)), ((), ())), preferred_element_type=F32)


def _lo_mask(rows):
    return lax.broadcasted_iota(I32, (rows, LANES), 1) < HEAD_DIM


def _proj_body(x_ref, w_ref, *o_refs):
    x = x_ref[...].astype(BF16)
    col = 0
    for o_ref in o_refs:
        n = o_ref.shape[0] * LANES
        acc = _dot(x, w_ref[:, col:col + n])
        for s in range(o_ref.shape[0]):
            o_ref[s] = acc[:, s * LANES:(s + 1) * LANES]
        col += n


def _project(x, w_bf, tm):
    rows = x.shape[0]
    n_cols = w_bf.shape[1]
    assert rows % tm == 0 and n_cols == sum(PROJ_SLABS) * LANES
    return pl.pallas_call(
        _proj_body,
        grid=(rows // tm,),
        in_specs=[pl.BlockSpec((tm, D_MODEL), lambda i: (i, 0)),
                  pl.BlockSpec((D_MODEL, n_cols), lambda i: (0, 0))],
        out_specs=[pl.BlockSpec((n, tm, LANES), lambda i: (0, i, 0)) for n in PROJ_SLABS],
        out_shape=[jax.ShapeDtypeStruct((n, rows, LANES), F32) for n in PROJ_SLABS],
        compiler_params=_cparams(("parallel",)),
        name="in_proj",
    )(x, w_bf)


def _matmul_body(x_ref, w_ref, o_ref):
    o_ref[...] = _dot(x_ref[...].astype(BF16), w_ref[...])


def _matmul_f32(x, w_bf):
    return pl.pallas_call(
        _matmul_body,
        out_shape=jax.ShapeDtypeStruct((x.shape[0], w_bf.shape[1]), F32),
        compiler_params=_cparams(None),
        name="mem_kv_proj",
    )(x, w_bf)


def _softmax_lse(s):
    m = jnp.max(s, axis=-1, keepdims=True)
    p = jnp.exp(s - m)
    l = jnp.sum(p, axis=-1, keepdims=True)
    return p / l, m + jnp.log(l)


def _softmax_parts(parts, sink=None):
    m = functools.reduce(jnp.maximum, [jnp.max(s, axis=-1, keepdims=True) for s in parts])
    if sink is not None:
        m = jnp.maximum(m, sink)
    ps = [jnp.exp(s - m) for s in parts]
    l = functools.reduce(jnp.add, [jnp.sum(p, axis=-1, keepdims=True) for p in ps])
    lse = m + jnp.log(l)
    if sink is not None:
        l = l + jnp.exp(sink - m)
    return [p / l for p in ps], lse


def _attn_a_prompt_body(q_ref, kvc_ref, kvp_ref, bias_ref, sink_ref, o_ref, *, nsub):
    first = pl.program_id(0) == 0
    lo = _lo_mask(QBLK)
    col = lax.broadcasted_iota(I32, (4 * QBLK, 2 * QBLK), 1)
    ones = jnp.ones((2 * QBLK, LANES), BF16)
    for j in range(nsub):
        rows = slice(j * QBLK, (j + 1) * QBLK)

        def keys(slab):
            cur = kvc_ref[slab, rows, :]
            prev = kvp_ref[slab] if j == 0 else kvc_ref[slab, (j - 1) * QBLK:j * QBLK, :]
            return jnp.concatenate([prev, cur], axis=0).astype(BF16)

        q = [q_ref[p, rows, :] for p in range(4)]
        zero = jnp.zeros_like(q[0])
        q_sets = (
            jnp.concatenate([jnp.where(lo, q[0], zero), jnp.where(lo, q[1], zero),
                             jnp.where(lo, zero, q[2]), jnp.where(lo, zero, q[3])], axis=0).astype(BF16),
            jnp.concatenate([jnp.where(lo, zero, q[0]), jnp.where(lo, zero, q[1]),
                             jnp.where(lo, q[2], zero), jnp.where(lo, q[3], zero)], axis=0).astype(BF16),
        )
        outs = []
        for st in range(2):
            s = _dot_t(q_sets[st], keys(st)) * (HEAD_DIM ** -0.5) + bias_ref[st]
            if j == 0:
                s = jnp.where(jnp.logical_and(first, col < QBLK), NEG_INF, s)
            sink = sink_ref[st]
            m = jnp.maximum(jnp.max(s, axis=-1, keepdims=True), sink)
            ol = _dot(jnp.exp(s - m).astype(BF16), jnp.concatenate([keys(2 + st), ones], axis=1))
            outs.append(ol[:, :LANES] / (ol[:, LANES:] + jnp.exp(sink - m)))
        o1, o2 = outs
        blk = lambda o, r: o[r * QBLK:(r + 1) * QBLK]
        slabs = (jnp.where(lo, blk(o1, 0), blk(o2, 0)), jnp.where(lo, blk(o1, 1), blk(o2, 1)),
                 jnp.where(lo, blk(o2, 2), blk(o1, 2)), jnp.where(lo, blk(o2, 3), blk(o1, 3)))
        for p in range(4):
            o_ref[rows, p * LANES:(p + 1) * LANES] = slabs[p].astype(o_ref.dtype)


def _attn_a_prompt(qa, kva, bias, sink, tq):
    seq = qa.shape[1]
    nsub = tq // QBLK
    return pl.pallas_call(
        functools.partial(_attn_a_prompt_body, nsub=nsub),
        grid=(seq // tq,),
        in_specs=[pl.BlockSpec((4, tq, LANES), lambda n: (0, n, 0)),
                  pl.BlockSpec((4, tq, LANES), lambda n: (0, n, 0)),
                  pl.BlockSpec((4, QBLK, LANES), lambda n: (0, jnp.maximum(n * nsub - 1, 0), 0)),
                  pl.BlockSpec((2, 4 * QBLK, 2 * QBLK), lambda n: (0, 0, 0)),
                  pl.BlockSpec((2, 4 * QBLK, 1), lambda n: (0, 0, 0))],
        out_specs=pl.BlockSpec((tq, 4 * LANES), lambda n: (n, 0)),
        out_shape=jax.ShapeDtypeStruct((seq, 4 * LANES), BF16),
        compiler_params=_cparams(("parallel",)),
        name="attn_a_prompt",
    )(qa, kva, kva, bias, sink)


def _attn_b_prompt_body(q_ref, kvc_ref, kvp_ref, bias_ref, o_ref, kv_buf, o_buf, l_buf, *, sb):
    first = pl.program_id(0) == 0
    kv_buf[:, :sb, :] = kvp_ref[...]
    kv_buf[:, sb:, :] = kvc_ref[...]
    lo = _lo_mask(QBLK)
    col = lax.broadcasted_iota(I32, (2 * QBLK, 2 * QBLK), 1)
    ones = jnp.ones((2 * QBLK, LANES), BF16)
    units = sb // QBLK
    for g, (_, dil) in enumerate(B_GROUPS):
        blocks_per_class = units // dil

        def unit(u, carry, g=g, dil=dil, blocks_per_class=blocks_per_class):
            r = u // blocks_per_class
            m = u % blocks_per_class
            q_start = r + dil * QBLK * m
            k_start = sb - dil * QBLK + q_start
            if dil == 1:
                q_start = pl.multiple_of(q_start, QBLK)
                k_start = pl.multiple_of(k_start, QBLK)
                q_idx = pl.ds(q_start, QBLK)
                k_idx = pl.ds(k_start, 2 * QBLK)
            else:
                q_idx = pl.ds(q_start, QBLK, stride=dil)
                k_idx = pl.ds(k_start, 2 * QBLK, stride=dil)
            for sp in range(2):
                q = q_ref[2 * g + sp, q_idx, :]
                zero = jnp.zeros_like(q)
                q2 = jnp.concatenate([jnp.where(lo, q, zero), jnp.where(lo, zero, q)], axis=0).astype(BF16)
                k = kv_buf[sp, k_idx, :].astype(BF16)
                v = kv_buf[2 + sp, k_idx, :].astype(BF16)
                s = _dot_t(q2, k) * (HEAD_DIM ** -0.5) + bias_ref[g, sp]
                s = jnp.where(jnp.logical_and(jnp.logical_and(first, m == 0), col < QBLK), NEG_INF, s)
                m = jnp.max(s, axis=-1, keepdims=True)
                ol = _dot(jnp.exp(s - m).astype(BF16), jnp.concatenate([v, ones], axis=1))
                o = ol[:, :LANES] / ol[:, LANES:]
                lse = m + jnp.log(ol[:, LANES:])
                o_buf[g, sp, q_idx, :] = jnp.where(lo, o[:QBLK], o[QBLK:])
                l_buf[g, sp, q_idx, :] = jnp.where(lo, lse[:QBLK], lse[QBLK:])
            return carry

        lax.fori_loop(0, units, unit, 0, unroll=B_UNIT_UNROLL)
    for sp in range(2):
        l = [l_buf[g, sp] for g in range(3)]
        mx = jnp.maximum(jnp.maximum(l[0], l[1]), l[2])
        e = [jnp.exp(x - mx) for x in l]
        den = e[0] + e[1] + e[2]
        acc = (e[0] / den) * o_buf[0, sp] + (e[1] / den) * o_buf[1, sp] + (e[2] / den) * o_buf[2, sp]
        o_ref[:, sp * LANES:(sp + 1) * LANES] = acc.astype(o_ref.dtype)


def _attn_b_prompt(qb, kvb, bias):
    seq = qb.shape[1]
    sb = B_DIL_MAX * QBLK
    assert seq % sb == 0
    return pl.pallas_call(
        functools.partial(_attn_b_prompt_body, sb=sb),
        grid=(seq // sb,),
        in_specs=[pl.BlockSpec((6, sb, LANES), lambda n: (0, n, 0)),
                  pl.BlockSpec((4, sb, LANES), lambda n: (0, n, 0)),
                  pl.BlockSpec((4, sb, LANES), lambda n: (0, jnp.maximum(n - 1, 0), 0)),
                  pl.BlockSpec((3, 2, 2 * QBLK, 2 * QBLK), lambda n: (0, 0, 0, 0))],
        out_specs=pl.BlockSpec((sb, 2 * LANES), lambda n: (n, 0)),
        out_shape=jax.ShapeDtypeStruct((seq, 2 * LANES), BF16),
        scratch_shapes=[pltpu.VMEM((4, 2 * sb, LANES), F32),
                        pltpu.VMEM((3, 2, sb, LANES), F32),
                        pltpu.VMEM((3, 2, sb, LANES), F32)],
        compiler_params=_cparams(("parallel",)),
        name="attn_b_prompt",
    )(qb, kvb, kvb, bias)


def _attn_mem_prompt_body(q_ref, mkv_ref, o_ref):
    ones = jnp.ones((MEM_TOKENS, LANES), BF16)
    for h in range(M_HEADS):
        k = mkv_ref[:, h * LANES:(h + 1) * LANES].astype(BF16)
        v = mkv_ref[:, (M_HEADS + h) * LANES:(M_HEADS + h + 1) * LANES].astype(BF16)
        s = _dot_t(q_ref[h].astype(BF16), k) * (M_HEAD_DIM ** -0.5)
        m = jnp.max(s, axis=-1, keepdims=True)
        ol = _dot(jnp.exp(s - m).astype(BF16), jnp.concatenate([v, ones], axis=1))
        o_ref[:, h * LANES:(h + 1) * LANES] = (ol[:, :LANES] / ol[:, LANES:]).astype(o_ref.dtype)


def _attn_mem_prompt(qm, mkv, tq):
    seq = qm.shape[1]
    return pl.pallas_call(
        _attn_mem_prompt_body,
        grid=(seq // tq,),
        in_specs=[pl.BlockSpec((M_HEADS, tq, LANES), lambda n: (0, n, 0)),
                  pl.BlockSpec((MEM_TOKENS, 2 * M_HEADS * LANES), lambda n: (0, 0))],
        out_specs=pl.BlockSpec((tq, M_HEADS * LANES), lambda n: (n, 0)),
        out_shape=jax.ShapeDtypeStruct((seq, M_HEADS * LANES), BF16),
        compiler_params=_cparams(("parallel",)),
        name="attn_mem_prompt",
    )(qm, mkv)


def _attn_mem_sample_body(q_ref, mk_ref, mv_ref, mask_ref, o_ref, *, bb):
    def one(b, carry):
        rows = pl.ds(pl.multiple_of(b * SAMPLE_ROWS, SAMPLE_ROWS), SAMPLE_ROWS)
        q = jnp.concatenate([q_ref[h, rows, :] for h in range(M_HEADS)], axis=0).astype(BF16)
        s = _dot_t(q, mk_ref[b].astype(BF16)) * (M_HEAD_DIM ** -0.5) + mask_ref[...]
        p, _ = _softmax_lse(s)
        o = _dot(p.astype(BF16), mv_ref[b].astype(BF16))
        for h in range(M_HEADS):
            o_ref[rows, h * LANES:(h + 1) * LANES] = o[h * SAMPLE_ROWS:(h + 1) * SAMPLE_ROWS].astype(o_ref.dtype)
        return carry

    lax.fori_loop(0, bb, one, 0, unroll=SAMPLE_UNROLL)


def _attn_mem_sample(qm, mk, mv, mask, bb):
    batch, n_rows = mk.shape[0], mk.shape[1]
    return pl.pallas_call(
        functools.partial(_attn_mem_sample_body, bb=bb),
        grid=(batch // bb,),
        in_specs=[pl.BlockSpec((M_HEADS, bb * SAMPLE_ROWS, LANES), lambda n: (0, n, 0)),
                  pl.BlockSpec((bb, n_rows, LANES), lambda n: (n, 0, 0)),
                  pl.BlockSpec((bb, n_rows, LANES), lambda n: (n, 0, 0)),
                  pl.BlockSpec(mask.shape, lambda n: (0, 0))],
        out_specs=pl.BlockSpec((bb * SAMPLE_ROWS, M_HEADS * LANES), lambda n: (n, 0)),
        out_shape=jax.ShapeDtypeStruct((batch * SAMPLE_ROWS, M_HEADS * LANES), BF16),
        compiler_params=_cparams(("parallel",)),
        name="attn_mem_sample",
    )(qm, mk, mv, mask)


def _head_rows(ref, slab, rows, half):
    x = ref[slab, rows, :]
    return x[:, half * HEAD_DIM:(half + 1) * HEAD_DIM]


def _pad_rows(x, n):
    return jnp.concatenate([x, jnp.zeros((n - x.shape[0], x.shape[1]), x.dtype)], axis=0)


def _attn_a_sample_body(q_ref, kvn_ref, ck_ref, cv_ref, bias_c_ref, bias_n_ref, sink_ref, o_ref, *, bb):
    rep = A_HEADS // A_KV_HEADS

    def one(b, carry):
        rows = pl.ds(pl.multiple_of(b * SAMPLE_ROWS, SAMPLE_ROWS), SAMPLE_ROWS)
        for g in range(A_KV_HEADS):
            heads = range(g * rep, (g + 1) * rep)
            q = jnp.concatenate([_head_rows(q_ref, h // 2, rows, h % 2) for h in heads], axis=0).astype(BF16)
            kn = _pad_rows(_head_rows(kvn_ref, 0, rows, g), LANES).astype(BF16)
            vn = _pad_rows(_head_rows(kvn_ref, 2, rows, g), LANES).astype(BF16)
            s_c = _dot(q, ck_ref[b, g].astype(BF16)) * (HEAD_DIM ** -0.5) + bias_c_ref[g]
            s_n = _dot_t(q, kn) * (HEAD_DIM ** -0.5) + bias_n_ref[g]
            (p_c, p_n), _ = _softmax_parts([s_c, s_n], sink_ref[g])
            o = _dot_t(p_c.astype(BF16), cv_ref[b, g].astype(BF16)) + _dot(p_n.astype(BF16), vn)
            for r, h in enumerate(heads):
                o_ref[h, rows, :] = o[r * SAMPLE_ROWS:(r + 1) * SAMPLE_ROWS].astype(o_ref.dtype)
        return carry

    lax.fori_loop(0, bb, one, 0, unroll=SAMPLE_UNROLL)


def _attn_a_sample(qa, kva, ck, cv, bias_c, bias_n, sink, bb):
    batch = ck.shape[0]
    full = lambda a: pl.BlockSpec(a.shape, lambda n: (0,) * a.ndim)
    buf_spec = pl.BlockSpec((bb,) + ck.shape[1:], lambda n: (n, 0, 0, 0))
    return pl.pallas_call(
        functools.partial(_attn_a_sample_body, bb=bb),
        grid=(batch // bb,),
        in_specs=[pl.BlockSpec((4, bb * SAMPLE_ROWS, LANES), lambda n: (0, n, 0)),
                  pl.BlockSpec((4, bb * SAMPLE_ROWS, LANES), lambda n: (0, n, 0)),
                  buf_spec, buf_spec, full(bias_c), full(bias_n), full(sink)],
        out_specs=pl.BlockSpec((A_HEADS, bb * SAMPLE_ROWS, HEAD_DIM), lambda n: (0, n, 0)),
        out_shape=jax.ShapeDtypeStruct((A_HEADS, batch * SAMPLE_ROWS, HEAD_DIM), BF16),
        compiler_params=_cparams(("parallel",)),
        name="attn_a_sample",
    )(qa, kva, ck, cv, bias_c, bias_n, sink)


def _attn_b_sample_body(q_ref, kvn_ref, ck_ref, cv_ref, bias_c_ref, bias_n_ref, o_ref, *, bb):
    n = SAMPLE_ROWS
    n_groups = len(B_GROUPS)
    lo = _lo_mask(n)
    scale = HEAD_DIM ** -0.5
    zero = jnp.zeros((n, LANES), F32)
    for b in range(bb):
        rows = slice(b * n, (b + 1) * n)
        q_rows = []
        for h in range(B_KV_HEADS):
            for g in range(n_groups):
                x = q_ref[2 * g + h // 2, rows, :]
                x = jnp.where(lo, x, zero) if h % 2 == 0 else jnp.where(lo, zero, x)
                q_rows.append(jnp.concatenate([x, zero] if h // 2 == 0 else [zero, x], axis=1))
        q = jnp.concatenate(q_rows, axis=0).astype(BF16)
        kn = _pad_rows(jnp.concatenate([kvn_ref[0, rows, :], kvn_ref[1, rows, :]], axis=1), LANES).astype(BF16)
        vn = _pad_rows(jnp.concatenate([kvn_ref[2, rows, :], kvn_ref[3, rows, :]], axis=1), LANES).astype(BF16)
        kt = ck_ref[b].reshape(B_KV_HEADS * HEAD_DIM, ck_ref.shape[-1]).astype(BF16)
        vt = cv_ref[b].reshape(B_KV_HEADS * HEAD_DIM, cv_ref.shape[-1]).astype(BF16)
        s_c = _dot(q, kt) * scale + bias_c_ref[...]
        s_n = _dot_t(q, kn) * scale + bias_n_ref[...]
        (p_c, p_n), lse = _softmax_parts([s_c, s_n])
        o = _dot_t(p_c.astype(BF16), vt) + _dot(p_n.astype(BF16), vn)
        merged = []
        for h in range(B_KV_HEADS):
            cols = slice((h // 2) * LANES, (h // 2 + 1) * LANES)
            blk = lambda g: slice((n_groups * h + g) * n, (n_groups * h + g + 1) * n)
            l_g = [lse[blk(g)] for g in range(n_groups)]
            mx = jnp.maximum(jnp.maximum(l_g[0], l_g[1]), l_g[2])
            e = [jnp.exp(x - mx) for x in l_g]
            den = e[0] + e[1] + e[2]
            merged.append((e[0] / den) * o[blk(0), cols] + (e[1] / den) * o[blk(1), cols]
                          + (e[2] / den) * o[blk(2), cols])
        for cg in range(2):
            o_ref[rows, cg * LANES:(cg + 1) * LANES] = jnp.where(lo, merged[2 * cg], merged[2 * cg + 1]).astype(o_ref.dtype)


def _attn_b_sample(qb, kvb, ck, cv, bias_c, bias_n, bb):
    batch = ck.shape[0]
    full = lambda a: pl.BlockSpec(a.shape, lambda n: (0,) * a.ndim)
    buf_spec = pl.BlockSpec((bb,) + ck.shape[1:], lambda n: (n, 0, 0, 0))
    return pl.pallas_call(
        functools.partial(_attn_b_sample_body, bb=bb),
        grid=(batch // bb,),
        in_specs=[pl.BlockSpec((6, bb * SAMPLE_ROWS, LANES), lambda n: (0, n, 0)),
                  pl.BlockSpec((4, bb * SAMPLE_ROWS, LANES), lambda n: (0, n, 0)),
                  buf_spec, buf_spec, full(bias_c), full(bias_n)],
        out_specs=pl.BlockSpec((bb * SAMPLE_ROWS, B_KV_HEADS * HEAD_DIM), lambda n: (n, 0)),
        out_shape=jax.ShapeDtypeStruct((batch * SAMPLE_ROWS, B_KV_HEADS * HEAD_DIM), BF16),
        compiler_params=_cparams(("parallel",)),
        name="attn_b_sample",
    )(qb, kvb, ck, cv, bias_c, bias_n)


def _layer_norm(x, g, b):
    mu = jnp.mean(x, axis=-1, keepdims=True)
    xc = x - mu
    var = jnp.mean(xc * xc, axis=-1, keepdims=True)
    return xc * lax.rsqrt(var + LN_EPS) * g + b


def _split_bf16(x):
    hi = x.astype(BF16)
    return hi, (x - hi.astype(F32)).astype(BF16)


def _merge_body(cnt_in_ref, x_ref, oa_ref, ob_ref, om_ref, wg_ref, wa_ref, wb_ref, wm_ref, wo_ref, g1_ref, b1_ref,
                wr_hi_ref, wr_lo_ref, br_ref, h_ref, route_ref, gate_ref, cnt_ref, *zero_fill, tm, n_sub, zero_blocks,
                per_step):
    i = pl.program_id(0)
    if zero_blocks:
        rows_ref, zbuf, zsem = zero_fill
        blk = EXPERT_ROWS * TILE_ROWS

        @pl.when(i == 0)
        def _():
            zbuf[...] = jnp.zeros_like(zbuf)

        def zero_copy(j):
            start = pl.multiple_of((i * per_step + j) * blk, blk)
            return pltpu.make_async_copy(zbuf, rows_ref.at[pl.ds(start, blk), :], zsem)

        def zero_each(fn):
            def body(j, carry):
                @pl.when(i * per_step + j < zero_blocks)
                def _():
                    fn(zero_copy(j))
                return carry
            lax.fori_loop(0, per_step, body, 0)

        zero_each(lambda c: c.start())
    @pl.when(i == 0)
    def _():
        cnt_ref[...] = cnt_in_ref[...]

    lane = lax.broadcasted_iota(I32, (tm, LANES), 1)
    lane_f = lane.astype(F32)
    row = lax.broadcasted_iota(I32, (tm, tm), 0)
    colm = lax.broadcasted_iota(I32, (tm, tm), 1)
    tri = jnp.where(colm < row, 1.0, 0.0).astype(BF16)
    carry = cnt_ref[...]
    for r in range(n_sub):
        rs = slice(r * tm, (r + 1) * tm)
        x = x_ref[rs, :]
        gates = jax.nn.sigmoid(_dot(x.astype(BF16), wg_ref[...]))
        u = (gates[:, :D_MODEL] * _dot(oa_ref[rs, :], wa_ref[...])
             + gates[:, D_MODEL:2 * D_MODEL] * _dot(ob_ref[rs, :], wb_ref[...])
             + gates[:, 2 * D_MODEL:] * _dot(om_ref[rs, :], wm_ref[...]))
        mixed = _dot(u.astype(BF16), wo_ref[...])
        h = _layer_norm(DEEPNORM_ALPHA * x + mixed, g1_ref[...], b1_ref[...])
        for c in range(TILE_ROWS):
            h_ref[pl.ds(r * tm * TILE_ROWS + c, tm, stride=TILE_ROWS), :] = h[:, c * LANES:(c + 1) * LANES]

        h_hi, h_lo = _split_bf16(h)
        logits = (_dot(h_hi, wr_hi_ref[...]) + (_dot(h_hi, wr_lo_ref[...]) + _dot(h_lo, wr_hi_ref[...]))
                  + br_ref[...])
        work = logits
        vals, sels, idxs = [], [], []
        for _ in range(TOP_K):
            mx = jnp.max(work, axis=-1, keepdims=True)
            idx = jnp.min(jnp.where(work == mx, lane_f, float(LANES)), axis=-1, keepdims=True)
            sel = lane_f == idx
            vals.append(mx)
            idxs.append(idx)
            sels.append(sel)
            work = jnp.where(sel, NEG_INF, work)
        ex = [jnp.exp(v - vals[0]) for v in vals]
        den = ex[0] + ex[1] + ex[2] + ex[3]
        chosen = jnp.logical_or(jnp.logical_or(sels[0], sels[1]), jnp.logical_or(sels[2], sels[3]))
        onehot = jnp.where(chosen, 1.0, 0.0)
        before = _dot(tri, onehot.astype(BF16)) + carry
        route = jnp.zeros((tm, LANES), F32)
        gate = jnp.zeros((tm, LANES), F32)
        for k in range(TOP_K):
            rank = jnp.sum(jnp.where(sels[k], before, 0.0), axis=-1, keepdims=True)
            route = jnp.where(lane == k, idxs[k], route)
            route = jnp.where(lane == TOP_K + k, rank, route)
            gate = jnp.where(lane == k, ex[k] / den, gate)
        route_ref[rs, :] = route.astype(I32)
        gate_ref[rs, :] = gate
        carry = carry + jnp.sum(onehot, axis=0, keepdims=True)
    cnt_ref[...] = carry
    if zero_blocks:
        zero_each(lambda c: c.wait())


def _merge(cnt_in, x, oa, ob, om, w, tm, n_sub, zero_rows=0):
    rows = x.shape[0]
    step_rows = tm * n_sub
    assert zero_rows % EXPERT_ROWS == 0
    zero_blocks = zero_rows // EXPERT_ROWS
    extra_specs = [pl.BlockSpec(memory_space=pl.ANY)] if zero_blocks else []
    extra_shapes = [jax.ShapeDtypeStruct((zero_rows * TILE_ROWS, LANES), F32)] if zero_blocks else []
    scratch = ([pltpu.VMEM((EXPERT_ROWS * TILE_ROWS, LANES), F32), pltpu.SemaphoreType.DMA(())]
               if zero_blocks else [])
    full = lambda a: pl.BlockSpec(a.shape, lambda i: (0,) * a.ndim)
    row_blk = lambda c: pl.BlockSpec((step_rows, c), lambda i: (i, 0))
    weights = (w["w_g"], w["w_a"], w["w_b"], w["w_m"], w["w_o"], w["ln1_g"], w["ln1_b"],
               w["wr_hi"], w["wr_lo"], w["b_r"])
    return pl.pallas_call(
        functools.partial(_merge_body, tm=tm, n_sub=n_sub, zero_blocks=zero_blocks,
                          per_step=-(-zero_blocks // (rows // step_rows))),
        grid=(rows // step_rows,),
        in_specs=[full(cnt_in), row_blk(D_MODEL), row_blk(oa.shape[1]), row_blk(ob.shape[1]), row_blk(om.shape[1])]
        + [full(a) for a in weights],
        out_specs=[pl.BlockSpec((step_rows * TILE_ROWS, LANES), lambda i: (i, 0)), row_blk(LANES), row_blk(LANES),
                   pl.BlockSpec((1, LANES), lambda i: (0, 0))] + extra_specs,
        out_shape=[jax.ShapeDtypeStruct((rows * TILE_ROWS, LANES), F32), jax.ShapeDtypeStruct((rows, LANES), I32),
                   jax.ShapeDtypeStruct((rows, LANES), F32), jax.ShapeDtypeStruct((1, LANES), F32)] + extra_shapes,
        scratch_shapes=scratch,
        compiler_params=_cparams(("arbitrary",)),
        name="merge_ln_router",
    )(cnt_in, x, oa, ob, om, *weights)


def _token_tile(ref, t):
    return ref.at[pl.ds(pl.multiple_of(t * TILE_ROWS, TILE_ROWS), TILE_ROWS), :]


def _natural_rows(ref, n, lead=()):
    return jnp.concatenate([ref[lead + (pl.ds(c, n, stride=TILE_ROWS), slice(None))] for c in range(TILE_ROWS)], axis=1)


def _dispatch_body(dest_ref, h_ref, xin_ref, xout_ref, sem, *, tm):
    del xin_ref

    def copy(t, k):
        return pltpu.make_async_copy(_token_tile(h_ref, t), _token_tile(xout_ref, dest_ref[t * TOP_K + k]), sem)

    def start(t, carry):
        for k in range(TOP_K):
            copy(t, k).start(priority=k % 2)
        return carry

    def wait(t, carry):
        for k in range(TOP_K):
            copy(t, k).wait()
        return carry

    lax.fori_loop(0, tm, start, 0, unroll=ROW_DMA_UNROLL)
    lax.fori_loop(0, tm, wait, 0, unroll=ROW_DMA_UNROLL)


def _dispatch(dest_flat, h, x_rows, tm):
    rows = h.shape[0] // TILE_ROWS
    return pl.pallas_call(
        functools.partial(_dispatch_body, tm=tm),
        grid=(rows // tm,),
        in_specs=[pl.BlockSpec((tm * TOP_K,), lambda i: (i,), memory_space=pltpu.SMEM),
                  pl.BlockSpec((tm * TILE_ROWS, LANES), lambda i: (i, 0)),
                  pl.BlockSpec(memory_space=pl.ANY)],
        out_specs=pl.BlockSpec(memory_space=pl.ANY),
        out_shape=jax.ShapeDtypeStruct(x_rows.shape, x_rows.dtype),
        scratch_shapes=[pltpu.SemaphoreType.DMA(())],
        input_output_aliases={2: 0},
        compiler_params=_cparams(("arbitrary",)),
        name="moe_dispatch",
    )(dest_flat, h, x_rows)


def _expert_body(be_ref, nv_ref, slot_ref, next_ref, x_ref, wgu_hbm, bgu_ref, wd_hbm, bd_ref, y_ref,
                 wgu_f32, wd_f32, wgu_bf, wd_bf, sems):
    i = pl.program_id(0)
    valid = i < nv_ref[0]

    def weight_copies(e, slot):
        return (pltpu.make_async_copy(wgu_hbm.at[e], wgu_f32.at[slot], sems.at[0, slot]),
                pltpu.make_async_copy(wd_hbm.at[e], wd_f32.at[slot], sems.at[1, slot]))

    @pl.when(valid)
    def _():
        e = be_ref[i]
        slot = slot_ref[i]
        new_expert = jnp.logical_or(i == 0, e != be_ref[jnp.maximum(i - 1, 0)])

        @pl.when(i == 0)
        def _():
            for c in weight_copies(e, slot):
                c.start()

        @pl.when(new_expert)
        def _():
            for c in weight_copies(e, slot):
                c.wait()

            @pl.when(next_ref[i] >= 0)
            def _():
                for c in weight_copies(next_ref[i], 1 - slot):
                    c.start()

            wgu_bf[...] = wgu_f32[slot].astype(BF16)
            wd_bf[...] = wd_f32[slot].astype(BF16)

        gu = _dot(_natural_rows(x_ref, EXPERT_ROWS).astype(BF16), wgu_bf[...]) + bgu_ref[0]
        g = jnp.minimum(gu[:, :D_FF], SWIGLU_LIMIT)
        u = jnp.clip(gu[:, D_FF:], -SWIGLU_LIMIT, SWIGLU_LIMIT)
        hidden = (u + 1.0) * g * jax.nn.sigmoid(SWIGLU_ALPHA * g)
        y = _dot(hidden.astype(BF16), wd_bf[...]) + bd_ref[0]
        for c in range(TILE_ROWS):
            y_ref[pl.ds(c, EXPERT_ROWS, stride=TILE_ROWS), :] = y[:, c * LANES:(c + 1) * LANES]

    @pl.when(jnp.logical_not(valid))
    def _():
        y_ref[...] = jnp.zeros_like(y_ref)


def _experts(block_expert, n_valid, block_slot, block_next, x_rows, w_gu, b_gu, w_down, b_down):
    blk_rows = EXPERT_ROWS * TILE_ROWS
    n_blocks = x_rows.shape[0] // blk_rows
    last = lambda i, nv: jnp.minimum(i, nv[0] - 1)
    grid_spec = pltpu.PrefetchScalarGridSpec(
        num_scalar_prefetch=4,
        grid=(n_blocks,),
        in_specs=[pl.BlockSpec((blk_rows, LANES), lambda i, be, nv, sl, nx: (last(i, nv), 0)),
                  pl.BlockSpec(memory_space=pl.ANY),
                  pl.BlockSpec((1, 1, 2 * D_FF), lambda i, be, nv, sl, nx: (be[last(i, nv)], 0, 0)),
                  pl.BlockSpec(memory_space=pl.ANY),
                  pl.BlockSpec((1, 1, D_MODEL), lambda i, be, nv, sl, nx: (be[last(i, nv)], 0, 0))],
        out_specs=pl.BlockSpec((blk_rows, LANES), lambda i, be, nv, sl, nx: (i, 0)),
        scratch_shapes=[pltpu.VMEM((2, D_MODEL, 2 * D_FF), F32), pltpu.VMEM((2, D_FF, D_MODEL), F32),
                        pltpu.VMEM((D_MODEL, 2 * D_FF), BF16), pltpu.VMEM((D_FF, D_MODEL), BF16),
                        pltpu.SemaphoreType.DMA((2, 2))],
    )
    return pl.pallas_call(
        _expert_body,
        grid_spec=grid_spec,
        out_shape=jax.ShapeDtypeStruct(x_rows.shape, F32),
        compiler_params=_cparams(("arbitrary",)),
        name="moe_experts",
    )(block_expert, n_valid, block_slot, block_next, x_rows, w_gu, b_gu.reshape(N_EXPERTS, 1, 2 * D_FF), w_down,
      b_down.reshape(N_EXPERTS, 1, D_MODEL))


def _combine_body(dest_ref, dest_next_ref, h_ref, gate_ref, g2_ref, b2_ref, yrows_ref, o_ref, buf, sems, *, tm):
    i = pl.program_id(0)
    n = pl.num_programs(0)
    slot = lax.rem(i, 2)

    def copy(d_ref, s, t, k):
        return pltpu.make_async_copy(_token_tile(yrows_ref, d_ref[t * TOP_K + k]), _token_tile(buf.at[s, k], t),
                                     sems.at[s])

    def start_tile(d_ref, s):
        def body(t, carry):
            for k in range(TOP_K):
                copy(d_ref, s, t, k).start(priority=k % 2)
            return carry
        lax.fori_loop(0, tm, body, 0, unroll=ROW_DMA_UNROLL)

    @pl.when(i == 0)
    def _():
        start_tile(dest_ref, 0)

    @pl.when(i + 1 < n)
    def _():
        start_tile(dest_next_ref, 1 - slot)

    def wait(t, carry):
        for k in range(TOP_K):
            copy(dest_ref, slot, t, k).wait()
        return carry

    lax.fori_loop(0, tm, wait, 0, unroll=ROW_DMA_UNROLL)
    gate = gate_ref[...]
    f = gate[:, 0:1] * _natural_rows(buf, tm, (slot, 0))
    for k in range(1, TOP_K):
        f = f + gate[:, k:k + 1] * _natural_rows(buf, tm, (slot, k))
    o_ref[...] = _layer_norm(DEEPNORM_ALPHA * _natural_rows(h_ref, tm) + f, g2_ref[...], b2_ref[...])


def _combine(dest_flat, h, gate, ln2_g, ln2_b, y_rows, tm):
    rows = h.shape[0] // TILE_ROWS
    n_tiles = rows // tm
    return pl.pallas_call(
        functools.partial(_combine_body, tm=tm),
        grid=(n_tiles,),
        in_specs=[pl.BlockSpec((tm * TOP_K,), lambda i: (i,), memory_space=pltpu.SMEM),
                  pl.BlockSpec((tm * TOP_K,), lambda i: (jnp.minimum(i + 1, n_tiles - 1),), memory_space=pltpu.SMEM),
                  pl.BlockSpec((tm * TILE_ROWS, LANES), lambda i: (i, 0)),
                  pl.BlockSpec((tm, LANES), lambda i: (i, 0)),
                  pl.BlockSpec((1, D_MODEL), lambda i: (0, 0)),
                  pl.BlockSpec((1, D_MODEL), lambda i: (0, 0)),
                  pl.BlockSpec(memory_space=pl.ANY)],
        out_specs=pl.BlockSpec((tm, D_MODEL), lambda i: (i, 0)),
        out_shape=jax.ShapeDtypeStruct((rows, D_MODEL), F32),
        scratch_shapes=[pltpu.VMEM((2, TOP_K, tm * TILE_ROWS, LANES), F32), pltpu.SemaphoreType.DMA((2,))],
        compiler_params=_cparams(("arbitrary",)),
        name="moe_combine_ln2",
    )(dest_flat, dest_flat, h, gate, ln2_g, ln2_b, y_rows)


def _t5_bucket_static(dist):
    d = np.maximum(dist, 0)
    d_large = np.maximum(d, REL_MAX_EXACT).astype(np.float32)
    val = (np.log(d_large / np.float32(REL_MAX_EXACT)) / np.float32(math.log(REL_MAX_DISTANCE / REL_MAX_EXACT))
           * np.float32(REL_BUCKETS - REL_MAX_EXACT))
    inner = (d > REL_MAX_EXACT) & (d < REL_MAX_DISTANCE)
    assert not np.any(inner & (np.abs(val - np.round(val)) < 1e-5))
    large = REL_MAX_EXACT + val.astype(np.int32)
    return np.where(d < REL_MAX_EXACT, d, np.minimum(large, REL_BUCKETS - 1)).astype(np.int32)


def _bias_table(rel_bias, dist, head, valid):
    hi = lax.Precision.HIGHEST
    head_rows = jnp.dot(jnp.asarray(np.eye(REL_HEADS, dtype=np.float32)[np.asarray(head)]),
                        rel_bias.astype(F32).T, precision=hi)
    onehot = (jnp.asarray(_t5_bucket_static(np.asarray(dist)))[..., None]
              == jnp.arange(REL_BUCKETS, dtype=I32)).astype(F32)
    tab = jnp.sum(onehot * head_rows[:, None, :], axis=-1)
    return jnp.where(jnp.asarray(np.asarray(valid)), tab, NEG_INF)


_A_SET_HEADS = ((0, 2, 5, 7), (1, 3, 4, 6))


def _tables_a_prompt(rel_bias, sinks):
    qi = np.arange(QBLK)[:, None]
    kj = np.arange(2 * QBLK)[None, :]
    dist = np.tile(QBLK + qi - kj, (4, 1))
    valid = (dist >= 0) & (dist < A_WINDOW)
    bias = jnp.stack([_bias_table(rel_bias, dist, np.repeat(heads, QBLK), valid) for heads in _A_SET_HEADS])
    s = sinks.astype(F32)
    sink = jnp.stack([jnp.repeat(jnp.stack([s[h] for h in heads]), QBLK).reshape(4 * QBLK, 1)
                      for heads in _A_SET_HEADS])
    return bias, sink


def _tables_b_prompt(rel_bias):
    qi = np.arange(QBLK)[:, None]
    kj = np.arange(2 * QBLK)[None, :]
    sub = np.tile(QBLK + qi - kj, (2, 1))
    out = []
    for g, (window, dil) in enumerate(B_GROUPS):
        valid = (sub >= 0) & (sub <= window // dil)
        out.append(jnp.stack([
            _bias_table(rel_bias, sub * dil, np.repeat(A_HEADS + g * B_KV_HEADS + 2 * sp + np.arange(2), QBLK), valid)
            for sp in range(2)]))
    return jnp.stack(out)


def _tables_a_sample(rel_bias, sinks, la):
    n = SAMPLE_ROWS
    rep = A_HEADS // A_KV_HEADS
    qi = np.tile(np.arange(n), rep)[:, None]
    dist_c = la + qi - np.arange(la)[None, :]
    coln = np.arange(LANES)[None, :]
    dist_n = qi - coln
    valid_c = (dist_c >= 0) & (dist_c < A_WINDOW)
    valid_n = (dist_n >= 0) & (dist_n < A_WINDOW) & (coln < n)
    heads = [np.repeat(np.arange(g * rep, (g + 1) * rep), n) for g in range(A_KV_HEADS)]
    bias_c = jnp.stack([_bias_table(rel_bias, dist_c, hd, valid_c) for hd in heads])
    bias_n = jnp.stack([_bias_table(rel_bias, dist_n, hd, valid_n) for hd in heads])
    sink = jnp.repeat(sinks.astype(F32), n).reshape(A_KV_HEADS, rep * n, 1)
    return bias_c, bias_n, sink


def _tables_b_sample(rel_bias, lb):
    n = SAMPLE_ROWS
    n_groups = len(B_GROUPS)
    h = np.repeat(np.arange(B_KV_HEADS), n_groups * n)
    g = np.tile(np.repeat(np.arange(n_groups), n), B_KV_HEADS)
    qi = np.tile(np.arange(n), B_KV_HEADS * n_groups)[:, None]
    window = np.array([w for w, _ in B_GROUPS])[g][:, None]
    dil = np.array([d for _, d in B_GROUPS])[g][:, None]
    coln = np.arange(LANES)[None, :]

    def table(dist, extra):
        valid = (dist >= 0) & (dist <= window) & (dist % dil == 0) & extra
        return _bias_table(rel_bias, dist, A_HEADS + g * B_KV_HEADS + h, valid)

    return table(lb + qi - np.arange(lb)[None, :], True), table(qi - coln, coln < n)


def _mask_mem_sample():
    row_head = np.repeat(np.arange(M_HEADS), SAMPLE_ROWS)[:, None]
    col_head = np.arange(MEM_TOKENS * M_HEADS)[None, :] % M_HEADS
    return jnp.asarray(np.where(row_head == col_head, 0.0, -np.inf).astype(np.float32))


def _swap_halves(w):
    return jnp.concatenate([w[:, HEAD_DIM:], w[:, :HEAD_DIM]], axis=1)


def _slabs_to_rows(slabs):
    return jnp.moveaxis(slabs, 0, 1).reshape(slabs.shape[1], slabs.shape[0] * LANES)


def _heads_to_rows(o):
    return jnp.moveaxis(o, 0, 1).reshape(o.shape[1], o.shape[0] * HEAD_DIM)


def kernel(x_prompt, x_sample, cache_a_k, cache_a_v, cache_b_k, cache_b_v, cache_mem_k, cache_mem_v, mem_prompt,
           rel_bias, sinks_a, w_in, w_mem_kv, w_br_a, w_br_b, w_br_m, w_o, ln1_g, ln1_b, ln2_g, ln2_b, w_router,
           b_router, w_gu, b_gu, w_down, b_down):
    assert w_in.shape[0] == DEPTH == 1
    bsz, seq, _ = x_prompt.shape
    dec_b, dec_n, _ = x_sample.shape
    assert bsz == 1 and dec_n <= SAMPLE_ROWS
    la, lb = cache_a_k.shape[2], cache_b_k.shape[2]
    assert la == A_WINDOW and lb == B_WINDOW_MAX

    w = w_in[0]
    c = np.cumsum((0, 512, 128, 128, 768, 256, 256, 512, 3072))
    w_qa, w_ka, w_va, w_qb, w_kb, w_vb, w_qm, w_g = (w[:, c[i]:c[i + 1]] for i in range(8))
    w_proj = jnp.concatenate([w_qb, w_kb, w_vb, w_qa, w_ka, _swap_halves(w_ka), w_va, _swap_halves(w_va), w_qm],
                             axis=1).astype(BF16)
    wr = jnp.pad(w_router[0].astype(F32), ((0, 0), (0, LANES - N_EXPERTS)))
    wr_hi = wr.astype(BF16)
    merge_w = dict(
        w_g=w_g.astype(BF16), w_a=w_br_a[0].astype(BF16), w_b=w_br_b[0].astype(BF16), w_m=w_br_m[0].astype(BF16),
        w_o=w_o[0].astype(BF16), ln1_g=ln1_g.astype(F32).reshape(1, D_MODEL), ln1_b=ln1_b.astype(F32).reshape(1, D_MODEL),
        wr_hi=wr_hi, wr_lo=(wr - wr_hi.astype(F32)).astype(BF16),
        b_r=jnp.pad(b_router.astype(F32).reshape(1, N_EXPERTS), ((0, 0), (0, LANES - N_EXPERTS)),
                    constant_values=NEG_INF))
    ln2g = ln2_g.astype(F32).reshape(1, D_MODEL)
    ln2b = ln2_b.astype(F32).reshape(1, D_MODEL)

    xp = x_prompt.reshape(seq, D_MODEL)
    qb_p, kvb_p, qa_p, kva_p, qm_p = _project(xp, w_proj, 512)
    mkv_p = _matmul_f32(mem_prompt.reshape(MEM_TOKENS, D_MODEL), w_mem_kv[0].astype(BF16))
    bias_a_p, sink_a_p = _tables_a_prompt(rel_bias, sinks_a[0])
    oa_p = _attn_a_prompt(qa_p, kva_p, bias_a_p, sink_a_p, 512)
    ob_p = _attn_b_prompt(qb_p, kvb_p, _tables_b_prompt(rel_bias))
    om_p = _attn_mem_prompt(qm_p, mkv_p, 512)
    zero_cnt = jnp.zeros((1, LANES), F32)
    n_tok = seq + dec_b * dec_n
    n_rows = n_tok * TOP_K + N_EXPERTS * EXPERT_ROWS
    h_p, route_p, gate_p, cnt_p, x_rows = _merge(zero_cnt, xp, oa_p, ob_p, om_p, merge_w, 512, 1, zero_rows=n_rows)

    xs = jnp.pad(x_sample, ((0, 0), (0, SAMPLE_ROWS - dec_n), (0, 0))).reshape(dec_b * SAMPLE_ROWS, D_MODEL)
    qb_s, kvb_s, qa_s, kva_s, qm_s = _project(xs, w_proj, 512)
    bias_a_c, bias_a_n, sink_a_s = _tables_a_sample(rel_bias, sinks_a[0], la)
    to_hdt = lambda c: jnp.transpose(c[0], (0, 2, 3, 1))
    oa_s = _attn_a_sample(qa_s, kva_s, to_hdt(cache_a_k), to_hdt(cache_a_v), bias_a_c, bias_a_n, sink_a_s, 8)
    ob_s = _attn_b_sample(qb_s, kvb_s, to_hdt(cache_b_k), to_hdt(cache_b_v), *_tables_b_sample(rel_bias, lb), 2)
    om_s = _attn_mem_sample(qm_s, cache_mem_k[0].reshape(dec_b, MEM_TOKENS * M_HEADS, M_HEAD_DIM),
                            cache_mem_v[0].reshape(dec_b, MEM_TOKENS * M_HEADS, M_HEAD_DIM), _mask_mem_sample(), 8)
    unpad = lambda a: a.reshape(dec_b, SAMPLE_ROWS, a.shape[-1])[:, :dec_n].reshape(dec_b * dec_n, a.shape[-1])
    x_s = x_sample.reshape(dec_b * dec_n, D_MODEL)
    h_s, route_s, gate_s, cnt_all = _merge(cnt_p, x_s, unpad(_heads_to_rows(oa_s)), unpad(ob_s),
                                           unpad(om_s), merge_w, dec_b * dec_n, 1)

    counts = cnt_all[0, :N_EXPERTS].astype(I32)
    padded = (counts + EXPERT_ROWS - 1) // EXPERT_ROWS * EXPERT_ROWS
    pad_end = jnp.cumsum(padded)
    pad_start = pad_end - padded
    n_valid = (pad_end[-1] // EXPERT_ROWS).reshape(1).astype(I32)
    block_row = jnp.arange(n_rows // EXPERT_ROWS, dtype=I32) * EXPERT_ROWS
    block_expert = jnp.minimum(jnp.sum((pad_end[None, :] <= block_row[:, None]).astype(I32), axis=1), N_EXPERTS - 1)
    expert_ids = jnp.arange(N_EXPERTS, dtype=I32)

    def dest(route):
        start = jnp.sum(jnp.where(route[:, :TOP_K, None] == expert_ids, pad_start, 0), axis=-1)
        return (start + route[:, TOP_K:2 * TOP_K]).reshape(-1).astype(I32)

    dest_p, dest_s = dest(route_p), dest(route_s)
    x_rows = _dispatch(dest_p, h_p, x_rows, 512)
    x_rows = _dispatch(dest_s, h_s, x_rows, dec_b * dec_n)
    changed = jnp.concatenate([jnp.zeros((1,), I32), (block_expert[1:] != block_expert[:-1]).astype(I32)])
    block_slot = jnp.cumsum(changed) % 2
    later = (expert_ids[None, :] > expert_ids[:, None]) & (counts[None, :] > 0)
    next_nonempty = jnp.min(jnp.where(later, expert_ids[None, :], N_EXPERTS), axis=1)
    next_nonempty = jnp.where(next_nonempty < N_EXPERTS, next_nonempty, -1)
    block_next = jnp.sum(jnp.where(block_expert[:, None] == expert_ids, next_nonempty, 0), axis=1)
    y_rows = _experts(block_expert, n_valid, block_slot.astype(I32), block_next.astype(I32), x_rows, w_gu[0], b_gu[0],
                      w_down[0], b_down[0])
    y_p = _combine(dest_p, h_p, gate_p, ln2g, ln2b, y_rows, 256)
    y_s = _combine(dest_s, h_s, gate_s, ln2g, ln2b, y_rows, 256)

    kva_rows = _slabs_to_rows(kva_p[:, seq - la:])
    kvb_rows = _slabs_to_rows(kvb_p[:, seq - lb:])
    a_k_p = kva_rows[:, 0:128].reshape(1, 1, la, A_KV_HEADS, HEAD_DIM)
    a_v_p = kva_rows[:, 256:384].reshape(1, 1, la, A_KV_HEADS, HEAD_DIM)
    b_k_p = kvb_rows[:, 0:256].reshape(1, 1, lb, B_KV_HEADS, HEAD_DIM)
    b_v_p = kvb_rows[:, 256:512].reshape(1, 1, lb, B_KV_HEADS, HEAD_DIM)
    m_k_p = mkv_p[:, :M_HEADS * LANES].reshape(1, 1, MEM_TOKENS, M_HEADS, M_HEAD_DIM)
    m_v_p = mkv_p[:, M_HEADS * LANES:].reshape(1, 1, MEM_TOKENS, M_HEADS, M_HEAD_DIM)
    kva_s_rows = unpad(_slabs_to_rows(kva_s))
    kvb_s_rows = unpad(_slabs_to_rows(kvb_s))
    a_k_s = kva_s_rows[:, 0:128].reshape(1, dec_b, dec_n, A_KV_HEADS, HEAD_DIM)
    a_v_s = kva_s_rows[:, 256:384].reshape(1, dec_b, dec_n, A_KV_HEADS, HEAD_DIM)
    b_k_s = kvb_s_rows[:, 0:256].reshape(1, dec_b, dec_n, B_KV_HEADS, HEAD_DIM)
    b_v_s = kvb_s_rows[:, 256:512].reshape(1, dec_b, dec_n, B_KV_HEADS, HEAD_DIM)
    return (y_p.reshape(bsz, seq, D_MODEL), y_s.reshape(dec_b, dec_n, D_MODEL),
            a_k_p, a_v_p, b_k_p, b_v_p, m_k_p, m_v_p, a_k_s, a_v_s, b_k_s, b_v_s)
```

```python
import functools
import math

import numpy as np
import jax
import jax.numpy as jnp
from jax import lax
from jax.experimental import pallas as pl
from jax.experimental.pallas import tpu as pltpu

F32 = jnp.float32
BF16 = jnp.bfloat16
I32 = jnp.int32

D_MODEL = 1024
HEAD_DIM = 64
A_HEADS = 8
A_KV_HEADS = 2
A_WINDOW = 128
B_KV_HEADS = 4
B_GROUPS = ((128, 1), (512, 4), (2048, 16))
B_WINDOW_MAX = 2048
MEM_TOKENS = 256
M_HEADS = 4
M_HEAD_DIM = 128
REL_BUCKETS = 32
REL_MAX_EXACT = REL_BUCKETS // 2
REL_MAX_DISTANCE = B_WINDOW_MAX
REL_HEADS = A_HEADS + B_KV_HEADS * len(B_GROUPS)
N_EXPERTS = 32
TOP_K = 4
D_FF = D_MODEL
SWIGLU_LIMIT = 7.0
SWIGLU_ALPHA = 1.702
LN_EPS = 1e-5
DEPTH = 1
DEEPNORM_ALPHA = (2 * DEPTH) ** 0.25

LANES = 128
TILE_ROWS = 8
QBLK = 128
SAMPLE_ROWS = 8
EXPERT_ROWS = 512
VMEM_LIMIT = 56 * 1024 * 1024
NEG_INF = float("-inf")
B_DIL_MAX = B_GROUPS[-1][1]
ROW_DMA_UNROLL = 8
SAMPLE_UNROLL = 4
B_UNIT_UNROLL = 8

PROJ_SLABS = (6, 4, 4, 4, 4)


def _cparams(sem):
    return pltpu.CompilerParams(dimension_semantics=sem, vmem_limit_bytes=VMEM_LIMIT)


def _dot(a, b):
    return jnp.dot(a, b, preferred_element_type=F32)


def _dot_t(a, b):
    return lax.dot_general(a, b, (((1,), (1,)), ((), ())), preferred_element_type=F32)


def _lo_mask(rows):
    return lax.broadcasted_iota(I32, (rows, LANES), 1) < HEAD_DIM


def _proj_body(x_ref, w_ref, *o_refs):
    x = x_ref[...].astype(BF16)
    col = 0
    for o_ref in o_refs:
        n = o_ref.shape[0] * LANES
        acc = _dot(x, w_ref[:, col:col + n])
        for s in range(o_ref.shape[0]):
            o_ref[s] = acc[:, s * LANES:(s + 1) * LANES]
        col += n


def _project(x, w_bf, tm):
    rows = x.shape[0]
    n_cols = w_bf.shape[1]
    assert rows % tm == 0 and n_cols == sum(PROJ_SLABS) * LANES
    return pl.pallas_call(
        _proj_body,
        grid=(rows // tm,),
        in_specs=[pl.BlockSpec((tm, D_MODEL), lambda i: (i, 0)),
                  pl.BlockSpec((D_MODEL, n_cols), lambda i: (0, 0))],
        out_specs=[pl.BlockSpec((n, tm, LANES), lambda i: (0, i, 0)) for n in PROJ_SLABS],
        out_shape=[jax.ShapeDtypeStruct((n, rows, LANES), F32) for n in PROJ_SLABS],
        compiler_params=_cparams(("parallel",)),
        name="in_proj",
    )(x, w_bf)


def _matmul_body(x_ref, w_ref, o_ref):
    o_ref[...] = _dot(x_ref[...].astype(BF16), w_ref[...])


def _matmul_f32(x, w_bf):
    return pl.pallas_call(
        _matmul_body,
        out_shape=jax.ShapeDtypeStruct((x.shape[0], w_bf.shape[1]), F32),
        compiler_params=_cparams(None),
        name="mem_kv_proj",
    )(x, w_bf)


def _softmax_lse(s):
    m = jnp.max(s, axis=-1, keepdims=True)
    p = jnp.exp(s - m)
    l = jnp.sum(p, axis=-1, keepdims=True)
    return p / l, m + jnp.log(l)


def _softmax_parts(parts, sink=None):
    m = functools.reduce(jnp.maximum, [jnp.max(s, axis=-1, keepdims=True) for s in parts])
    if sink is not None:
        m = jnp.maximum(m, sink)
    ps = [jnp.exp(s - m) for s in parts]
    l = functools.reduce(jnp.add, [jnp.sum(p, axis=-1, keepdims=True) for p in ps])
    lse = m + jnp.log(l)
    if sink is not None:
        l = l + jnp.exp(sink - m)
    return [p / l for p in ps], lse


def _attn_a_prompt_body(q_ref, kvc_ref, kvp_ref, bias_ref, sink_ref, o_ref, *, nsub):
    first = pl.program_id(0) == 0
    lo = _lo_mask(QBLK)
    col = lax.broadcasted_iota(I32, (4 * QBLK, 2 * QBLK), 1)
    ones = jnp.ones((2 * QBLK, LANES), BF16)
    for j in range(nsub):
        rows = slice(j * QBLK, (j + 1) * QBLK)

        def keys(slab):
            cur = kvc_ref[slab, rows, :]
            prev = kvp_ref[slab] if j == 0 else kvc_ref[slab, (j - 1) * QBLK:j * QBLK, :]
            return jnp.concatenate([prev, cur], axis=0).astype(BF16)

        q = [q_ref[p, rows, :] for p in range(4)]
        zero = jnp.zeros_like(q[0])
        q_sets = (
            jnp.concatenate([jnp.where(lo, q[0], zero), jnp.where(lo, q[1], zero),
                             jnp.where(lo, zero, q[2]), jnp.where(lo, zero, q[3])], axis=0).astype(BF16),
            jnp.concatenate([jnp.where(lo, zero, q[0]), jnp.where(lo, zero, q[1]),
                             jnp.where(lo, q[2], zero), jnp.where(lo, q[3], zero)], axis=0).astype(BF16),
        )
        outs = []
        for st in range(2):
            s = _dot_t(q_sets[st], keys(st)) * (HEAD_DIM ** -0.5) + bias_ref[st]
            if j == 0:
                s = jnp.where(jnp.logical_and(first, col < QBLK), NEG_INF, s)
            sink = sink_ref[st]
            m = jnp.maximum(jnp.max(s, axis=-1, keepdims=True), sink)
            ol = _dot(jnp.exp(s - m).astype(BF16), jnp.concatenate([keys(2 + st), ones], axis=1))
            outs.append(ol[:, :LANES] / (ol[:, LANES:] + jnp.exp(sink - m)))
        o1, o2 = outs
        blk = lambda o, r: o[r * QBLK:(r + 1) * QBLK]
        slabs = (jnp.where(lo, blk(o1, 0), blk(o2, 0)), jnp.where(lo, blk(o1, 1), blk(o2, 1)),
                 jnp.where(lo, blk(o2, 2), blk(o1, 2)), jnp.where(lo, blk(o2, 3), blk(o1, 3)))
        for p in range(4):
            o_ref[rows, p * LANES:(p + 1) * LANES] = slabs[p].astype(o_ref.dtype)


def _attn_a_prompt(qa, kva, bias, sink, tq):
    seq = qa.shape[1]
    nsub = tq // QBLK
    return pl.pallas_call(
        functools.partial(_attn_a_prompt_body, nsub=nsub),
        grid=(seq // tq,),
        in_specs=[pl.BlockSpec((4, tq, LANES), lambda n: (0, n, 0)),
                  pl.BlockSpec((4, tq, LANES), lambda n: (0, n, 0)),
                  pl.BlockSpec((4, QBLK, LANES), lambda n: (0, jnp.maximum(n * nsub - 1, 0), 0)),
                  pl.BlockSpec((2, 4 * QBLK, 2 * QBLK), lambda n: (0, 0, 0)),
                  pl.BlockSpec((2, 4 * QBLK, 1), lambda n: (0, 0, 0))],
        out_specs=pl.BlockSpec((tq, 4 * LANES), lambda n: (n, 0)),
        out_shape=jax.ShapeDtypeStruct((seq, 4 * LANES), BF16),
        compiler_params=_cparams(("parallel",)),
        name="attn_a_prompt",
    )(qa, kva, kva, bias, sink)


def _attn_b_prompt_body(q_ref, kvc_ref, kvp_ref, bias_ref, o_ref, kv_buf, o_buf, l_buf, *, sb):
    first = pl.program_id(0) == 0
    kv_buf[:, :sb, :] = kvp_ref[...]
    kv_buf[:, sb:, :] = kvc_ref[...]
    lo = _lo_mask(QBLK)
    col = lax.broadcasted_iota(I32, (2 * QBLK, 2 * QBLK), 1)
    ones = jnp.ones((2 * QBLK, LANES), BF16)
    units = sb // QBLK
    for g, (_, dil) in enumerate(B_GROUPS):
        blocks_per_class = units // dil

        def unit(u, carry, g=g, dil=dil, blocks_per_class=blocks_per_class):
            r = u // blocks_per_class
            m = u % blocks_per_class
            q_start = r + dil * QBLK * m
            k_start = sb - dil * QBLK + q_start
            if dil == 1:
                q_start = pl.multiple_of(q_start, QBLK)
                k_start = pl.multiple_of(k_start, QBLK)
                q_idx = pl.ds(q_start, QBLK)
                k_idx = pl.ds(k_start, 2 * QBLK)
            else:
                q_idx = pl.ds(q_start, QBLK, stride=dil)
                k_idx = pl.ds(k_start, 2 * QBLK, stride=dil)
            for sp in range(2):
                q = q_ref[2 * g + sp, q_idx, :]
                zero = jnp.zeros_like(q)
                q2 = jnp.concatenate([jnp.where(lo, q, zero), jnp.where(lo, zero, q)], axis=0).astype(BF16)
                k = kv_buf[sp, k_idx, :].astype(BF16)
                v = kv_buf[2 + sp, k_idx, :].astype(BF16)
                s = _dot_t(q2, k) * (HEAD_DIM ** -0.5) + bias_ref[g, sp]
                s = jnp.where(jnp.logical_and(jnp.logical_and(first, m == 0), col < QBLK), NEG_INF, s)
                m = jnp.max(s, axis=-1, keepdims=True)
                ol = _dot(jnp.exp(s - m).astype(BF16), jnp.concatenate([v, ones], axis=1))
                o = ol[:, :LANES] / ol[:, LANES:]
                lse = m + jnp.log(ol[:, LANES:])
                o_buf[g, sp, q_idx, :] = jnp.where(lo, o[:QBLK], o[QBLK:])
                l_buf[g, sp, q_idx, :] = jnp.where(lo, lse[:QBLK], lse[QBLK:])
            return carry

        lax.fori_loop(0, units, unit, 0, unroll=B_UNIT_UNROLL)
    for sp in range(2):
        l = [l_buf[g, sp] for g in range(3)]
        mx = jnp.maximum(jnp.maximum(l[0], l[1]), l[2])
        e = [jnp.exp(x - mx) for x in l]
        den = e[0] + e[1] + e[2]
        acc = (e[0] / den) * o_buf[0, sp] + (e[1] / den) * o_buf[1, sp] + (e[2] / den) * o_buf[2, sp]
        o_ref[:, sp * LANES:(sp + 1) * LANES] = acc.astype(o_ref.dtype)


def _attn_b_prompt(qb, kvb, bias):
    seq = qb.shape[1]
    sb = B_DIL_MAX * QBLK
    assert seq % sb == 0
    return pl.pallas_call(
        functools.partial(_attn_b_prompt_body, sb=sb),
        grid=(seq // sb,),
        in_specs=[pl.BlockSpec((6, sb, LANES), lambda n: (0, n, 0)),
                  pl.BlockSpec((4, sb, LANES), lambda n: (0, n, 0)),
                  pl.BlockSpec((4, sb, LANES), lambda n: (0, jnp.maximum(n - 1, 0), 0)),
                  pl.BlockSpec((3, 2, 2 * QBLK, 2 * QBLK), lambda n: (0, 0, 0, 0))],
        out_specs=pl.BlockSpec((sb, 2 * LANES), lambda n: (n, 0)),
        out_shape=jax.ShapeDtypeStruct((seq, 2 * LANES), BF16),
        scratch_shapes=[pltpu.VMEM((4, 2 * sb, LANES), F32),
                        pltpu.VMEM((3, 2, sb, LANES), F32),
                        pltpu.VMEM((3, 2, sb, LANES), F32)],
        compiler_params=_cparams(("parallel",)),
        name="attn_b_prompt",
    )(qb, kvb, kvb, bias)


def _attn_mem_prompt_body(q_ref, mkv_ref, o_ref):
    ones = jnp.ones((MEM_TOKENS, LANES), BF16)
    for h in range(M_HEADS):
        k = mkv_ref[:, h * LANES:(h + 1) * LANES].astype(BF16)
        v = mkv_ref[:, (M_HEADS + h) * LANES:(M_HEADS + h + 1) * LANES].astype(BF16)
        s = _dot_t(q_ref[h].astype(BF16), k) * (M_HEAD_DIM ** -0.5)
        m = jnp.max(s, axis=-1, keepdims=True)
        ol = _dot(jnp.exp(s - m).astype(BF16), jnp.concatenate([v, ones], axis=1))
        o_ref[:, h * LANES:(h + 1) * LANES] = (ol[:, :LANES] / ol[:, LANES:]).astype(o_ref.dtype)


def _attn_mem_prompt(qm, mkv, tq):
    seq = qm.shape[1]
    return pl.pallas_call(
        _attn_mem_prompt_body,
        grid=(seq // tq,),
        in_specs=[pl.BlockSpec((M_HEADS, tq, LANES), lambda n: (0, n, 0)),
                  pl.BlockSpec((MEM_TOKENS, 2 * M_HEADS * LANES), lambda n: (0, 0))],
        out_specs=pl.BlockSpec((tq, M_HEADS * LANES), lambda n: (n, 0)),
        out_shape=jax.ShapeDtypeStruct((seq, M_HEADS * LANES), BF16),
        compiler_params=_cparams(("parallel",)),
        name="attn_mem_prompt",
    )(qm, mkv)


def _attn_mem_sample_body(q_ref, mk_ref, mv_ref, mask_ref, o_ref, *, bb):
    def one(b, carry):
        rows = pl.ds(pl.multiple_of(b * SAMPLE_ROWS, SAMPLE_ROWS), SAMPLE_ROWS)
        q = jnp.concatenate([q_ref[h, rows, :] for h in range(M_HEADS)], axis=0).astype(BF16)
        s = _dot_t(q, mk_ref[b].astype(BF16)) * (M_HEAD_DIM ** -0.5) + mask_ref[...]
        p, _ = _softmax_lse(s)
        o = _dot(p.astype(BF16), mv_ref[b].astype(BF16))
        for h in range(M_HEADS):
            o_ref[rows, h * LANES:(h + 1) * LANES] = o[h * SAMPLE_ROWS:(h + 1) * SAMPLE_ROWS].astype(o_ref.dtype)
        return carry

    lax.fori_loop(0, bb, one, 0, unroll=SAMPLE_UNROLL)


def _attn_mem_sample(qm, mk, mv, mask, bb):
    batch, n_rows = mk.shape[0], mk.shape[1]
    return pl.pallas_call(
        functools.partial(_attn_mem_sample_body, bb=bb),
        grid=(batch // bb,),
        in_specs=[pl.BlockSpec((M_HEADS, bb * SAMPLE_ROWS, LANES), lambda n: (0, n, 0)),
                  pl.BlockSpec((bb, n_rows, LANES), lambda n: (n, 0, 0)),
                  pl.BlockSpec((bb, n_rows, LANES), lambda n: (n, 0, 0)),
                  pl.BlockSpec(mask.shape, lambda n: (0, 0))],
        out_specs=pl.BlockSpec((bb * SAMPLE_ROWS, M_HEADS * LANES), lambda n: (n, 0)),
        out_shape=jax.ShapeDtypeStruct((batch * SAMPLE_ROWS, M_HEADS * LANES), BF16),
        compiler_params=_cparams(("parallel",)),
        name="attn_mem_sample",
    )(qm, mk, mv, mask)


def _head_rows(ref, slab, rows, half):
    x = ref[slab, rows, :]
    return x[:, half * HEAD_DIM:(half + 1) * HEAD_DIM]


def _pad_rows(x, n):
    return jnp.concatenate([x, jnp.zeros((n - x.shape[0], x.shape[1]), x.dtype)], axis=0)


def _attn_a_sample_body(q_ref, kvn_ref, ck_ref, cv_ref, bias_ref, sink_ref, o_ref, *, bb):
    rep = A_HEADS // A_KV_HEADS

    def one(b, carry):
        rows = pl.ds(pl.multiple_of(b * SAMPLE_ROWS, SAMPLE_ROWS), SAMPLE_ROWS)
        for g in range(A_KV_HEADS):
            heads = range(g * rep, (g + 1) * rep)
            q = jnp.concatenate([_head_rows(q_ref, h // 2, rows, h % 2) for h in heads], axis=0).astype(BF16)
            kn = _pad_rows(_head_rows(kvn_ref, 0, rows, g), LANES).astype(BF16)
            vn = _pad_rows(_head_rows(kvn_ref, 2, rows, g), LANES).astype(BF16)
            s_c = _dot(q, ck_ref[b, g].astype(BF16)) * (HEAD_DIM ** -0.5) + bias_ref[0, g]
            s_n = _dot_t(q, kn) * (HEAD_DIM ** -0.5) + bias_ref[1, g]
            (p_c, p_n), _ = _softmax_parts([s_c, s_n], sink_ref[g])
            o = _dot_t(p_c.astype(BF16), cv_ref[b, g].astype(BF16)) + _dot(p_n.astype(BF16), vn)
            for r, h in enumerate(heads):
                o_ref[h, rows, :] = o[r * SAMPLE_ROWS:(r + 1) * SAMPLE_ROWS].astype(o_ref.dtype)
        return carry

    lax.fori_loop(0, bb, one, 0, unroll=SAMPLE_UNROLL)


def _attn_a_sample(qa, kva, ck, cv, bias, sink, bb):
    batch = ck.shape[0]
    full = lambda a: pl.BlockSpec(a.shape, lambda n: (0,) * a.ndim)
    buf_spec = pl.BlockSpec((bb,) + ck.shape[1:], lambda n: (n, 0, 0, 0))
    return pl.pallas_call(
        functools.partial(_attn_a_sample_body, bb=bb),
        grid=(batch // bb,),
        in_specs=[pl.BlockSpec((4, bb * SAMPLE_ROWS, LANES), lambda n: (0, n, 0)),
                  pl.BlockSpec((4, bb * SAMPLE_ROWS, LANES), lambda n: (0, n, 0)),
                  buf_spec, buf_spec, full(bias), full(sink)],
        out_specs=pl.BlockSpec((A_HEADS, bb * SAMPLE_ROWS, HEAD_DIM), lambda n: (0, n, 0)),
        out_shape=jax.ShapeDtypeStruct((A_HEADS, batch * SAMPLE_ROWS, HEAD_DIM), BF16),
        compiler_params=_cparams(("parallel",)),
        name="attn_a_sample",
    )(qa, kva, ck, cv, bias, sink)


def _attn_b_sample_body(q_ref, kvn_ref, ck_ref, cv_ref, bias_c_ref, bias_n_ref, o_ref, *, bb):
    n = SAMPLE_ROWS
    n_groups = len(B_GROUPS)
    lo = _lo_mask(n)
    scale = HEAD_DIM ** -0.5
    zero = jnp.zeros((n, LANES), F32)
    for b in range(bb):
        rows = slice(b * n, (b + 1) * n)
        q_rows = []
        for h in range(B_KV_HEADS):
            for g in range(n_groups):
                x = q_ref[2 * g + h // 2, rows, :]
                x = jnp.where(lo, x, zero) if h % 2 == 0 else jnp.where(lo, zero, x)
                q_rows.append(jnp.concatenate([x, zero] if h // 2 == 0 else [zero, x], axis=1))
        q = jnp.concatenate(q_rows, axis=0).astype(BF16)
        kn = _pad_rows(jnp.concatenate([kvn_ref[0, rows, :], kvn_ref[1, rows, :]], axis=1), LANES).astype(BF16)
        vn = _pad_rows(jnp.concatenate([kvn_ref[2, rows, :], kvn_ref[3, rows, :]], axis=1), LANES).astype(BF16)
        kt = ck_ref[b].reshape(B_KV_HEADS * HEAD_DIM, ck_ref.shape[-1]).astype(BF16)
        vt = cv_ref[b].reshape(B_KV_HEADS * HEAD_DIM, cv_ref.shape[-1]).astype(BF16)
        s_c = _dot(q, kt) * scale + bias_c_ref[...]
        s_n = _dot_t(q, kn) * scale + bias_n_ref[...]
        (p_c, p_n), lse = _softmax_parts([s_c, s_n])
        o = _dot_t(p_c.astype(BF16), vt) + _dot(p_n.astype(BF16), vn)
        merged = []
        for h in range(B_KV_HEADS):
            cols = slice((h // 2) * LANES, (h // 2 + 1) * LANES)
            blk = lambda g: slice((n_groups * h + g) * n, (n_groups * h + g + 1) * n)
            l_g = [lse[blk(g)] for g in range(n_groups)]
            mx = jnp.maximum(jnp.maximum(l_g[0], l_g[1]), l_g[2])
            e = [jnp.exp(x - mx) for x in l_g]
            den = e[0] + e[1] + e[2]
            merged.append((e[0] / den) * o[blk(0), cols] + (e[1] / den) * o[blk(1), cols]
                          + (e[2] / den) * o[blk(2), cols])
        for cg in range(2):
            o_ref[rows, cg * LANES:(cg + 1) * LANES] = jnp.where(lo, merged[2 * cg], merged[2 * cg + 1]).astype(o_ref.dtype)


def _attn_b_sample(qb, kvb, ck, cv, bias_c, bias_n, bb):
    batch = ck.shape[0]
    full = lambda a: pl.BlockSpec(a.shape, lambda n: (0,) * a.ndim)
    buf_spec = pl.BlockSpec((bb,) + ck.shape[1:], lambda n: (n, 0, 0, 0))
    return pl.pallas_call(
        functools.partial(_attn_b_sample_body, bb=bb),
        grid=(batch // bb,),
        in_specs=[pl.BlockSpec((6, bb * SAMPLE_ROWS, LANES), lambda n: (0, n, 0)),
                  pl.BlockSpec((4, bb * SAMPLE_ROWS, LANES), lambda n: (0, n, 0)),
                  buf_spec, buf_spec, full(bias_c), full(bias_n)],
        out_specs=pl.BlockSpec((bb * SAMPLE_ROWS, B_KV_HEADS * HEAD_DIM), lambda n: (n, 0)),
        out_shape=jax.ShapeDtypeStruct((batch * SAMPLE_ROWS, B_KV_HEADS * HEAD_DIM), BF16),
        compiler_params=_cparams(("parallel",)),
        name="attn_b_sample",
    )(qb, kvb, ck, cv, bias_c, bias_n)


def _layer_norm(x, g, b):
    mu = jnp.mean(x, axis=-1, keepdims=True)
    xc = x - mu
    var = jnp.mean(xc * xc, axis=-1, keepdims=True)
    return xc * lax.rsqrt(var + LN_EPS) * g + b


def _split_bf16(x):
    hi = x.astype(BF16)
    return hi, (x - hi.astype(F32)).astype(BF16)


def _merge_body(cnt_in_ref, x_ref, oa_ref, ob_ref, om_ref, wg_ref, wa_ref, wb_ref, wm_ref, wo_ref, g1_ref, b1_ref,
                wr_hi_ref, wr_lo_ref, br_ref, h_ref, route_ref, gate_ref, cnt_ref, *zero_fill, tm, n_sub, zero_blocks,
                per_step):
    i = pl.program_id(0)
    if zero_blocks:
        rows_ref, zbuf, zsem = zero_fill
        blk = EXPERT_ROWS * TILE_ROWS

        @pl.when(i == 0)
        def _():
            zbuf[...] = jnp.zeros_like(zbuf)

        def zero_copy(j):
            start = pl.multiple_of((i * per_step + j) * blk, blk)
            return pltpu.make_async_copy(zbuf, rows_ref.at[pl.ds(start, blk), :], zsem)

        def zero_each(fn):
            def body(j, carry):
                @pl.when(i * per_step + j < zero_blocks)
                def _():
                    fn(zero_copy(j))
                return carry
            lax.fori_loop(0, per_step, body, 0)

        zero_each(lambda c: c.start())
    @pl.when(i == 0)
    def _():
        cnt_ref[...] = cnt_in_ref[...]

    lane = lax.broadcasted_iota(I32, (tm, LANES), 1)
    lane_f = lane.astype(F32)
    row = lax.broadcasted_iota(I32, (tm, tm), 0)
    colm = lax.broadcasted_iota(I32, (tm, tm), 1)
    tri = jnp.where(colm < row, 1.0, 0.0).astype(BF16)
    carry = cnt_ref[...]
    for r in range(n_sub):
        rs = slice(r * tm, (r + 1) * tm)
        x = x_ref[rs, :]
        gates = jax.nn.sigmoid(_dot(x.astype(BF16), wg_ref[...]))
        u = (gates[:, :D_MODEL] * _dot(oa_ref[rs, :], wa_ref[...])
             + gates[:, D_MODEL:2 * D_MODEL] * _dot(ob_ref[rs, :], wb_ref[...])
             + gates[:, 2 * D_MODEL:] * _dot(om_ref[rs, :], wm_ref[...]))
        mixed = _dot(u.astype(BF16), wo_ref[...])
        h = _layer_norm(DEEPNORM_ALPHA * x + mixed, g1_ref[...], b1_ref[...])
        for c in range(TILE_ROWS):
            h_ref[pl.ds(r * tm * TILE_ROWS + c, tm, stride=TILE_ROWS), :] = h[:, c * LANES:(c + 1) * LANES]

        h_hi, h_lo = _split_bf16(h)
        logits = (_dot(h_hi, wr_hi_ref[...]) + (_dot(h_hi, wr_lo_ref[...]) + _dot(h_lo, wr_hi_ref[...]))
                  + br_ref[...])
        work = logits
        vals, sels, idxs = [], [], []
        for _ in range(TOP_K):
            mx = jnp.max(work, axis=-1, keepdims=True)
            idx = jnp.min(jnp.where(work == mx, lane_f, float(LANES)), axis=-1, keepdims=True)
            sel = lane_f == idx
            vals.append(mx)
            idxs.append(idx)
            sels.append(sel)
            work = jnp.where(sel, NEG_INF, work)
        ex = [jnp.exp(v - vals[0]) for v in vals]
        den = ex[0] + ex[1] + ex[2] + ex[3]
        chosen = jnp.logical_or(jnp.logical_or(sels[0], sels[1]), jnp.logical_or(sels[2], sels[3]))
        onehot = jnp.where(chosen, 1.0, 0.0)
        before = _dot(tri, onehot.astype(BF16)) + carry
        route = jnp.zeros((tm, LANES), F32)
        gate = jnp.zeros((tm, LANES), F32)
        for k in range(TOP_K):
            rank = jnp.sum(jnp.where(sels[k], before, 0.0), axis=-1, keepdims=True)
            route = jnp.where(lane == k, idxs[k], route)
            route = jnp.where(lane == TOP_K + k, rank, route)
            gate = jnp.where(lane == k, ex[k] / den, gate)
        route_ref[rs, :] = route.astype(I32)
        gate_ref[rs, :] = gate
        carry = carry + jnp.sum(onehot, axis=0, keepdims=True)
    cnt_ref[...] = carry
    if zero_blocks:
        zero_each(lambda c: c.wait())


def _merge(cnt_in, x, oa, ob, om, w, tm, n_sub, zero_rows=0):
    rows = x.shape[0]
    step_rows = tm * n_sub
    assert zero_rows % EXPERT_ROWS == 0
    zero_blocks = zero_rows // EXPERT_ROWS
    extra_specs = [pl.BlockSpec(memory_space=pl.ANY)] if zero_blocks else []
    extra_shapes = [jax.ShapeDtypeStruct((zero_rows * TILE_ROWS, LANES), F32)] if zero_blocks else []
    scratch = ([pltpu.VMEM((EXPERT_ROWS * TILE_ROWS, LANES), F32), pltpu.SemaphoreType.DMA(())]
               if zero_blocks else [])
    full = lambda a: pl.BlockSpec(a.shape, lambda i: (0,) * a.ndim)
    row_blk = lambda c: pl.BlockSpec((step_rows, c), lambda i: (i, 0))
    weights = (w["w_g"], w["w_a"], w["w_b"], w["w_m"], w["w_o"], w["ln1_g"], w["ln1_b"],
               w["wr_hi"], w["wr_lo"], w["b_r"])
    return pl.pallas_call(
        functools.partial(_merge_body, tm=tm, n_sub=n_sub, zero_blocks=zero_blocks,
                          per_step=-(-zero_blocks // (rows // step_rows))),
        grid=(rows // step_rows,),
        in_specs=[full(cnt_in), row_blk(D_MODEL), row_blk(oa.shape[1]), row_blk(ob.shape[1]), row_blk(om.shape[1])]
        + [full(a) for a in weights],
        out_specs=[pl.BlockSpec((step_rows * TILE_ROWS, LANES), lambda i: (i, 0)), row_blk(LANES), row_blk(LANES),
                   pl.BlockSpec((1, LANES), lambda i: (0, 0))] + extra_specs,
        out_shape=[jax.ShapeDtypeStruct((rows * TILE_ROWS, LANES), F32), jax.ShapeDtypeStruct((rows, LANES), I32),
                   jax.ShapeDtypeStruct((rows, LANES), F32), jax.ShapeDtypeStruct((1, LANES), F32)] + extra_shapes,
        scratch_shapes=scratch,
        compiler_params=_cparams(("arbitrary",)),
        name="merge_ln_router",
    )(cnt_in, x, oa, ob, om, *weights)


def _token_tile(ref, t):
    return ref.at[pl.ds(pl.multiple_of(t * TILE_ROWS, TILE_ROWS), TILE_ROWS), :]


def _natural_rows(ref, n, lead=()):
    return jnp.concatenate([ref[lead + (pl.ds(c, n, stride=TILE_ROWS), slice(None))] for c in range(TILE_ROWS)], axis=1)


def _dispatch_body(dest_ref, h_ref, xin_ref, xout_ref, sem, *, tm):
    del xin_ref

    def copy(t, k):
        return pltpu.make_async_copy(_token_tile(h_ref, t), _token_tile(xout_ref, dest_ref[t * TOP_K + k]), sem)

    def start(t, carry):
        for k in range(TOP_K):
            copy(t, k).start(priority=k % 2)
        return carry

    def wait(t, carry):
        for k in range(TOP_K):
            copy(t, k).wait()
        return carry

    lax.fori_loop(0, tm, start, 0, unroll=ROW_DMA_UNROLL)
    lax.fori_loop(0, tm, wait, 0, unroll=ROW_DMA_UNROLL)


def _dispatch(dest_flat, h, x_rows, tm):
    rows = h.shape[0] // TILE_ROWS
    return pl.pallas_call(
        functools.partial(_dispatch_body, tm=tm),
        grid=(rows // tm,),
        in_specs=[pl.BlockSpec((tm * TOP_K,), lambda i: (i,), memory_space=pltpu.SMEM),
                  pl.BlockSpec((tm * TILE_ROWS, LANES), lambda i: (i, 0)),
                  pl.BlockSpec(memory_space=pl.ANY)],
        out_specs=pl.BlockSpec(memory_space=pl.ANY),
        out_shape=jax.ShapeDtypeStruct(x_rows.shape, x_rows.dtype),
        scratch_shapes=[pltpu.SemaphoreType.DMA(())],
        input_output_aliases={2: 0},
        compiler_params=_cparams(("arbitrary",)),
        name="moe_dispatch",
    )(dest_flat, h, x_rows)


def _expert_body(be_ref, nv_ref, slot_ref, next_ref, x_ref, wgu_hbm, bgu_ref, wd_hbm, bd_ref, y_ref,
                 wgu_f32, wd_f32, wgu_bf, wd_bf, sems):
    i = pl.program_id(0)
    valid = i < nv_ref[0]

    def weight_copies(e, slot):
        return (pltpu.make_async_copy(wgu_hbm.at[e], wgu_f32.at[slot], sems.at[0, slot]),
                pltpu.make_async_copy(wd_hbm.at[e], wd_f32.at[slot], sems.at[1, slot]))

    @pl.when(valid)
    def _():
        e = be_ref[i]
        slot = slot_ref[i]
        new_expert = jnp.logical_or(i == 0, e != be_ref[jnp.maximum(i - 1, 0)])

        @pl.when(i == 0)
        def _():
            for c in weight_copies(e, slot):
                c.start()

        @pl.when(new_expert)
        def _():
            for c in weight_copies(e, slot):
                c.wait()

            @pl.when(next_ref[i] >= 0)
            def _():
                for c in weight_copies(next_ref[i], 1 - slot):
                    c.start()

            wgu_bf[...] = wgu_f32[slot].astype(BF16)
            wd_bf[...] = wd_f32[slot].astype(BF16)

        gu = _dot(_natural_rows(x_ref, EXPERT_ROWS).astype(BF16), wgu_bf[...]) + bgu_ref[0]
        g = jnp.minimum(gu[:, :D_FF], SWIGLU_LIMIT)
        u = jnp.clip(gu[:, D_FF:], -SWIGLU_LIMIT, SWIGLU_LIMIT)
        hidden = (u + 1.0) * g * jax.nn.sigmoid(SWIGLU_ALPHA * g)
        y = _dot(hidden.astype(BF16), wd_bf[...]) + bd_ref[0]
        for c in range(TILE_ROWS):
            y_ref[pl.ds(c, EXPERT_ROWS, stride=TILE_ROWS), :] = y[:, c * LANES:(c + 1) * LANES]

    @pl.when(jnp.logical_not(valid))
    def _():
        y_ref[...] = jnp.zeros_like(y_ref)


def _experts(block_expert, n_valid, block_slot, block_next, x_rows, w_gu, b_gu, w_down, b_down):
    blk_rows = EXPERT_ROWS * TILE_ROWS
    n_blocks = x_rows.shape[0] // blk_rows
    last = lambda i, nv: jnp.minimum(i, nv[0] - 1)
    grid_spec = pltpu.PrefetchScalarGridSpec(
        num_scalar_prefetch=4,
        grid=(n_blocks,),
        in_specs=[pl.BlockSpec((blk_rows, LANES), lambda i, be, nv, sl, nx: (last(i, nv), 0)),
                  pl.BlockSpec(memory_space=pl.ANY),
                  pl.BlockSpec((1, 1, 2 * D_FF), lambda i, be, nv, sl, nx: (be[last(i, nv)], 0, 0)),
                  pl.BlockSpec(memory_space=pl.ANY),
                  pl.BlockSpec((1, 1, D_MODEL), lambda i, be, nv, sl, nx: (be[last(i, nv)], 0, 0))],
        out_specs=pl.BlockSpec((blk_rows, LANES), lambda i, be, nv, sl, nx: (i, 0)),
        scratch_shapes=[pltpu.VMEM((2, D_MODEL, 2 * D_FF), F32), pltpu.VMEM((2, D_FF, D_MODEL), F32),
                        pltpu.VMEM((D_MODEL, 2 * D_FF), BF16), pltpu.VMEM((D_FF, D_MODEL), BF16),
                        pltpu.SemaphoreType.DMA((2, 2))],
    )
    return pl.pallas_call(
        _expert_body,
        grid_spec=grid_spec,
        out_shape=jax.ShapeDtypeStruct(x_rows.shape, F32),
        compiler_params=_cparams(("arbitrary",)),
        name="moe_experts",
    )(block_expert, n_valid, block_slot, block_next, x_rows, w_gu, b_gu.reshape(N_EXPERTS, 1, 2 * D_FF), w_down,
      b_down.reshape(N_EXPERTS, 1, D_MODEL))


def _combine_body(dest_ref, dest_next_ref, h_ref, gate_ref, g2_ref, b2_ref, yrows_ref, o_ref, buf, sems, *, tm):
    i = pl.program_id(0)
    n = pl.num_programs(0)
    slot = lax.rem(i, 2)

    def copy(d_ref, s, t, k):
        return pltpu.make_async_copy(_token_tile(yrows_ref, d_ref[t * TOP_K + k]), _token_tile(buf.at[s, k], t),
                                     sems.at[s])

    def start_tile(d_ref, s):
        def body(t, carry):
            for k in range(TOP_K):
                copy(d_ref, s, t, k).start(priority=k % 2)
            return carry
        lax.fori_loop(0, tm, body, 0, unroll=ROW_DMA_UNROLL)

    @pl.when(i == 0)
    def _():
        start_tile(dest_ref, 0)

    @pl.when(i + 1 < n)
    def _():
        start_tile(dest_next_ref, 1 - slot)

    def wait(t, carry):
        for k in range(TOP_K):
            copy(dest_ref, slot, t, k).wait()
        return carry

    lax.fori_loop(0, tm, wait, 0, unroll=ROW_DMA_UNROLL)
    gate = gate_ref[...]
    f = gate[:, 0:1] * _natural_rows(buf, tm, (slot, 0))
    for k in range(1, TOP_K):
        f = f + gate[:, k:k + 1] * _natural_rows(buf, tm, (slot, k))
    o_ref[...] = _layer_norm(DEEPNORM_ALPHA * _natural_rows(h_ref, tm) + f, g2_ref[...], b2_ref[...])


def _combine(dest_flat, h, gate, ln2_g, ln2_b, y_rows, tm):
    rows = h.shape[0] // TILE_ROWS
    n_tiles = rows // tm
    return pl.pallas_call(
        functools.partial(_combine_body, tm=tm),
        grid=(n_tiles,),
        in_specs=[pl.BlockSpec((tm * TOP_K,), lambda i: (i,), memory_space=pltpu.SMEM),
                  pl.BlockSpec((tm * TOP_K,), lambda i: (jnp.minimum(i + 1, n_tiles - 1),), memory_space=pltpu.SMEM),
                  pl.BlockSpec((tm * TILE_ROWS, LANES), lambda i: (i, 0)),
                  pl.BlockSpec((tm, LANES), lambda i: (i, 0)),
                  pl.BlockSpec((1, D_MODEL), lambda i: (0, 0)),
                  pl.BlockSpec((1, D_MODEL), lambda i: (0, 0)),
                  pl.BlockSpec(memory_space=pl.ANY)],
        out_specs=pl.BlockSpec((tm, D_MODEL), lambda i: (i, 0)),
        out_shape=jax.ShapeDtypeStruct((rows, D_MODEL), F32),
        scratch_shapes=[pltpu.VMEM((2, TOP_K, tm * TILE_ROWS, LANES), F32), pltpu.SemaphoreType.DMA((2,))],
        compiler_params=_cparams(("arbitrary",)),
        name="moe_combine_ln2",
    )(dest_flat, dest_flat, h, gate, ln2_g, ln2_b, y_rows)


def _t5_bucket_static(dist):
    d = np.maximum(dist, 0)
    d_large = np.maximum(d, REL_MAX_EXACT).astype(np.float32)
    val = (np.log(d_large / np.float32(REL_MAX_EXACT)) / np.float32(math.log(REL_MAX_DISTANCE / REL_MAX_EXACT))
           * np.float32(REL_BUCKETS - REL_MAX_EXACT))
    inner = (d > REL_MAX_EXACT) & (d < REL_MAX_DISTANCE)
    assert not np.any(inner & (np.abs(val - np.round(val)) < 1e-5))
    large = REL_MAX_EXACT + val.astype(np.int32)
    return np.where(d < REL_MAX_EXACT, d, np.minimum(large, REL_BUCKETS - 1)).astype(np.int32)


def _bias_table(rel_bias, dist, head, valid):
    hi = lax.Precision.HIGHEST
    head_rows = jnp.dot(jnp.asarray(np.eye(REL_HEADS, dtype=np.float32)[np.asarray(head)]),
                        rel_bias.astype(F32).T, precision=hi)
    onehot = (jnp.asarray(_t5_bucket_static(np.asarray(dist)))[..., None]
              == jnp.arange(REL_BUCKETS, dtype=I32)).astype(F32)
    tab = jnp.sum(onehot * head_rows[:, None, :], axis=-1)
    return jnp.where(jnp.asarray(np.asarray(valid)), tab, NEG_INF)


_A_SET_HEADS = ((0, 2, 5, 7), (1, 3, 4, 6))


def _tables_a_prompt(rel_bias, sinks):
    heads = np.array(_A_SET_HEADS).reshape(-1)
    qi = np.arange(QBLK)[:, None]
    kj = np.arange(2 * QBLK)[None, :]
    dist = np.tile(QBLK + qi - kj, (len(heads), 1))
    bias = _bias_table(rel_bias, dist, np.repeat(heads, QBLK), (dist >= 0) & (dist < A_WINDOW))
    pick = jnp.asarray(np.eye(A_HEADS, dtype=np.float32)[heads])
    sink = jnp.repeat(jnp.dot(pick, sinks.astype(F32), precision=lax.Precision.HIGHEST), QBLK)
    return bias.reshape(2, 4 * QBLK, 2 * QBLK), sink.reshape(2, 4 * QBLK, 1)


def _tables_b_prompt(rel_bias):
    qi = np.arange(QBLK)[:, None]
    kj = np.arange(2 * QBLK)[None, :]
    sub = np.tile(QBLK + qi - kj, (2, 1))
    dist, head, valid = [], [], []
    for g, (window, dil) in enumerate(B_GROUPS):
        for sp in range(2):
            dist.append(sub * dil)
            head.append(np.repeat(A_HEADS + g * B_KV_HEADS + 2 * sp + np.arange(2), QBLK))
            valid.append((sub >= 0) & (sub <= window // dil))
    tab = _bias_table(rel_bias, np.concatenate(dist), np.concatenate(head), np.concatenate(valid))
    return tab.reshape(len(B_GROUPS), 2, 2 * QBLK, 2 * QBLK)


def _tables_a_sample(rel_bias, sinks, la):
    assert la == LANES
    n = SAMPLE_ROWS
    rep = A_HEADS // A_KV_HEADS
    qi = np.tile(np.arange(n), A_HEADS)[:, None]
    head = np.repeat(np.arange(A_HEADS), n)
    col = np.arange(LANES)[None, :]
    dist_c = la + qi - col
    dist_n = qi - col
    valid_c = (dist_c >= 0) & (dist_c < A_WINDOW)
    valid_n = (dist_n >= 0) & (dist_n < A_WINDOW) & (col < n)
    tab = _bias_table(rel_bias, np.concatenate([dist_c, dist_n]), np.concatenate([head, head]),
                      np.concatenate([valid_c, valid_n]))
    sink = jnp.repeat(sinks.astype(F32), n).reshape(A_KV_HEADS, rep * n, 1)
    return tab.reshape(2, A_KV_HEADS, rep * n, LANES), sink


def _tables_b_sample(rel_bias, lb):
    n = SAMPLE_ROWS
    n_groups = len(B_GROUPS)
    h = np.repeat(np.arange(B_KV_HEADS), n_groups * n)
    g = np.tile(np.repeat(np.arange(n_groups), n), B_KV_HEADS)
    qi = np.tile(np.arange(n), B_KV_HEADS * n_groups)[:, None]
    window = np.array([w for w, _ in B_GROUPS])[g][:, None]
    dil = np.array([d for _, d in B_GROUPS])[g][:, None]
    coln = np.arange(LANES)[None, :]

    def table(dist, extra):
        valid = (dist >= 0) & (dist <= window) & (dist % dil == 0) & extra
        return _bias_table(rel_bias, dist, A_HEADS + g * B_KV_HEADS + h, valid)

    return table(lb + qi - np.arange(lb)[None, :], True), table(qi - coln, coln < n)


def _mask_mem_sample():
    row_head = np.repeat(np.arange(M_HEADS), SAMPLE_ROWS)[:, None]
    col_head = np.arange(MEM_TOKENS * M_HEADS)[None, :] % M_HEADS
    return jnp.asarray(np.where(row_head == col_head, 0.0, -np.inf).astype(np.float32))


def _swap_halves(w):
    return jnp.concatenate([w[:, HEAD_DIM:], w[:, :HEAD_DIM]], axis=1)


def _slabs_to_rows(slabs):
    return jnp.moveaxis(slabs, 0, 1).reshape(slabs.shape[1], slabs.shape[0] * LANES)


def _heads_to_rows(o):
    return jnp.moveaxis(o, 0, 1).reshape(o.shape[1], o.shape[0] * HEAD_DIM)


def kernel(x_prompt, x_sample, cache_a_k, cache_a_v, cache_b_k, cache_b_v, cache_mem_k, cache_mem_v, mem_prompt,
           rel_bias, sinks_a, w_in, w_mem_kv, w_br_a, w_br_b, w_br_m, w_o, ln1_g, ln1_b, ln2_g, ln2_b, w_router,
           b_router, w_gu, b_gu, w_down, b_down):
    assert w_in.shape[0] == DEPTH == 1
    bsz, seq, _ = x_prompt.shape
    dec_b, dec_n, _ = x_sample.shape
    assert bsz == 1 and dec_n <= SAMPLE_ROWS
    la, lb = cache_a_k.shape[2], cache_b_k.shape[2]
    assert la == A_WINDOW and lb == B_WINDOW_MAX

    w = w_in[0]
    c = np.cumsum((0, 512, 128, 128, 768, 256, 256, 512, 3072))
    w_qa, w_ka, w_va, w_qb, w_kb, w_vb, w_qm, w_g = (w[:, c[i]:c[i + 1]] for i in range(8))
    w_proj = jnp.concatenate([w_qb, w_kb, w_vb, w_qa, w_ka, _swap_halves(w_ka), w_va, _swap_halves(w_va), w_qm],
                             axis=1).astype(BF16)
    wr = jnp.pad(w_router[0].astype(F32), ((0, 0), (0, LANES - N_EXPERTS)))
    wr_hi = wr.astype(BF16)
    merge_w = dict(
        w_g=w_g.astype(BF16), w_a=w_br_a[0].astype(BF16), w_b=w_br_b[0].astype(BF16), w_m=w_br_m[0].astype(BF16),
        w_o=w_o[0].astype(BF16), ln1_g=ln1_g.astype(F32).reshape(1, D_MODEL), ln1_b=ln1_b.astype(F32).reshape(1, D_MODEL),
        wr_hi=wr_hi, wr_lo=(wr - wr_hi.astype(F32)).astype(BF16),
        b_r=jnp.pad(b_router.astype(F32).reshape(1, N_EXPERTS), ((0, 0), (0, LANES - N_EXPERTS)),
                    constant_values=NEG_INF))
    ln2g = ln2_g.astype(F32).reshape(1, D_MODEL)
    ln2b = ln2_b.astype(F32).reshape(1, D_MODEL)

    xp = x_prompt.reshape(seq, D_MODEL)
    qb_p, kvb_p, qa_p, kva_p, qm_p = _project(xp, w_proj, 1024)
    mkv_p = _matmul_f32(mem_prompt.reshape(MEM_TOKENS, D_MODEL), w_mem_kv[0].astype(BF16))
    bias_a_p, sink_a_p = _tables_a_prompt(rel_bias, sinks_a[0])
    oa_p = _attn_a_prompt(qa_p, kva_p, bias_a_p, sink_a_p, 1024)
    ob_p = _attn_b_prompt(qb_p, kvb_p, _tables_b_prompt(rel_bias))
    om_p = _attn_mem_prompt(qm_p, mkv_p, 1024)
    zero_cnt = jnp.zeros((1, LANES), F32)
    n_tok = seq + dec_b * dec_n
    n_rows = n_tok * TOP_K + N_EXPERTS * EXPERT_ROWS
    h_p, route_p, gate_p, cnt_p, x_rows = _merge(zero_cnt, xp, oa_p, ob_p, om_p, merge_w, 512, 1, zero_rows=n_rows)

    xs = jnp.pad(x_sample, ((0, 0), (0, SAMPLE_ROWS - dec_n), (0, 0))).reshape(dec_b * SAMPLE_ROWS, D_MODEL)
    qb_s, kvb_s, qa_s, kva_s, qm_s = _project(xs, w_proj, dec_b * SAMPLE_ROWS)
    bias_a_s, sink_a_s = _tables_a_sample(rel_bias, sinks_a[0], la)
    to_hdt = lambda c: jnp.transpose(c[0], (0, 2, 3, 1))
    oa_s = _attn_a_sample(qa_s, kva_s, to_hdt(cache_a_k), to_hdt(cache_a_v), bias_a_s, sink_a_s, 16)
    ob_s = _attn_b_sample(qb_s, kvb_s, to_hdt(cache_b_k), to_hdt(cache_b_v), *_tables_b_sample(rel_bias, lb), 4)
    om_s = _attn_mem_sample(qm_s, cache_mem_k[0].reshape(dec_b, MEM_TOKENS * M_HEADS, M_HEAD_DIM),
                            cache_mem_v[0].reshape(dec_b, MEM_TOKENS * M_HEADS, M_HEAD_DIM), _mask_mem_sample(), 16)
    unpad = lambda a: a.reshape(dec_b, SAMPLE_ROWS, a.shape[-1])[:, :dec_n].reshape(dec_b * dec_n, a.shape[-1])
    x_s = x_sample.reshape(dec_b * dec_n, D_MODEL)
    h_s, route_s, gate_s, cnt_all = _merge(cnt_p, x_s, unpad(_heads_to_rows(oa_s)), unpad(ob_s),
                                           unpad(om_s), merge_w, dec_b * dec_n, 1)

    counts = cnt_all[0, :N_EXPERTS].astype(I32)
    padded = (counts + EXPERT_ROWS - 1) // EXPERT_ROWS * EXPERT_ROWS
    pad_end = jnp.cumsum(padded)
    pad_start = pad_end - padded
    n_valid = (pad_end[-1] // EXPERT_ROWS).reshape(1).astype(I32)
    block_row = jnp.arange(n_rows // EXPERT_ROWS, dtype=I32) * EXPERT_ROWS
    block_expert = jnp.minimum(jnp.sum((pad_end[None, :] <= block_row[:, None]).astype(I32), axis=1), N_EXPERTS - 1)
    expert_ids = jnp.arange(N_EXPERTS, dtype=I32)

    def dest(route):
        start = jnp.sum(jnp.where(route[:, :TOP_K, None] == expert_ids, pad_start, 0), axis=-1)
        return (start + route[:, TOP_K:2 * TOP_K]).reshape(-1).astype(I32)

    dest_p, dest_s = dest(route_p), dest(route_s)
    x_rows = _dispatch(dest_p, h_p, x_rows, 512)
    x_rows = _dispatch(dest_s, h_s, x_rows, dec_b * dec_n)
    changed = jnp.concatenate([jnp.zeros((1,), I32), (block_expert[1:] != block_expert[:-1]).astype(I32)])
    block_slot = jnp.cumsum(changed) % 2
    later = (expert_ids[None, :] > expert_ids[:, None]) & (counts[None, :] > 0)
    next_nonempty = jnp.min(jnp.where(later, expert_ids[None, :], N_EXPERTS), axis=1)
    next_nonempty = jnp.where(next_nonempty < N_EXPERTS, next_nonempty, -1)
    block_next = jnp.sum(jnp.where(block_expert[:, None] == expert_ids, next_nonempty, 0), axis=1)
    y_rows = _experts(block_expert, n_valid, block_slot.astype(I32), block_next.astype(I32), x_rows, w_gu[0], b_gu[0],
                      w_down[0], b_down[0])
    y_p = _combine(dest_p, h_p, gate_p, ln2g, ln2b, y_rows, 512)
    y_s = _combine(dest_s, h_s, gate_s, ln2g, ln2b, y_rows, 256)

    kva_rows = _slabs_to_rows(kva_p[:, seq - la:])
    kvb_rows = _slabs_to_rows(kvb_p[:, seq - lb:])
    a_k_p = kva_rows[:, 0:128].reshape(1, 1, la, A_KV_HEADS, HEAD_DIM)
    a_v_p = kva_rows[:, 256:384].reshape(1, 1, la, A_KV_HEADS, HEAD_DIM)
    b_k_p = kvb_rows[:, 0:256].reshape(1, 1, lb, B_KV_HEADS, HEAD_DIM)
    b_v_p = kvb_rows[:, 256:512].reshape(1, 1, lb, B_KV_HEADS, HEAD_DIM)
    m_k_p = mkv_p[:, :M_HEADS * LANES].reshape(1, 1, MEM_TOKENS, M_HEADS, M_HEAD_DIM)
    m_v_p = mkv_p[:, M_HEADS * LANES:].reshape(1, 1, MEM_TOKENS, M_HEADS, M_HEAD_DIM)
    kva_s_rows = unpad(_slabs_to_rows(kva_s))
    kvb_s_rows = unpad(_slabs_to_rows(kvb_s))
    a_k_s = kva_s_rows[:, 0:128].reshape(1, dec_b, dec_n, A_KV_HEADS, HEAD_DIM)
    a_v_s = kva_s_rows[:, 256:384].reshape(1, dec_b, dec_n, A_KV_HEADS, HEAD_DIM)
    b_k_s = kvb_s_rows[:, 0:256].reshape(1, dec_b, dec_n, B_KV_HEADS, HEAD_DIM)
    b_v_s = kvb_s_rows[:, 256:512].reshape(1, dec_b, dec_n, B_KV_HEADS, HEAD_DIM)
    return (y_p.reshape(bsz, seq, D_MODEL), y_s.reshape(dec_b, dec_n, D_MODEL),
            a_k_p, a_v_p, b_k_p, b_v_p, m_k_p, m_v_p, a_k_s, a_v_s, b_k_s, b_v_s)
```

```python
import functools
import math

import numpy as np
import jax
import jax.numpy as jnp
from jax import lax
from jax.experimental import pallas as pl
from jax.experimental.pallas import tpu as pltpu

F32 = jnp.float32
BF16 = jnp.bfloat16
I32 = jnp.int32

D_MODEL = 1024
HEAD_DIM = 64
A_HEADS = 8
A_KV_HEADS = 2
A_WINDOW = 128
B_KV_HEADS = 4
B_GROUPS = ((128, 1), (512, 4), (2048, 16))
B_WINDOW_MAX = 2048
MEM_TOKENS = 256
M_HEADS = 4
M_HEAD_DIM = 128
REL_BUCKETS = 32
REL_MAX_EXACT = REL_BUCKETS // 2
REL_MAX_DISTANCE = B_WINDOW_MAX
REL_HEADS = A_HEADS + B_KV_HEADS * len(B_GROUPS)
N_EXPERTS = 32
TOP_K = 4
D_FF = D_MODEL
SWIGLU_LIMIT = 7.0
SWIGLU_ALPHA = 1.702
LN_EPS = 1e-5
DEPTH = 1
DEEPNORM_ALPHA = (2 * DEPTH) ** 0.25

LANES = 128
TILE_ROWS = 8
QBLK = 128
SAMPLE_ROWS = 8
EXPERT_ROWS = 512
VMEM_LIMIT = 56 * 1024 * 1024
NEG_INF = float("-inf")
B_DIL_MAX = B_GROUPS[-1][1]
ROW_DMA_UNROLL = 8
SAMPLE_UNROLL = 4
B_UNIT_UNROLL = 8

PROJ_SLABS = (6, 4, 4, 4, 4)


def _cparams(sem):
    return pltpu.CompilerParams(dimension_semantics=sem, vmem_limit_bytes=VMEM_LIMIT)


def _dot(a, b):
    return jnp.dot(a, b, preferred_element_type=F32)


def _dot_t(a, b):
    return lax.dot_general(a, b, (((1,), (1,)), ((), ())), preferred_element_type=F32)


def _lo_mask(rows):
    return lax.broadcasted_iota(I32, (rows, LANES), 1) < HEAD_DIM


def _mem_attention(q, mkv_ref, h, ones):
    k = mkv_ref[:, h * LANES:(h + 1) * LANES].astype(BF16)
    v = mkv_ref[:, (M_HEADS + h) * LANES:(M_HEADS + h + 1) * LANES].astype(BF16)
    s = _dot_t(q.astype(BF16), k) * (M_HEAD_DIM ** -0.5)
    m = jnp.max(s, axis=-1, keepdims=True)
    ol = _dot(jnp.exp(s - m).astype(BF16), jnp.concatenate([v, ones], axis=1))
    return ol[:, :LANES] / ol[:, LANES:]


def _proj_body(x_ref, w_ref, *refs, with_memory):
    if with_memory:
        mkv_ref, o_refs, om_ref = refs[0], refs[1:-1], refs[-1]
    else:
        o_refs = refs
    x = x_ref[...].astype(BF16)
    col = 0
    for o_ref in o_refs:
        n = o_ref.shape[0] * LANES
        acc = _dot(x, w_ref[:, col:col + n])
        for s in range(o_ref.shape[0]):
            o_ref[s] = acc[:, s * LANES:(s + 1) * LANES]
        col += n
    if with_memory:
        ones = jnp.ones((MEM_TOKENS, LANES), BF16)
        qm = _dot(x, w_ref[:, col:col + M_HEADS * LANES])
        for h in range(M_HEADS):
            om_ref[:, h * LANES:(h + 1) * LANES] = _mem_attention(qm[:, h * LANES:(h + 1) * LANES], mkv_ref, h,
                                                                  ones).astype(om_ref.dtype)


def _project(x, w_bf, tm, mkv=None):
    rows = x.shape[0]
    n_cols = w_bf.shape[1]
    assert rows % tm == 0 and n_cols == sum(PROJ_SLABS) * LANES
    slabs = PROJ_SLABS if mkv is None else PROJ_SLABS[:-1]
    in_specs = [pl.BlockSpec((tm, D_MODEL), lambda i: (i, 0)), pl.BlockSpec((D_MODEL, n_cols), lambda i: (0, 0))]
    out_specs = [pl.BlockSpec((n, tm, LANES), lambda i: (0, i, 0)) for n in slabs]
    out_shape = [jax.ShapeDtypeStruct((n, rows, LANES), F32) for n in slabs]
    args = (x, w_bf)
    if mkv is not None:
        in_specs.append(pl.BlockSpec(mkv.shape, lambda i: (0, 0)))
        out_specs.append(pl.BlockSpec((tm, M_HEADS * LANES), lambda i: (i, 0)))
        out_shape.append(jax.ShapeDtypeStruct((rows, M_HEADS * LANES), BF16))
        args = (x, w_bf, mkv)
    return pl.pallas_call(
        functools.partial(_proj_body, with_memory=mkv is not None),
        grid=(rows // tm,),
        in_specs=in_specs,
        out_specs=out_specs,
        out_shape=out_shape,
        compiler_params=_cparams(("parallel",)),
        name="in_proj",
    )(*args)


def _matmul_body(x_ref, w_ref, o_ref):
    o_ref[...] = _dot(x_ref[...].astype(BF16), w_ref[...])


def _matmul_f32(x, w_bf):
    return pl.pallas_call(
        _matmul_body,
        out_shape=jax.ShapeDtypeStruct((x.shape[0], w_bf.shape[1]), F32),
        compiler_params=_cparams(None),
        name="mem_kv_proj",
    )(x, w_bf)


def _softmax_lse(s):
    m = jnp.max(s, axis=-1, keepdims=True)
    p = jnp.exp(s - m)
    l = jnp.sum(p, axis=-1, keepdims=True)
    return p / l, m + jnp.log(l)


def _softmax_parts(parts, sink=None):
    m = functools.reduce(jnp.maximum, [jnp.max(s, axis=-1, keepdims=True) for s in parts])
    if sink is not None:
        m = jnp.maximum(m, sink)
    ps = [jnp.exp(s - m) for s in parts]
    l = functools.reduce(jnp.add, [jnp.sum(p, axis=-1, keepdims=True) for p in ps])
    lse = m + jnp.log(l)
    if sink is not None:
        l = l + jnp.exp(sink - m)
    return [p / l for p in ps], lse


def _attn_a_prompt_body(q_ref, kvc_ref, kvp_ref, bias_ref, sink_ref, o_ref, *, nsub):
    first = pl.program_id(0) == 0
    lo = _lo_mask(QBLK)
    col = lax.broadcasted_iota(I32, (4 * QBLK, 2 * QBLK), 1)
    ones = jnp.ones((2 * QBLK, LANES), BF16)
    for j in range(nsub):
        rows = slice(j * QBLK, (j + 1) * QBLK)

        def keys(slab):
            cur = kvc_ref[slab, rows, :]
            prev = kvp_ref[slab] if j == 0 else kvc_ref[slab, (j - 1) * QBLK:j * QBLK, :]
            return jnp.concatenate([prev, cur], axis=0).astype(BF16)

        q = [q_ref[p, rows, :] for p in range(4)]
        zero = jnp.zeros_like(q[0])
        q_sets = (
            jnp.concatenate([jnp.where(lo, q[0], zero), jnp.where(lo, q[1], zero),
                             jnp.where(lo, zero, q[2]), jnp.where(lo, zero, q[3])], axis=0).astype(BF16),
            jnp.concatenate([jnp.where(lo, zero, q[0]), jnp.where(lo, zero, q[1]),
                             jnp.where(lo, q[2], zero), jnp.where(lo, q[3], zero)], axis=0).astype(BF16),
        )
        outs = []
        for st in range(2):
            s = _dot_t(q_sets[st], keys(st)) * (HEAD_DIM ** -0.5) + bias_ref[st]
            if j == 0:
                s = jnp.where(jnp.logical_and(first, col < QBLK), NEG_INF, s)
            sink = sink_ref[st]
            m = jnp.maximum(jnp.max(s, axis=-1, keepdims=True), sink)
            ol = _dot(jnp.exp(s - m).astype(BF16), jnp.concatenate([keys(2 + st), ones], axis=1))
            outs.append(ol[:, :LANES] / (ol[:, LANES:] + jnp.exp(sink - m)))
        o1, o2 = outs
        blk = lambda o, r: o[r * QBLK:(r + 1) * QBLK]
        slabs = (jnp.where(lo, blk(o1, 0), blk(o2, 0)), jnp.where(lo, blk(o1, 1), blk(o2, 1)),
                 jnp.where(lo, blk(o2, 2), blk(o1, 2)), jnp.where(lo, blk(o2, 3), blk(o1, 3)))
        for p in range(4):
            o_ref[rows, p * LANES:(p + 1) * LANES] = slabs[p].astype(o_ref.dtype)


def _attn_a_prompt(qa, kva, bias, sink, tq):
    seq = qa.shape[1]
    nsub = tq // QBLK
    return pl.pallas_call(
        functools.partial(_attn_a_prompt_body, nsub=nsub),
        grid=(seq // tq,),
        in_specs=[pl.BlockSpec((4, tq, LANES), lambda n: (0, n, 0)),
                  pl.BlockSpec((4, tq, LANES), lambda n: (0, n, 0)),
                  pl.BlockSpec((4, QBLK, LANES), lambda n: (0, jnp.maximum(n * nsub - 1, 0), 0)),
                  pl.BlockSpec((2, 4 * QBLK, 2 * QBLK), lambda n: (0, 0, 0)),
                  pl.BlockSpec((2, 4 * QBLK, 1), lambda n: (0, 0, 0))],
        out_specs=pl.BlockSpec((tq, 4 * LANES), lambda n: (n, 0)),
        out_shape=jax.ShapeDtypeStruct((seq, 4 * LANES), BF16),
        compiler_params=_cparams(("parallel",)),
        name="attn_a_prompt",
    )(qa, kva, kva, bias, sink)


def _attn_b_prompt_body(q_ref, kvc_ref, kvp_ref, bias_ref, o_ref, kv_buf, o_buf, l_buf, *, sb):
    first = pl.program_id(0) == 0
    kv_buf[:, :sb, :] = kvp_ref[...]
    kv_buf[:, sb:, :] = kvc_ref[...]
    lo = _lo_mask(QBLK)
    col = lax.broadcasted_iota(I32, (2 * QBLK, 2 * QBLK), 1)
    ones = jnp.ones((2 * QBLK, LANES), BF16)
    units = sb // QBLK
    for g, (_, dil) in enumerate(B_GROUPS):
        blocks_per_class = units // dil

        def unit(u, carry, g=g, dil=dil, blocks_per_class=blocks_per_class):
            r = u // blocks_per_class
            m = u % blocks_per_class
            q_start = r + dil * QBLK * m
            k_start = sb - dil * QBLK + q_start
            if dil == 1:
                q_start = pl.multiple_of(q_start, QBLK)
                k_start = pl.multiple_of(k_start, QBLK)
                q_idx = pl.ds(q_start, QBLK)
                k_idx = pl.ds(k_start, 2 * QBLK)
            else:
                q_idx = pl.ds(q_start, QBLK, stride=dil)
                k_idx = pl.ds(k_start, 2 * QBLK, stride=dil)
            for sp in range(2):
                q = q_ref[2 * g + sp, q_idx, :]
                zero = jnp.zeros_like(q)
                q2 = jnp.concatenate([jnp.where(lo, q, zero), jnp.where(lo, zero, q)], axis=0).astype(BF16)
                k = kv_buf[sp, k_idx, :].astype(BF16)
                v = kv_buf[2 + sp, k_idx, :].astype(BF16)
                s = _dot_t(q2, k) * (HEAD_DIM ** -0.5) + bias_ref[g, sp]
                s = jnp.where(jnp.logical_and(jnp.logical_and(first, m == 0), col < QBLK), NEG_INF, s)
                m = jnp.max(s, axis=-1, keepdims=True)
                ol = _dot(jnp.exp(s - m).astype(BF16), jnp.concatenate([v, ones], axis=1))
                o = ol[:, :LANES] / ol[:, LANES:]
                lse = m + jnp.log(ol[:, LANES:])
                o_buf[g, sp, q_idx, :] = jnp.where(lo, o[:QBLK], o[QBLK:])
                l_buf[g, sp, q_idx, :] = jnp.where(lo, lse[:QBLK], lse[QBLK:])
            return carry

        lax.fori_loop(0, units, unit, 0, unroll=B_UNIT_UNROLL)
    for sp in range(2):
        l = [l_buf[g, sp] for g in range(3)]
        mx = jnp.maximum(jnp.maximum(l[0], l[1]), l[2])
        e = [jnp.exp(x - mx) for x in l]
        den = e[0] + e[1] + e[2]
        acc = (e[0] / den) * o_buf[0, sp] + (e[1] / den) * o_buf[1, sp] + (e[2] / den) * o_buf[2, sp]
        o_ref[:, sp * LANES:(sp + 1) * LANES] = acc.astype(o_ref.dtype)


def _attn_b_prompt(qb, kvb, bias):
    seq = qb.shape[1]
    sb = B_DIL_MAX * QBLK
    assert seq % sb == 0
    return pl.pallas_call(
        functools.partial(_attn_b_prompt_body, sb=sb),
        grid=(seq // sb,),
        in_specs=[pl.BlockSpec((6, sb, LANES), lambda n: (0, n, 0)),
                  pl.BlockSpec((4, sb, LANES), lambda n: (0, n, 0)),
                  pl.BlockSpec((4, sb, LANES), lambda n: (0, jnp.maximum(n - 1, 0), 0)),
                  pl.BlockSpec((3, 2, 2 * QBLK, 2 * QBLK), lambda n: (0, 0, 0, 0))],
        out_specs=pl.BlockSpec((sb, 2 * LANES), lambda n: (n, 0)),
        out_shape=jax.ShapeDtypeStruct((seq, 2 * LANES), BF16),
        scratch_shapes=[pltpu.VMEM((4, 2 * sb, LANES), F32),
                        pltpu.VMEM((3, 2, sb, LANES), F32),
                        pltpu.VMEM((3, 2, sb, LANES), F32)],
        compiler_params=_cparams(("parallel",)),
        name="attn_b_prompt",
    )(qb, kvb, kvb, bias)


def _attn_mem_sample_body(q_ref, mk_ref, mv_ref, mask_ref, o_ref, *, bb):
    def one(b, carry):
        rows = pl.ds(pl.multiple_of(b * SAMPLE_ROWS, SAMPLE_ROWS), SAMPLE_ROWS)
        q = jnp.concatenate([q_ref[h, rows, :] for h in range(M_HEADS)], axis=0).astype(BF16)
        s = _dot_t(q, mk_ref[b].astype(BF16)) * (M_HEAD_DIM ** -0.5) + mask_ref[...]
        p, _ = _softmax_lse(s)
        o = _dot(p.astype(BF16), mv_ref[b].astype(BF16))
        for h in range(M_HEADS):
            o_ref[rows, h * LANES:(h + 1) * LANES] = o[h * SAMPLE_ROWS:(h + 1) * SAMPLE_ROWS].astype(o_ref.dtype)
        return carry

    lax.fori_loop(0, bb, one, 0, unroll=SAMPLE_UNROLL)


def _attn_mem_sample(qm, mk, mv, mask, bb):
    batch, n_rows = mk.shape[0], mk.shape[1]
    return pl.pallas_call(
        functools.partial(_attn_mem_sample_body, bb=bb),
        grid=(batch // bb,),
        in_specs=[pl.BlockSpec((M_HEADS, bb * SAMPLE_ROWS, LANES), lambda n: (0, n, 0)),
                  pl.BlockSpec((bb, n_rows, LANES), lambda n: (n, 0, 0)),
                  pl.BlockSpec((bb, n_rows, LANES), lambda n: (n, 0, 0)),
                  pl.BlockSpec(mask.shape, lambda n: (0, 0))],
        out_specs=pl.BlockSpec((bb * SAMPLE_ROWS, M_HEADS * LANES), lambda n: (n, 0)),
        out_shape=jax.ShapeDtypeStruct((batch * SAMPLE_ROWS, M_HEADS * LANES), BF16),
        compiler_params=_cparams(("parallel",)),
        name="attn_mem_sample",
    )(qm, mk, mv, mask)


def _head_rows(ref, slab, rows, half):
    x = ref[slab, rows, :]
    return x[:, half * HEAD_DIM:(half + 1) * HEAD_DIM]


def _pad_rows(x, n):
    return jnp.concatenate([x, jnp.zeros((n - x.shape[0], x.shape[1]), x.dtype)], axis=0)


def _attn_a_sample_body(q_ref, kvn_ref, ck_ref, cv_ref, bias_ref, sink_ref, o_ref, *, bb):
    rep = A_HEADS // A_KV_HEADS

    def one(b, carry):
        rows = pl.ds(pl.multiple_of(b * SAMPLE_ROWS, SAMPLE_ROWS), SAMPLE_ROWS)
        for g in range(A_KV_HEADS):
            heads = range(g * rep, (g + 1) * rep)
            q = jnp.concatenate([_head_rows(q_ref, h // 2, rows, h % 2) for h in heads], axis=0).astype(BF16)
            kn = _pad_rows(_head_rows(kvn_ref, 0, rows, g), LANES).astype(BF16)
            vn = _pad_rows(_head_rows(kvn_ref, 2, rows, g), LANES).astype(BF16)
            s_c = _dot(q, ck_ref[b, g].astype(BF16)) * (HEAD_DIM ** -0.5) + bias_ref[0, g]
            s_n = _dot_t(q, kn) * (HEAD_DIM ** -0.5) + bias_ref[1, g]
            (p_c, p_n), _ = _softmax_parts([s_c, s_n], sink_ref[g])
            o = _dot_t(p_c.astype(BF16), cv_ref[b, g].astype(BF16)) + _dot(p_n.astype(BF16), vn)
            for r, h in enumerate(heads):
                o_ref[h, rows, :] = o[r * SAMPLE_ROWS:(r + 1) * SAMPLE_ROWS].astype(o_ref.dtype)
        return carry

    lax.fori_loop(0, bb, one, 0, unroll=SAMPLE_UNROLL)


def _attn_a_sample(qa, kva, ck, cv, bias, sink, bb):
    batch = ck.shape[0]
    full = lambda a: pl.BlockSpec(a.shape, lambda n: (0,) * a.ndim)
    buf_spec = pl.BlockSpec((bb,) + ck.shape[1:], lambda n: (n, 0, 0, 0))
    return pl.pallas_call(
        functools.partial(_attn_a_sample_body, bb=bb),
        grid=(batch // bb,),
        in_specs=[pl.BlockSpec((4, bb * SAMPLE_ROWS, LANES), lambda n: (0, n, 0)),
                  pl.BlockSpec((4, bb * SAMPLE_ROWS, LANES), lambda n: (0, n, 0)),
                  buf_spec, buf_spec, full(bias), full(sink)],
        out_specs=pl.BlockSpec((A_HEADS, bb * SAMPLE_ROWS, HEAD_DIM), lambda n: (0, n, 0)),
        out_shape=jax.ShapeDtypeStruct((A_HEADS, batch * SAMPLE_ROWS, HEAD_DIM), BF16),
        compiler_params=_cparams(("parallel",)),
        name="attn_a_sample",
    )(qa, kva, ck, cv, bias, sink)


def _attn_b_sample_body(q_ref, kvn_ref, ck_ref, cv_ref, bias_c_ref, bias_n_ref, o_ref, *, bb):
    n = SAMPLE_ROWS
    n_groups = len(B_GROUPS)
    lo = _lo_mask(n)
    scale = HEAD_DIM ** -0.5
    zero = jnp.zeros((n, LANES), F32)
    for b in range(bb):
        rows = slice(b * n, (b + 1) * n)
        q_rows = []
        for h in range(B_KV_HEADS):
            for g in range(n_groups):
                x = q_ref[2 * g + h // 2, rows, :]
                x = jnp.where(lo, x, zero) if h % 2 == 0 else jnp.where(lo, zero, x)
                q_rows.append(jnp.concatenate([x, zero] if h // 2 == 0 else [zero, x], axis=1))
        q = jnp.concatenate(q_rows, axis=0).astype(BF16)
        kn = _pad_rows(jnp.concatenate([kvn_ref[0, rows, :], kvn_ref[1, rows, :]], axis=1), LANES).astype(BF16)
        vn = _pad_rows(jnp.concatenate([kvn_ref[2, rows, :], kvn_ref[3, rows, :]], axis=1), LANES).astype(BF16)
        kt = ck_ref[b].reshape(B_KV_HEADS * HEAD_DIM, ck_ref.shape[-1]).astype(BF16)
        vt = cv_ref[b].reshape(B_KV_HEADS * HEAD_DIM, cv_ref.shape[-1]).astype(BF16)
        s_c = _dot(q, kt) * scale + bias_c_ref[...]
        s_n = _dot_t(q, kn) * scale + bias_n_ref[...]
        (p_c, p_n), lse = _softmax_parts([s_c, s_n])
        o = _dot_t(p_c.astype(BF16), vt) + _dot(p_n.astype(BF16), vn)
        merged = []
        for h in range(B_KV_HEADS):
            cols = slice((h // 2) * LANES, (h // 2 + 1) * LANES)
            blk = lambda g: slice((n_groups * h + g) * n, (n_groups * h + g + 1) * n)
            l_g = [lse[blk(g)] for g in range(n_groups)]
            mx = jnp.maximum(jnp.maximum(l_g[0], l_g[1]), l_g[2])
            e = [jnp.exp(x - mx) for x in l_g]
            den = e[0] + e[1] + e[2]
            merged.append((e[0] / den) * o[blk(0), cols] + (e[1] / den) * o[blk(1), cols]
                          + (e[2] / den) * o[blk(2), cols])
        for cg in range(2):
            o_ref[rows, cg * LANES:(cg + 1) * LANES] = jnp.where(lo, merged[2 * cg], merged[2 * cg + 1]).astype(o_ref.dtype)


def _attn_b_sample(qb, kvb, ck, cv, bias_c, bias_n, bb):
    batch = ck.shape[0]
    full = lambda a: pl.BlockSpec(a.shape, lambda n: (0,) * a.ndim)
    buf_spec = pl.BlockSpec((bb,) + ck.shape[1:], lambda n: (n, 0, 0, 0))
    return pl.pallas_call(
        functools.partial(_attn_b_sample_body, bb=bb),
        grid=(batch // bb,),
        in_specs=[pl.BlockSpec((6, bb * SAMPLE_ROWS, LANES), lambda n: (0, n, 0)),
                  pl.BlockSpec((4, bb * SAMPLE_ROWS, LANES), lambda n: (0, n, 0)),
                  buf_spec, buf_spec, full(bias_c), full(bias_n)],
        out_specs=pl.BlockSpec((bb * SAMPLE_ROWS, B_KV_HEADS * HEAD_DIM), lambda n: (n, 0)),
        out_shape=jax.ShapeDtypeStruct((batch * SAMPLE_ROWS, B_KV_HEADS * HEAD_DIM), BF16),
        compiler_params=_cparams(("parallel",)),
        name="attn_b_sample",
    )(qb, kvb, ck, cv, bias_c, bias_n)


def _layer_norm(x, g, b):
    mu = jnp.mean(x, axis=-1, keepdims=True)
    xc = x - mu
    var = jnp.mean(xc * xc, axis=-1, keepdims=True)
    return xc * lax.rsqrt(var + LN_EPS) * g + b


def _split_bf16(x):
    hi = x.astype(BF16)
    return hi, (x - hi.astype(F32)).astype(BF16)


def _merge_body(cnt_in_ref, x_ref, oa_ref, ob_ref, om_ref, wg_ref, wa_ref, wb_ref, wm_ref, wo_ref, g1_ref, b1_ref,
                wr_hi_ref, wr_lo_ref, br_ref, h_ref, route_ref, gate_ref, cnt_ref, *zero_fill, tm, n_sub, zero_blocks,
                per_step):
    i = pl.program_id(0)
    if zero_blocks:
        rows_ref, zbuf, zsem = zero_fill
        blk = EXPERT_ROWS * TILE_ROWS

        @pl.when(i == 0)
        def _():
            zbuf[...] = jnp.zeros_like(zbuf)

        def zero_copy(j):
            start = pl.multiple_of((i * per_step + j) * blk, blk)
            return pltpu.make_async_copy(zbuf, rows_ref.at[pl.ds(start, blk), :], zsem)

        def zero_each(fn):
            def body(j, carry):
                @pl.when(i * per_step + j < zero_blocks)
                def _():
                    fn(zero_copy(j))
                return carry
            lax.fori_loop(0, per_step, body, 0)

        zero_each(lambda c: c.start())
    @pl.when(i == 0)
    def _():
        cnt_ref[...] = cnt_in_ref[...]

    lane = lax.broadcasted_iota(I32, (tm, LANES), 1)
    lane_f = lane.astype(F32)
    row = lax.broadcasted_iota(I32, (tm, tm), 0)
    colm = lax.broadcasted_iota(I32, (tm, tm), 1)
    tri = jnp.where(colm < row, 1.0, 0.0).astype(BF16)
    carry = cnt_ref[...]
    for r in range(n_sub):
        rs = slice(r * tm, (r + 1) * tm)
        x = x_ref[rs, :]
        gates = jax.nn.sigmoid(_dot(x.astype(BF16), wg_ref[...]))
        u = (gates[:, :D_MODEL] * _dot(oa_ref[rs, :], wa_ref[...])
             + gates[:, D_MODEL:2 * D_MODEL] * _dot(ob_ref[rs, :], wb_ref[...])
             + gates[:, 2 * D_MODEL:] * _dot(om_ref[rs, :], wm_ref[...]))
        mixed = _dot(u.astype(BF16), wo_ref[...])
        h = _layer_norm(DEEPNORM_ALPHA * x + mixed, g1_ref[...], b1_ref[...])
        for c in range(TILE_ROWS):
            h_ref[pl.ds(r * tm * TILE_ROWS + c, tm, stride=TILE_ROWS), :] = h[:, c * LANES:(c + 1) * LANES]

        h_hi, h_lo = _split_bf16(h)
        logits = (_dot(h_hi, wr_hi_ref[...]) + (_dot(h_hi, wr_lo_ref[...]) + _dot(h_lo, wr_hi_ref[...]))
                  + br_ref[...])
        work = logits
        vals, sels, idxs = [], [], []
        for _ in range(TOP_K):
            mx = jnp.max(work, axis=-1, keepdims=True)
            idx = jnp.min(jnp.where(work == mx, lane_f, float(LANES)), axis=-1, keepdims=True)
            sel = lane_f == idx
            vals.append(mx)
            idxs.append(idx)
            sels.append(sel)
            work = jnp.where(sel, NEG_INF, work)
        ex = [jnp.exp(v - vals[0]) for v in vals]
        den = ex[0] + ex[1] + ex[2] + ex[3]
        chosen = jnp.logical_or(jnp.logical_or(sels[0], sels[1]), jnp.logical_or(sels[2], sels[3]))
        onehot = jnp.where(chosen, 1.0, 0.0)
        before = _dot(tri, onehot.astype(BF16)) + carry
        route = jnp.zeros((tm, LANES), F32)
        gate = jnp.zeros((tm, LANES), F32)
        for k in range(TOP_K):
            rank = jnp.sum(jnp.where(sels[k], before, 0.0), axis=-1, keepdims=True)
            route = jnp.where(lane == k, idxs[k], route)
            route = jnp.where(lane == TOP_K + k, rank, route)
            gate = jnp.where(lane == k, ex[k] / den, gate)
        route_ref[rs, :] = route.astype(I32)
        gate_ref[rs, :] = gate
        carry = carry + jnp.sum(onehot, axis=0, keepdims=True)
    cnt_ref[...] = carry
    if zero_blocks:
        zero_each(lambda c: c.wait())


def _merge(cnt_in, x, oa, ob, om, w, tm, n_sub, zero_rows=0):
    rows = x.shape[0]
    step_rows = tm * n_sub
    assert zero_rows % EXPERT_ROWS == 0
    zero_blocks = zero_rows // EXPERT_ROWS
    extra_specs = [pl.BlockSpec(memory_space=pl.ANY)] if zero_blocks else []
    extra_shapes = [jax.ShapeDtypeStruct((zero_rows * TILE_ROWS, LANES), F32)] if zero_blocks else []
    scratch = ([pltpu.VMEM((EXPERT_ROWS * TILE_ROWS, LANES), F32), pltpu.SemaphoreType.DMA(())]
               if zero_blocks else [])
    full = lambda a: pl.BlockSpec(a.shape, lambda i: (0,) * a.ndim)
    row_blk = lambda c: pl.BlockSpec((step_rows, c), lambda i: (i, 0))
    weights = (w["w_g"], w["w_a"], w["w_b"], w["w_m"], w["w_o"], w["ln1_g"], w["ln1_b"],
               w["wr_hi"], w["wr_lo"], w["b_r"])
    return pl.pallas_call(
        functools.partial(_merge_body, tm=tm, n_sub=n_sub, zero_blocks=zero_blocks,
                          per_step=-(-zero_blocks // (rows // step_rows))),
        grid=(rows // step_rows,),
        in_specs=[full(cnt_in), row_blk(D_MODEL), row_blk(oa.shape[1]), row_blk(ob.shape[1]), row_blk(om.shape[1])]
        + [full(a) for a in weights],
        out_specs=[pl.BlockSpec((step_rows * TILE_ROWS, LANES), lambda i: (i, 0)), row_blk(LANES), row_blk(LANES),
                   pl.BlockSpec((1, LANES), lambda i: (0, 0))] + extra_specs,
        out_shape=[jax.ShapeDtypeStruct((rows * TILE_ROWS, LANES), F32), jax.ShapeDtypeStruct((rows, LANES), I32),
                   jax.ShapeDtypeStruct((rows, LANES), F32), jax.ShapeDtypeStruct((1, LANES), F32)] + extra_shapes,
        scratch_shapes=scratch,
        compiler_params=_cparams(("arbitrary",)),
        name="merge_ln_router",
    )(cnt_in, x, oa, ob, om, *weights)


def _token_tile(ref, t):
    return ref.at[pl.ds(pl.multiple_of(t * TILE_ROWS, TILE_ROWS), TILE_ROWS), :]


def _natural_rows(ref, n, lead=()):
    return jnp.concatenate([ref[lead + (pl.ds(c, n, stride=TILE_ROWS), slice(None))] for c in range(TILE_ROWS)], axis=1)


def _dispatch_body(dest_ref, h_ref, xin_ref, xout_ref, sem, *, tm):
    del xin_ref

    def copy(t, k):
        return pltpu.make_async_copy(_token_tile(h_ref, t), _token_tile(xout_ref, dest_ref[t * TOP_K + k]), sem)

    def start(t, carry):
        for k in range(TOP_K):
            copy(t, k).start(priority=k % 2)
        return carry

    def wait(t, carry):
        for k in range(TOP_K):
            copy(t, k).wait()
        return carry

    lax.fori_loop(0, tm, start, 0, unroll=ROW_DMA_UNROLL)
    lax.fori_loop(0, tm, wait, 0, unroll=ROW_DMA_UNROLL)


def _dispatch(dest_flat, h, x_rows, tm):
    rows = h.shape[0] // TILE_ROWS
    return pl.pallas_call(
        functools.partial(_dispatch_body, tm=tm),
        grid=(rows // tm,),
        in_specs=[pl.BlockSpec((tm * TOP_K,), lambda i: (i,), memory_space=pltpu.SMEM),
                  pl.BlockSpec((tm * TILE_ROWS, LANES), lambda i: (i, 0)),
                  pl.BlockSpec(memory_space=pl.ANY)],
        out_specs=pl.BlockSpec(memory_space=pl.ANY),
        out_shape=jax.ShapeDtypeStruct(x_rows.shape, x_rows.dtype),
        scratch_shapes=[pltpu.SemaphoreType.DMA(())],
        input_output_aliases={2: 0},
        compiler_params=_cparams(("arbitrary",)),
        name="moe_dispatch",
    )(dest_flat, h, x_rows)


def _expert_body(be_ref, nv_ref, slot_ref, next_ref, x_ref, wgu_hbm, bgu_ref, wd_hbm, bd_ref, y_ref,
                 wgu_f32, wd_f32, wgu_bf, wd_bf, sems):
    i = pl.program_id(0)
    valid = i < nv_ref[0]

    def weight_copies(e, slot):
        return (pltpu.make_async_copy(wgu_hbm.at[e], wgu_f32.at[slot], sems.at[0, slot]),
                pltpu.make_async_copy(wd_hbm.at[e], wd_f32.at[slot], sems.at[1, slot]))

    @pl.when(valid)
    def _():
        e = be_ref[i]
        slot = slot_ref[i]
        new_expert = jnp.logical_or(i == 0, e != be_ref[jnp.maximum(i - 1, 0)])

        @pl.when(i == 0)
        def _():
            for c in weight_copies(e, slot):
                c.start()

        @pl.when(new_expert)
        def _():
            for c in weight_copies(e, slot):
                c.wait()

            @pl.when(next_ref[i] >= 0)
            def _():
                for c in weight_copies(next_ref[i], 1 - slot):
                    c.start()

            wgu_bf[...] = wgu_f32[slot].astype(BF16)
            wd_bf[...] = wd_f32[slot].astype(BF16)

        gu = _dot(_natural_rows(x_ref, EXPERT_ROWS).astype(BF16), wgu_bf[...]) + bgu_ref[0]
        g = jnp.minimum(gu[:, :D_FF], SWIGLU_LIMIT)
        u = jnp.clip(gu[:, D_FF:], -SWIGLU_LIMIT, SWIGLU_LIMIT)
        hidden = (u + 1.0) * g * jax.nn.sigmoid(SWIGLU_ALPHA * g)
        y = _dot(hidden.astype(BF16), wd_bf[...]) + bd_ref[0]
        for c in range(TILE_ROWS):
            y_ref[pl.ds(c, EXPERT_ROWS, stride=TILE_ROWS), :] = y[:, c * LANES:(c + 1) * LANES]

    @pl.when(jnp.logical_not(valid))
    def _():
        y_ref[...] = jnp.zeros_like(y_ref)


def _experts(block_expert, n_valid, block_slot, block_next, x_rows, w_gu, b_gu, w_down, b_down):
    blk_rows = EXPERT_ROWS * TILE_ROWS
    n_blocks = x_rows.shape[0] // blk_rows
    last = lambda i, nv: jnp.minimum(i, nv[0] - 1)
    grid_spec = pltpu.PrefetchScalarGridSpec(
        num_scalar_prefetch=4,
        grid=(n_blocks,),
        in_specs=[pl.BlockSpec((blk_rows, LANES), lambda i, be, nv, sl, nx: (last(i, nv), 0)),
                  pl.BlockSpec(memory_space=pl.ANY),
                  pl.BlockSpec((1, 1, 2 * D_FF), lambda i, be, nv, sl, nx: (be[last(i, nv)], 0, 0)),
                  pl.BlockSpec(memory_space=pl.ANY),
                  pl.BlockSpec((1, 1, D_MODEL), lambda i, be, nv, sl, nx: (be[last(i, nv)], 0, 0))],
        out_specs=pl.BlockSpec((blk_rows, LANES), lambda i, be, nv, sl, nx: (i, 0)),
        scratch_shapes=[pltpu.VMEM((2, D_MODEL, 2 * D_FF), F32), pltpu.VMEM((2, D_FF, D_MODEL), F32),
                        pltpu.VMEM((D_MODEL, 2 * D_FF), BF16), pltpu.VMEM((D_FF, D_MODEL), BF16),
                        pltpu.SemaphoreType.DMA((2, 2))],
    )
    return pl.pallas_call(
        _expert_body,
        grid_spec=grid_spec,
        out_shape=jax.ShapeDtypeStruct(x_rows.shape, F32),
        compiler_params=_cparams(("arbitrary",)),
        name="moe_experts",
    )(block_expert, n_valid, block_slot, block_next, x_rows, w_gu, b_gu.reshape(N_EXPERTS, 1, 2 * D_FF), w_down,
      b_down.reshape(N_EXPERTS, 1, D_MODEL))


def _combine_body(dest_ref, dest_next_ref, h_ref, gate_ref, g2_ref, b2_ref, yrows_ref, o_ref, buf, sems, *, tm):
    i = pl.program_id(0)
    n = pl.num_programs(0)
    slot = lax.rem(i, 2)

    def copy(d_ref, s, t, k):
        return pltpu.make_async_copy(_token_tile(yrows_ref, d_ref[t * TOP_K + k]), _token_tile(buf.at[s, k], t),
                                     sems.at[s])

    def start_tile(d_ref, s):
        def body(t, carry):
            for k in range(TOP_K):
                copy(d_ref, s, t, k).start(priority=k % 2)
            return carry
        lax.fori_loop(0, tm, body, 0, unroll=ROW_DMA_UNROLL)

    @pl.when(i == 0)
    def _():
        start_tile(dest_ref, 0)

    @pl.when(i + 1 < n)
    def _():
        start_tile(dest_next_ref, 1 - slot)

    def wait(t, carry):
        for k in range(TOP_K):
            copy(dest_ref, slot, t, k).wait()
        return carry

    lax.fori_loop(0, tm, wait, 0, unroll=ROW_DMA_UNROLL)
    gate = gate_ref[...]
    f = gate[:, 0:1] * _natural_rows(buf, tm, (slot, 0))
    for k in range(1, TOP_K):
        f = f + gate[:, k:k + 1] * _natural_rows(buf, tm, (slot, k))
    o_ref[...] = _layer_norm(DEEPNORM_ALPHA * _natural_rows(h_ref, tm) + f, g2_ref[...], b2_ref[...])


def _combine(dest_flat, h, gate, ln2_g, ln2_b, y_rows, tm):
    rows = h.shape[0] // TILE_ROWS
    n_tiles = rows // tm
    return pl.pallas_call(
        functools.partial(_combine_body, tm=tm),
        grid=(n_tiles,),
        in_specs=[pl.BlockSpec((tm * TOP_K,), lambda i: (i,), memory_space=pltpu.SMEM),
                  pl.BlockSpec((tm * TOP_K,), lambda i: (jnp.minimum(i + 1, n_tiles - 1),), memory_space=pltpu.SMEM),
                  pl.BlockSpec((tm * TILE_ROWS, LANES), lambda i: (i, 0)),
                  pl.BlockSpec((tm, LANES), lambda i: (i, 0)),
                  pl.BlockSpec((1, D_MODEL), lambda i: (0, 0)),
                  pl.BlockSpec((1, D_MODEL), lambda i: (0, 0)),
                  pl.BlockSpec(memory_space=pl.ANY)],
        out_specs=pl.BlockSpec((tm, D_MODEL), lambda i: (i, 0)),
        out_shape=jax.ShapeDtypeStruct((rows, D_MODEL), F32),
        scratch_shapes=[pltpu.VMEM((2, TOP_K, tm * TILE_ROWS, LANES), F32), pltpu.SemaphoreType.DMA((2,))],
        compiler_params=_cparams(("arbitrary",)),
        name="moe_combine_ln2",
    )(dest_flat, dest_flat, h, gate, ln2_g, ln2_b, y_rows)


def _t5_bucket_static(dist):
    d = np.maximum(dist, 0)
    d_large = np.maximum(d, REL_MAX_EXACT).astype(np.float32)
    val = (np.log(d_large / np.float32(REL_MAX_EXACT)) / np.float32(math.log(REL_MAX_DISTANCE / REL_MAX_EXACT))
           * np.float32(REL_BUCKETS - REL_MAX_EXACT))
    inner = (d > REL_MAX_EXACT) & (d < REL_MAX_DISTANCE)
    assert not np.any(inner & (np.abs(val - np.round(val)) < 1e-5))
    large = REL_MAX_EXACT + val.astype(np.int32)
    return np.where(d < REL_MAX_EXACT, d, np.minimum(large, REL_BUCKETS - 1)).astype(np.int32)


def _bias_table(rel_bias, dist, head, valid):
    hi = lax.Precision.HIGHEST
    head_rows = jnp.dot(jnp.asarray(np.eye(REL_HEADS, dtype=np.float32)[np.asarray(head)]),
                        rel_bias.astype(F32).T, precision=hi)
    onehot = (jnp.asarray(_t5_bucket_static(np.asarray(dist)))[..., None]
              == jnp.arange(REL_BUCKETS, dtype=I32)).astype(F32)
    tab = jnp.sum(onehot * head_rows[:, None, :], axis=-1)
    return jnp.where(jnp.asarray(np.asarray(valid)), tab, NEG_INF)


_A_SET_HEADS = ((0, 2, 5, 7), (1, 3, 4, 6))


def _tables_a_prompt(rel_bias, sinks):
    heads = np.array(_A_SET_HEADS).reshape(-1)
    qi = np.arange(QBLK)[:, None]
    kj = np.arange(2 * QBLK)[None, :]
    dist = np.tile(QBLK + qi - kj, (len(heads), 1))
    bias = _bias_table(rel_bias, dist, np.repeat(heads, QBLK), (dist >= 0) & (dist < A_WINDOW))
    pick = jnp.asarray(np.eye(A_HEADS, dtype=np.float32)[heads])
    sink = jnp.repeat(jnp.dot(pick, sinks.astype(F32), precision=lax.Precision.HIGHEST), QBLK)
    return bias.reshape(2, 4 * QBLK, 2 * QBLK), sink.reshape(2, 4 * QBLK, 1)


def _tables_b_prompt(rel_bias):
    qi = np.arange(QBLK)[:, None]
    kj = np.arange(2 * QBLK)[None, :]
    sub = np.tile(QBLK + qi - kj, (2, 1))
    dist, head, valid = [], [], []
    for g, (window, dil) in enumerate(B_GROUPS):
        for sp in range(2):
            dist.append(sub * dil)
            head.append(np.repeat(A_HEADS + g * B_KV_HEADS + 2 * sp + np.arange(2), QBLK))
            valid.append((sub >= 0) & (sub <= window // dil))
    tab = _bias_table(rel_bias, np.concatenate(dist), np.concatenate(head), np.concatenate(valid))
    return tab.reshape(len(B_GROUPS), 2, 2 * QBLK, 2 * QBLK)


def _tables_a_sample(rel_bias, sinks, la):
    assert la == LANES
    n = SAMPLE_ROWS
    rep = A_HEADS // A_KV_HEADS
    qi = np.tile(np.arange(n), A_HEADS)[:, None]
    head = np.repeat(np.arange(A_HEADS), n)
    col = np.arange(LANES)[None, :]
    dist_c = la + qi - col
    dist_n = qi - col
    valid_c = (dist_c >= 0) & (dist_c < A_WINDOW)
    valid_n = (dist_n >= 0) & (dist_n < A_WINDOW) & (col < n)
    tab = _bias_table(rel_bias, np.concatenate([dist_c, dist_n]), np.concatenate([head, head]),
                      np.concatenate([valid_c, valid_n]))
    sink = jnp.repeat(sinks.astype(F32), n).reshape(A_KV_HEADS, rep * n, 1)
    return tab.reshape(2, A_KV_HEADS, rep * n, LANES), sink


def _tables_b_sample(rel_bias, lb):
    n = SAMPLE_ROWS
    n_groups = len(B_GROUPS)
    h = np.repeat(np.arange(B_KV_HEADS), n_groups * n)
    g = np.tile(np.repeat(np.arange(n_groups), n), B_KV_HEADS)
    qi = np.tile(np.arange(n), B_KV_HEADS * n_groups)[:, None]
    window = np.array([w for w, _ in B_GROUPS])[g][:, None]
    dil = np.array([d for _, d in B_GROUPS])[g][:, None]
    coln = np.arange(LANES)[None, :]

    def table(dist, extra):
        valid = (dist >= 0) & (dist <= window) & (dist % dil == 0) & extra
        return _bias_table(rel_bias, dist, A_HEADS + g * B_KV_HEADS + h, valid)

    return table(lb + qi - np.arange(lb)[None, :], True), table(qi - coln, coln < n)


def _mask_mem_sample():
    row_head = np.repeat(np.arange(M_HEADS), SAMPLE_ROWS)[:, None]
    col_head = np.arange(MEM_TOKENS * M_HEADS)[None, :] % M_HEADS
    return jnp.asarray(np.where(row_head == col_head, 0.0, -np.inf).astype(np.float32))


def _swap_halves(w):
    return jnp.concatenate([w[:, HEAD_DIM:], w[:, :HEAD_DIM]], axis=1)


def _slabs_to_rows(slabs):
    return jnp.moveaxis(slabs, 0, 1).reshape(slabs.shape[1], slabs.shape[0] * LANES)


def _heads_to_rows(o):
    return jnp.moveaxis(o, 0, 1).reshape(o.shape[1], o.shape[0] * HEAD_DIM)


def kernel(x_prompt, x_sample, cache_a_k, cache_a_v, cache_b_k, cache_b_v, cache_mem_k, cache_mem_v, mem_prompt,
           rel_bias, sinks_a, w_in, w_mem_kv, w_br_a, w_br_b, w_br_m, w_o, ln1_g, ln1_b, ln2_g, ln2_b, w_router,
           b_router, w_gu, b_gu, w_down, b_down):
    assert w_in.shape[0] == DEPTH == 1
    bsz, seq, _ = x_prompt.shape
    dec_b, dec_n, _ = x_sample.shape
    assert bsz == 1 and dec_n <= SAMPLE_ROWS
    la, lb = cache_a_k.shape[2], cache_b_k.shape[2]
    assert la == A_WINDOW and lb == B_WINDOW_MAX

    w = w_in[0]
    c = np.cumsum((0, 512, 128, 128, 768, 256, 256, 512, 3072))
    w_qa, w_ka, w_va, w_qb, w_kb, w_vb, w_qm, w_g = (w[:, c[i]:c[i + 1]] for i in range(8))
    w_proj = jnp.concatenate([w_qb, w_kb, w_vb, w_qa, w_ka, _swap_halves(w_ka), w_va, _swap_halves(w_va), w_qm],
                             axis=1).astype(BF16)
    wr = jnp.pad(w_router[0].astype(F32), ((0, 0), (0, LANES - N_EXPERTS)))
    wr_hi = wr.astype(BF16)
    merge_w = dict(
        w_g=w_g.astype(BF16), w_a=w_br_a[0].astype(BF16), w_b=w_br_b[0].astype(BF16), w_m=w_br_m[0].astype(BF16),
        w_o=w_o[0].astype(BF16), ln1_g=ln1_g.astype(F32).reshape(1, D_MODEL), ln1_b=ln1_b.astype(F32).reshape(1, D_MODEL),
        wr_hi=wr_hi, wr_lo=(wr - wr_hi.astype(F32)).astype(BF16),
        b_r=jnp.pad(b_router.astype(F32).reshape(1, N_EXPERTS), ((0, 0), (0, LANES - N_EXPERTS)),
                    constant_values=NEG_INF))
    ln2g = ln2_g.astype(F32).reshape(1, D_MODEL)
    ln2b = ln2_b.astype(F32).reshape(1, D_MODEL)

    xp = x_prompt.reshape(seq, D_MODEL)
    mkv_p = _matmul_f32(mem_prompt.reshape(MEM_TOKENS, D_MODEL), w_mem_kv[0].astype(BF16))
    qb_p, kvb_p, qa_p, kva_p, om_p = _project(xp, w_proj, 1024, mkv=mkv_p)
    bias_a_p, sink_a_p = _tables_a_prompt(rel_bias, sinks_a[0])
    oa_p = _attn_a_prompt(qa_p, kva_p, bias_a_p, sink_a_p, 2048)
    ob_p = _attn_b_prompt(qb_p, kvb_p, _tables_b_prompt(rel_bias))
    zero_cnt = jnp.zeros((1, LANES), F32)
    n_tok = seq + dec_b * dec_n
    n_rows = n_tok * TOP_K + N_EXPERTS * EXPERT_ROWS
    h_p, route_p, gate_p, cnt_p, x_rows = _merge(zero_cnt, xp, oa_p, ob_p, om_p, merge_w, 512, 1, zero_rows=n_rows)

    xs = jnp.pad(x_sample, ((0, 0), (0, SAMPLE_ROWS - dec_n), (0, 0))).reshape(dec_b * SAMPLE_ROWS, D_MODEL)
    qb_s, kvb_s, qa_s, kva_s, qm_s = _project(xs, w_proj, dec_b * SAMPLE_ROWS)
    bias_a_s, sink_a_s = _tables_a_sample(rel_bias, sinks_a[0], la)
    to_hdt = lambda c: jnp.transpose(c[0], (0, 2, 3, 1))
    oa_s = _attn_a_sample(qa_s, kva_s, to_hdt(cache_a_k), to_hdt(cache_a_v), bias_a_s, sink_a_s, 16)
    ob_s = _attn_b_sample(qb_s, kvb_s, to_hdt(cache_b_k), to_hdt(cache_b_v), *_tables_b_sample(rel_bias, lb), 4)
    om_s = _attn_mem_sample(qm_s, cache_mem_k[0].reshape(dec_b, MEM_TOKENS * M_HEADS, M_HEAD_DIM),
                            cache_mem_v[0].reshape(dec_b, MEM_TOKENS * M_HEADS, M_HEAD_DIM), _mask_mem_sample(), 8)
    unpad = lambda a: a.reshape(dec_b, SAMPLE_ROWS, a.shape[-1])[:, :dec_n].reshape(dec_b * dec_n, a.shape[-1])
    x_s = x_sample.reshape(dec_b * dec_n, D_MODEL)
    h_s, route_s, gate_s, cnt_all = _merge(cnt_p, x_s, unpad(_heads_to_rows(oa_s)), unpad(ob_s),
                                           unpad(om_s), merge_w, dec_b * dec_n, 1)

    counts = cnt_all[0, :N_EXPERTS].astype(I32)
    padded = (counts + EXPERT_ROWS - 1) // EXPERT_ROWS * EXPERT_ROWS
    pad_end = jnp.cumsum(padded)
    pad_start = pad_end - padded
    n_valid = (pad_end[-1] // EXPERT_ROWS).reshape(1).astype(I32)
    block_row = jnp.arange(n_rows // EXPERT_ROWS, dtype=I32) * EXPERT_ROWS
    block_expert = jnp.minimum(jnp.sum((pad_end[None, :] <= block_row[:, None]).astype(I32), axis=1), N_EXPERTS - 1)
    expert_ids = jnp.arange(N_EXPERTS, dtype=I32)

    def dest(route):
        start = jnp.sum(jnp.where(route[:, :TOP_K, None] == expert_ids, pad_start, 0), axis=-1)
        return (start + route[:, TOP_K:2 * TOP_K]).reshape(-1).astype(I32)

    dest_p, dest_s = dest(route_p), dest(route_s)
    x_rows = _dispatch(dest_p, h_p, x_rows, 512)
    x_rows = _dispatch(dest_s, h_s, x_rows, dec_b * dec_n)
    changed = jnp.concatenate([jnp.zeros((1,), I32), (block_expert[1:] != block_expert[:-1]).astype(I32)])
    block_slot = jnp.cumsum(changed) % 2
    later = (expert_ids[None, :] > expert_ids[:, None]) & (counts[None, :] > 0)
    next_nonempty = jnp.min(jnp.where(later, expert_ids[None, :], N_EXPERTS), axis=1)
    next_nonempty = jnp.where(next_nonempty < N_EXPERTS, next_nonempty, -1)
    block_next = jnp.sum(jnp.where(block_expert[:, None] == expert_ids, next_nonempty, 0), axis=1)
    y_rows = _experts(block_expert, n_valid, block_slot.astype(I32), block_next.astype(I32), x_rows, w_gu[0], b_gu[0],
                      w_down[0], b_down[0])
    y_p = _combine(dest_p, h_p, gate_p, ln2g, ln2b, y_rows, 256)
    y_s = _combine(dest_s, h_s, gate_s, ln2g, ln2b, y_rows, 256)

    kva_rows = _slabs_to_rows(kva_p[:, seq - la:])
    kvb_rows = _slabs_to_rows(kvb_p[:, seq - lb:])
    a_k_p = kva_rows[:, 0:128].reshape(1, 1, la, A_KV_HEADS, HEAD_DIM)
    a_v_p = kva_rows[:, 256:384].reshape(1, 1, la, A_KV_HEADS, HEAD_DIM)
    b_k_p = kvb_rows[:, 0:256].reshape(1, 1, lb, B_KV_HEADS, HEAD_DIM)
    b_v_p = kvb_rows[:, 256:512].reshape(1, 1, lb, B_KV_HEADS, HEAD_DIM)
    m_k_p = mkv_p[:, :M_HEADS * LANES].reshape(1, 1, MEM_TOKENS, M_HEADS, M_HEAD_DIM)
    m_v_p = mkv_p[:, M_HEADS * LANES:].reshape(1, 1, MEM_TOKENS, M_HEADS, M_HEAD_DIM)
    kva_s_rows = unpad(_slabs_to_rows(kva_s))
    kvb_s_rows = unpad(_slabs_to_rows(kvb_s))
    a_k_s = kva_s_rows[:, 0:128].reshape(1, dec_b, dec_n, A_KV_HEADS, HEAD_DIM)
    a_v_s = kva_s_rows[:, 256:384].reshape(1, dec_b, dec_n, A_KV_HEADS, HEAD_DIM)
    b_k_s = kvb_s_rows[:, 0:256].reshape(1, dec_b, dec_n, B_KV_HEADS, HEAD_DIM)
    b_v_s = kvb_s_rows[:, 256:512].reshape(1, dec_b, dec_n, B_KV_HEADS, HEAD_DIM)
    return (y_p.reshape(bsz, seq, D_MODEL), y_s.reshape(dec_b, dec_n, D_MODEL),
            a_k_p, a_v_p, b_k_p, b_v_p, m_k_p, m_v_p, a_k_s, a_v_s, b_k_s, b_v_s)
```
